```python
import math
import jax, jax.numpy as jnp
from jax import lax
import numpy as np

D_MODEL = 1024
BATCH = 16
SEQ = 2048
DEPTH = 2

HEAD_DIM = 64
SB_HEADS = 8
DIFF_HEADS = 4
DIFF_SUB = 64
DIFF_VDIM = 2 * DIFF_SUB
DSA_HEADS = 16
DSA_LATENT = 128
DSA_VDIM = 64
IDX_HEADS = 8
IDX_DIM = 64
TOPK_MAX = 256
N_EXPERTS = 32
TOP_K = 4
D_FF = 1024
SWIGLU_LIMIT = 7.0
SWIGLU_ALPHA = 1.702
Q_BLOCK = 128
LN_EPS = 1e-5
RMS_EPS = 1e-5
DEEPNORM_ALPHA = (2 * DEPTH) ** 0.25
DEEPNORM_BETA = (8 * DEPTH) ** -0.25
N_EVEN = (DEPTH + 1) // 2
N_ODD = DEPTH // 2

SB_W = SB_HEADS * HEAD_DIM
DIFF_QK_W = DIFF_HEADS * 2 * DIFF_SUB
DIFF_V_W = DIFF_HEADS * DIFF_VDIM
EVEN_COLS = 3 * SB_W + 2 * DIFF_QK_W + DIFF_V_W
EVEN_SPLITS = (SB_W, 2 * SB_W, 3 * SB_W, 3 * SB_W + DIFF_QK_W, 3 * SB_W + 2 * DIFF_QK_W)
EVEN_MIX_W = SB_W + DIFF_V_W

DSA_Q_W = DSA_HEADS * DSA_LATENT
ODD_SPLITS = (DSA_Q_W, DSA_Q_W + DSA_LATENT, DSA_Q_W + DSA_LATENT + IDX_HEADS * IDX_DIM,
              DSA_Q_W + DSA_LATENT + IDX_HEADS * IDX_DIM + IDX_DIM)
ODD_COLS = DSA_Q_W + DSA_LATENT + IDX_HEADS * IDX_DIM + IDX_DIM + IDX_HEADS
ODD_MIX_W = DSA_HEADS * DSA_VDIM

kernel_name = 'stickbreak_diff_dsa_moe_deepnorm'


def _alibi_slopes(n):
    return 2.0 ** (-8.0 * jnp.arange(1, n + 1, dtype=jnp.float32) / n)


def _layer_norm(x, g, b):
    xf = x.astype(jnp.float32)
    mu = jnp.mean(xf, axis=-1, keepdims=True)
    var = jnp.mean(jnp.square(xf - mu), axis=-1, keepdims=True)
    y = (xf - mu) * lax.rsqrt(var + LN_EPS) * g.astype(jnp.float32) + b.astype(jnp.float32)
    return y.astype(x.dtype)


def _rms_norm(x, w):
    xf = x.astype(jnp.float32)
    y = xf * lax.rsqrt(jnp.mean(xf * xf, axis=-1, keepdims=True) + RMS_EPS) * w.astype(jnp.float32)
    return y.astype(x.dtype)


def _even_mixer(x, w_in, w_out, lam_q1, lam_k1, lam_q2, lam_k2, subln_w, layer):
    f32 = jnp.float32
    bsz, seq_len, _ = x.shape
    proj = x @ w_in
    q_a, k_a, v_a, q_b, k_b, v_b = jnp.split(proj, list(EVEN_SPLITS), axis=-1)
    q_a = q_a.reshape(bsz, seq_len, SB_HEADS, HEAD_DIM)
    k_a = k_a.reshape(bsz, seq_len, SB_HEADS, HEAD_DIM)
    v_a = v_a.reshape(bsz, seq_len, SB_HEADS, HEAD_DIM)
    q_b = q_b.reshape(bsz, seq_len, DIFF_HEADS, 2, DIFF_SUB)
    k_b = k_b.reshape(bsz, seq_len, DIFF_HEADS, 2, DIFF_SUB)
    v_b = v_b.reshape(bsz, seq_len, DIFF_HEADS, DIFF_VDIM)

    lam_init = 0.8 - 0.6 * math.exp(-0.3 * layer)
    lam = (jnp.exp(jnp.sum(lam_q1.astype(f32) * lam_k1.astype(f32)))
           - jnp.exp(jnp.sum(lam_q2.astype(f32) * lam_k2.astype(f32))) + lam_init)
    slopes = _alibi_slopes(DIFF_HEADS)
    sb_scale = HEAD_DIM ** -0.5
    diff_scale = DIFF_SUB ** -0.5
    key_pos = jnp.arange(seq_len)

    def block(i):
        t0 = i * Q_BLOCK
        t = t0 + jnp.arange(Q_BLOCK)
        dist = t[:, None] - key_pos[None, :]
        qa = lax.dynamic_slice_in_dim(q_a, t0, Q_BLOCK, axis=1)
        qb = lax.dynamic_slice_in_dim(q_b, t0, Q_BLOCK, axis=1)

        z = jnp.einsum('bqhd,bshd->bhqs', qa, k_a).astype(f32) * sb_scale
        strict = dist > 0
        log_keep = jnp.where(strict, jax.nn.log_sigmoid(-z), 0.0)
        log_after = lax.cumsum(log_keep, axis=3, reverse=True) - log_keep
        attn_a = jnp.where(strict, jnp.exp(jax.nn.log_sigmoid(z) + log_after), 0.0)
        o_a = jnp.einsum('bhqs,bshd->bqhd', attn_a.astype(v_a.dtype), v_a)

        s = jnp.einsum('bqhcd,bshcd->bhcqs', qb, k_b).astype(f32) * diff_scale
        s = s - slopes[:, None, None, None] * dist.astype(f32)
        s = jnp.where(dist >= 0, s, -jnp.inf)
        p = jax.nn.softmax(s, axis=-1)
        p = p[:, :, 0] - lam * p[:, :, 1]
        o_b = jnp.einsum('bhqs,bshe->bqhe', p.astype(v_b.dtype), v_b)
        o_b = _rms_norm(o_b, subln_w) * (1.0 - lam_init)

        return jnp.concatenate([o_a.reshape(bsz, Q_BLOCK, SB_W),
                                o_b.reshape(bsz, Q_BLOCK, DIFF_V_W)], axis=-1)

    out = lax.map(block, jnp.arange(seq_len // Q_BLOCK))
    out = jnp.moveaxis(out, 0, 1).reshape(bsz, seq_len, EVEN_MIX_W)
    return out @ w_out


def _odd_mixer(x, w_in, kv_norm_w, w_uv, w_out):
    f32 = jnp.float32
    bsz, seq_len, _ = x.shape
    k_sel = min(TOPK_MAX, seq_len // 4)
    proj = x @ w_in
    q, c_kv, q_idx, k_idx, w_idx = jnp.split(proj, list(ODD_SPLITS), axis=-1)
    q = q.reshape(bsz, seq_len, DSA_HEADS, DSA_LATENT)
    c_kv = _rms_norm(c_kv, kv_norm_w)
    q_idx = q_idx.reshape(bsz, seq_len, IDX_HEADS, IDX_DIM)
    w_idx = w_idx * (IDX_HEADS ** -0.5)
    slopes = _alibi_slopes(DSA_HEADS)
    idx_scale = IDX_DIM ** -0.5
    attn_scale = DSA_LATENT ** -0.5
    key_pos = jnp.arange(seq_len)
    batch_ix = jnp.arange(bsz)[:, None, None]

    def block(i):
        t0 = i * Q_BLOCK
        t = t0 + jnp.arange(Q_BLOCK)
        qi = lax.dynamic_slice_in_dim(q_idx, t0, Q_BLOCK, axis=1)
        wi = lax.dynamic_slice_in_dim(w_idx, t0, Q_BLOCK, axis=1)
        qm = lax.dynamic_slice_in_dim(q, t0, Q_BLOCK, axis=1)
        rel = jax.nn.relu(jnp.einsum('bqhd,bsd->bqhs', qi, k_idx).astype(f32) * idx_scale)
        score = jnp.einsum('bqhs,bqh->bqs', rel, wi.astype(f32))
        score = jnp.where(key_pos[None, None, :] <= t[None, :, None], score, -jnp.inf)
        _, idx = lax.top_k(score, k_sel)
        c_sel = c_kv[batch_ix, idx]
        valid = idx <= t[None, :, None]
        dist = (t[None, :, None] - idx).astype(f32)
        s = jnp.einsum('bqhc,bqkc->bqhk', qm, c_sel).astype(f32) * attn_scale
        s = s - slopes[None, None, :, None] * dist[:, :, None, :]
        s = jnp.where(valid[:, :, None, :], s, -jnp.inf)
        p = jax.nn.softmax(s, axis=-1)
        o = jnp.einsum('bqhk,bqkc->bqhc', p.astype(c_sel.dtype), c_sel)
        o = jnp.einsum('bqhc,hcd->bqhd', o, w_uv)
        return o.reshape(bsz, Q_BLOCK, ODD_MIX_W)

    out = lax.map(block, jnp.arange(seq_len // Q_BLOCK))
    out = jnp.moveaxis(out, 0, 1).reshape(bsz, seq_len, ODD_MIX_W)
    return out @ w_out


def _moe(x, router_w, router_b, w_gu, b_gu, w_down, b_down):
    bsz, seq_len, d = x.shape
    xt = x.reshape(-1, d)
    logits = (xt @ router_w + router_b).astype(jnp.float32)
    top_val, top_idx = lax.top_k(logits, TOP_K)
    gates = jax.nn.softmax(top_val, axis=-1)
    combine = jnp.sum(jax.nn.one_hot(top_idx, N_EXPERTS, dtype=jnp.float32) * gates[..., None], axis=1)

    def expert_step(acc, params):
        wgu, bgu, wd, bd, cw = params
        hgu = xt @ wgu + bgu
        gate = jnp.minimum(hgu[:, :D_FF], SWIGLU_LIMIT)
        up = jnp.clip(hgu[:, D_FF:], -SWIGLU_LIMIT, SWIGLU_LIMIT)
        act = gate * jax.nn.sigmoid(SWIGLU_ALPHA * gate) * (up + 1.0)
        y = act @ wd + bd
        return acc + cw[:, None] * y.astype(jnp.float32), None

    acc0 = jnp.zeros(xt.shape, jnp.float32)
    acc, _ = lax.scan(expert_step, acc0, (w_gu, b_gu, w_down, b_down, combine.T))
    return acc.astype(x.dtype).reshape(bsz, seq_len, d)


def setup_inputs(seed: int = 0) -> dict:
    key = jax.random.key(seed)
    ks = jax.random.split(key, 24)
    f32 = jnp.float32
    beta = DEEPNORM_BETA
    nrm = lambda k, shape: jax.random.normal(k, shape, f32)
    even_col_scale = jnp.concatenate([
        jnp.ones((2 * SB_W,), f32), jnp.full((SB_W,), beta, f32),
        jnp.ones((2 * DIFF_QK_W,), f32), jnp.full((DIFF_V_W,), beta, f32)])
    return {
        'x': nrm(ks[0], (BATCH, SEQ, D_MODEL)),
        'ev_w_in': nrm(ks[1], (N_EVEN, D_MODEL, EVEN_COLS)) * (D_MODEL ** -0.5) * even_col_scale,
        'ev_w_out': nrm(ks[2], (N_EVEN, EVEN_MIX_W, D_MODEL)) * (EVEN_MIX_W ** -0.5) * beta,
        'ev_lambda_q1': nrm(ks[3], (N_EVEN, DIFF_SUB)) * 0.1,
        'ev_lambda_k1': nrm(ks[4], (N_EVEN, DIFF_SUB)) * 0.1,
        'ev_lambda_q2': nrm(ks[5], (N_EVEN, DIFF_SUB)) * 0.1,
        'ev_lambda_k2': nrm(ks[6], (N_EVEN, DIFF_SUB)) * 0.1,
        'ev_subln_w': 1.0 + 0.02 * nrm(ks[7], (N_EVEN, DIFF_VDIM)),
        'od_w_in': nrm(ks[8], (N_ODD, D_MODEL, ODD_COLS)) * (D_MODEL ** -0.5),
        'od_kv_norm_w': 1.0 + 0.02 * nrm(ks[9], (N_ODD, DSA_LATENT)),
        'od_w_uv': nrm(ks[10], (N_ODD, DSA_HEADS, DSA_LATENT, DSA_VDIM)) * (DSA_LATENT ** -0.5) * beta,
        'od_w_out': nrm(ks[11], (N_ODD, ODD_MIX_W, D_MODEL)) * (ODD_MIX_W ** -0.5) * beta,
        'ln_mix_g': 1.0 + 0.02 * nrm(ks[12], (DEPTH, D_MODEL)),
        'ln_mix_b': 0.02 * nrm(ks[13], (DEPTH, D_MODEL)),
        'router_w': nrm(ks[14], (DEPTH, D_MODEL, N_EXPERTS)) * (D_MODEL ** -0.5),
        'router_b': 0.01 * nrm(ks[15], (DEPTH, N_EXPERTS)),
        'exp_w_gu': nrm(ks[16], (DEPTH, N_EXPERTS, D_MODEL, 2 * D_FF)) * (D_MODEL ** -0.5) * beta,
        'exp_b_gu': 0.02 * nrm(ks[17], (DEPTH, N_EXPERTS, 2 * D_FF)),
        'exp_w_down': nrm(ks[18], (DEPTH, N_EXPERTS, D_FF, D_MODEL)) * (D_FF ** -0.5) * beta,
        'exp_b_down': 0.02 * nrm(ks[19], (DEPTH, N_EXPERTS, D_MODEL)),
        'ln_ffn_g': 1.0 + 0.02 * nrm(ks[20], (DEPTH, D_MODEL)),
        'ln_ffn_b': 0.02 * nrm(ks[21], (DEPTH, D_MODEL)),
    }


def reference(x, ev_w_in, ev_w_out, ev_lambda_q1, ev_lambda_k1, ev_lambda_q2, ev_lambda_k2,
              ev_subln_w, od_w_in, od_kv_norm_w, od_w_uv, od_w_out, ln_mix_g, ln_mix_b,
              router_w, router_b, exp_w_gu, exp_b_gu, exp_w_down, exp_b_down,
              ln_ffn_g, ln_ffn_b):
    h = x
    for layer in range(DEPTH):
        j = layer // 2
        if layer % 2 == 0:
            mix = _even_mixer(h, ev_w_in[j], ev_w_out[j], ev_lambda_q1[j], ev_lambda_k1[j],
                              ev_lambda_q2[j], ev_lambda_k2[j], ev_subln_w[j], layer)
        else:
            mix = _odd_mixer(h, od_w_in[j], od_kv_norm_w[j], od_w_uv[j], od_w_out[j])
        h = _layer_norm(DEEPNORM_ALPHA * h + mix, ln_mix_g[layer], ln_mix_b[layer])
        ffn = _moe(h, router_w[layer], router_b[layer], exp_w_gu[layer], exp_b_gu[layer],
                   exp_w_down[layer], exp_b_down[layer])
        h = _layer_norm(DEEPNORM_ALPHA * h + ffn, ln_ffn_g[layer], ln_ffn_b[layer])
    return h
```

```python
import functools
import math

import jax
import jax.numpy as jnp
from jax import lax
from jax.experimental import pallas as pl
from jax.experimental.pallas import tpu as pltpu

F32, BF16, I32 = jnp.float32, jnp.bfloat16, jnp.int32

HEAD_DIM = 64
SB_HEADS = 8
DIFF_HEADS = 4
DIFF_SUB = 64
DIFF_VDIM = 128
DSA_HEADS = 16
DSA_LATENT = 128
DSA_VDIM = 64
IDX_HEADS = 8
IDX_DIM = 64
TOPK_MAX = 256
N_EXPERTS = 32
TOP_K = 4
SWIGLU_LIMIT = 7.0
SWIGLU_ALPHA = 1.702
LN_EPS = 1e-5
RMS_EPS = 1e-5
DEPTH = 2
DEEPNORM_ALPHA = (2 * DEPTH) ** 0.25

LANES = 128
VMEM_LIMIT = 56 * 1024 * 1024
INT_MIN = -(2 ** 31)

TM_PROJ = 512
TQ_ATT = 256
TQ_DSA = 128
TRI = 256
TM_ROUTE = 512
TM_FFN = 512
TD_DISPATCH = 512
TC_COMBINE = 256


def _cparams(*sem):
    return pltpu.CompilerParams(dimension_semantics=sem, vmem_limit_bytes=VMEM_LIMIT)


def _nt_dot(a, b):
    return lax.dot_general(a, b, (((1,), (1,)), ((), ())), preferred_element_type=F32)


def _dot(a, b):
    return jnp.dot(a, b, preferred_element_type=F32)


def _layer_norm(y, g, b):
    mu = jnp.mean(y, axis=-1, keepdims=True)
    d = y - mu
    var = jnp.mean(d * d, axis=-1, keepdims=True)
    return d * lax.rsqrt(var + LN_EPS) * g + b


def _col_chunk(n):
    for c in (512, 384, 256, 128):
        if n % c == 0:
            return c
    raise ValueError(n)


def _proj_kernel(x_ref, w_ref, o_ref):
    x = x_ref[...].astype(BF16)
    n = o_ref.shape[-1]
    c = _col_chunk(n)
    for j in range(0, n, c):
        o_ref[:, j:j + c] = _dot(x, w_ref[:, j:j + c]).astype(o_ref.dtype)


def _project(x, w, out_dtype):
    t, k = x.shape
    n = w.shape[1]
    tm = min(TM_PROJ, t)
    return pl.pallas_call(
        _proj_kernel,
        grid=(t // tm,),
        in_specs=[pl.BlockSpec((tm, k), lambda i: (i, 0)),
                  pl.BlockSpec((k, n), lambda i: (0, 0))],
        out_specs=pl.BlockSpec((tm, n), lambda i: (i, 0)),
        out_shape=jax.ShapeDtypeStruct((t, n), out_dtype),
        compiler_params=_cparams("parallel"),
        name="in_proj",
    )(x, w)


def _sb_kernel(q_ref, k_ref, v_ref, o_ref, *, tq):
    i = pl.program_id(2)
    scale = HEAD_DIM ** -0.5
    lane = lax.broadcasted_iota(I32, (1, LANES), 1)
    row = lax.broadcasted_iota(I32, (tq, tq), 0)
    col = lax.broadcasted_iota(I32, (tq, tq), 1)
    strict = col < row
    after = (row > col).astype(BF16)
    q = q_ref[0]
    acc = jnp.zeros((tq, LANES), F32)
    for hh in range(2):
        head_lanes = (lane // HEAD_DIM) == hh
        qm = jnp.where(head_lanes, q, jnp.zeros_like(q))

        def block(j, carry, masked, qm=qm, head_lanes=head_lanes):
            run, acc = carry
            start = pl.multiple_of(j * tq, tq)
            kb = k_ref[0, pl.ds(start, tq), :]
            vb = v_ref[0, pl.ds(start, tq), :]
            z = _nt_dot(qm, kb) * scale
            log_keep = jnp.minimum(-z, 0.0) - jnp.log1p(jnp.exp(-jnp.abs(z)))
            lk = jnp.where(strict, log_keep, 0.0) if masked else log_keep
            hi = lk.astype(BF16)
            lo = (lk - hi.astype(F32)).astype(BF16)
            log_after = _dot(hi, after) + _dot(lo, after) + run
            w = jnp.exp(z + log_keep + log_after)
            if masked:
                w = jnp.where(strict, w, 0.0)
            vm = jnp.where(head_lanes, vb, jnp.zeros_like(vb))
            acc = acc + _dot(w.astype(BF16), vm)
            run = run + jnp.sum(lk, axis=-1, keepdims=True)
            return run, acc

        carry = block(i, (jnp.zeros((tq, 1), F32), acc), True)
        carry = lax.fori_loop(0, i, lambda jj, c, block=block: block(i - 1 - jj, c, False), carry)
        acc = carry[1]
    o_ref[0] = acc.astype(o_ref.dtype)


def _sb_attention(proj, bsz, seq):
    tq = min(TQ_ATT, seq)
    pairs = SB_HEADS * HEAD_DIM // LANES
    return pl.pallas_call(
        functools.partial(_sb_kernel, tq=tq),
        grid=(bsz, pairs, seq // tq),
        in_specs=[pl.BlockSpec((1, tq, LANES), lambda b, p, i: (b, i, p)),
                  pl.BlockSpec((1, seq, LANES), lambda b, p, i: (b, 0, pairs + p)),
                  pl.BlockSpec((1, seq, LANES), lambda b, p, i: (b, 0, 2 * pairs + p))],
        out_specs=pl.BlockSpec((1, tq, LANES), lambda b, p, i: (b, i, p)),
        out_shape=jax.ShapeDtypeStruct((bsz, seq, SB_HEADS * HEAD_DIM), BF16),
        compiler_params=_cparams("parallel", "parallel", "parallel"),
        name="stickbreak_attn",
    )(proj, proj, proj)


def _diff_kernel(slope_ref, lq1_ref, lk1_ref, lq2_ref, lk2_ref, subw_ref, q_ref, k_ref, v_ref, o_ref,
                 *, tq, lam_init):
    h = pl.program_id(1)
    i = pl.program_id(2)
    scale = DIFF_SUB ** -0.5
    slope = slope_ref[h]
    lam = (jnp.exp(jnp.sum(lq1_ref[...] * lk1_ref[...], keepdims=True))
           - jnp.exp(jnp.sum(lq2_ref[...] * lk2_ref[...], keepdims=True)) + lam_init)
    lane = lax.broadcasted_iota(I32, (1, LANES), 1)
    row = lax.broadcasted_iota(I32, (tq, tq), 0)
    col = lax.broadcasted_iota(I32, (tq, tq), 1)
    causal = col <= row
    d0 = (row - col).astype(F32)
    q = q_ref[0]
    outs = []
    for c in range(2):
        qm = jnp.where((lane // DIFF_SUB) == c, q, jnp.zeros_like(q))

        def block(j, carry, masked, qm=qm):
            m, l, acc = carry
            start = pl.multiple_of(j * tq, tq)
            kb = k_ref[0, pl.ds(start, tq), :]
            vb = v_ref[0, pl.ds(start, tq), :]
            dist = d0 + ((i - j) * tq).astype(F32)
            s = _nt_dot(qm, kb) * scale - slope * dist
            if masked:
                s = jnp.where(causal, s, -jnp.inf)
            m_new = jnp.maximum(m, jnp.max(s, axis=-1, keepdims=True))
            a = jnp.exp(m - m_new)
            p = jnp.exp(s - m_new)
            l = a * l + jnp.sum(p, axis=-1, keepdims=True)
            acc = a * acc + _dot(p.astype(BF16), vb)
            return m_new, l, acc

        init = (jnp.full((tq, 1), -jnp.inf, F32), jnp.zeros((tq, 1), F32), jnp.zeros((tq, LANES), F32))
        carry = block(i, init, True)
        carry = lax.fori_loop(0, i, lambda jj, cr, block=block: block(jj, cr, False), carry)
        outs.append(carry[2] / carry[1])
    o = outs[0] - lam * outs[1]
    o = o * lax.rsqrt(jnp.mean(o * o, axis=-1, keepdims=True) + RMS_EPS) * subw_ref[...]
    o_ref[0] = (o * (1.0 - lam_init)).astype(o_ref.dtype)


def _diff_attention(proj, slopes, lq1, lk1, lq2, lk2, subw, bsz, seq, lam_init):
    tq = min(TQ_ATT, seq)
    q0 = 3 * SB_HEADS * HEAD_DIM // LANES
    k0 = q0 + DIFF_HEADS
    v0 = k0 + DIFF_HEADS
    vec = lambda n: pl.BlockSpec((1, n), lambda b, h, i: (0, 0))
    return pl.pallas_call(
        functools.partial(_diff_kernel, tq=tq, lam_init=lam_init),
        grid=(bsz, DIFF_HEADS, seq // tq),
        in_specs=[pl.BlockSpec(memory_space=pltpu.SMEM),
                  vec(DIFF_SUB), vec(DIFF_SUB), vec(DIFF_SUB), vec(DIFF_SUB), vec(DIFF_VDIM),
                  pl.BlockSpec((1, tq, LANES), lambda b, h, i: (b, i, q0 + h)),
                  pl.BlockSpec((1, seq, LANES), lambda b, h, i: (b, 0, k0 + h)),
                  pl.BlockSpec((1, seq, LANES), lambda b, h, i: (b, 0, v0 + h))],
        out_specs=pl.BlockSpec((1, tq, LANES), lambda b, h, i: (b, i, h)),
        out_shape=jax.ShapeDtypeStruct((bsz, seq, DIFF_HEADS * DIFF_VDIM), BF16),
        compiler_params=_cparams("parallel", "parallel", "parallel"),
        name="diff_attn",
    )(slopes, lq1, lk1, lq2, lk2, subw, proj, proj, proj)


def _dsa_kernel(slope_ref, q_ref, ckv_ref, kvw_ref, qi_ref, ki_ref, wi_ref, o_ref, *, tq, k_sel):
    i = pl.program_id(1)
    seq = ckv_ref.shape[1]
    t0 = i * tq
    c = ckv_ref[0]
    cn = (c * lax.rsqrt(jnp.mean(c * c, axis=-1, keepdims=True) + RMS_EPS) * kvw_ref[...]).astype(BF16)

    lane = lax.broadcasted_iota(I32, (1, LANES), 1)
    ki = ki_ref[0]
    wi = wi_ref[0] * (IDX_HEADS ** -0.5)
    idx_scale = IDX_DIM ** -0.5
    score = jnp.zeros((tq, seq), F32)
    for h in range(IDX_HEADS):
        qp = qi_ref[0, :, (h // 2) * LANES:(h // 2 + 1) * LANES]
        qm = jnp.where((lane // IDX_DIM) == (h % 2), qp, jnp.zeros_like(qp))
        rel = jnp.maximum(_nt_dot(qm, ki) * idx_scale, 0.0)
        score = score + rel * wi[:, h:h + 1]

    key_pos = lax.broadcasted_iota(I32, (tq, seq), 1)
    q_pos = t0 + lax.broadcasted_iota(I32, (tq, seq), 0)
    causal = key_pos <= q_pos
    score = jnp.where(score == 0.0, 0.0, score)
    bits = pltpu.bitcast(score, I32)
    skey = jnp.where(bits < 0, bits ^ 0x7FFFFFFF, bits)
    skey = jnp.where(causal, skey, INT_MIN)

    def count_ge(cand):
        return jnp.sum((skey >= cand).astype(F32), axis=-1, keepdims=True)

    kf = float(k_sel)
    thr = jnp.where(count_ge(jnp.zeros((tq, 1), I32)) >= kf, 0, INT_MIN).astype(I32)

    def bit_step(b, thr):
        cand = thr + jnp.left_shift(jnp.int32(1), 30 - b)
        return jnp.where(count_ge(cand) >= kf, cand, thr)

    thr = lax.fori_loop(0, 31, bit_step, thr)

    above = skey > thr
    tied = skey == thr
    need = kf - jnp.sum(above.astype(F32), axis=-1, keepdims=True)
    r2 = lax.broadcasted_iota(I32, (TRI, TRI), 0)
    c2 = lax.broadcasted_iota(I32, (TRI, TRI), 1)
    upto = (r2 <= c2).astype(BF16)
    seen = jnp.zeros((tq, 1), F32)
    bias_blocks = []
    for jb in range(seq // TRI):
        blk = slice(jb * TRI, (jb + 1) * TRI)
        tb = tied[:, blk]
        prefix = _dot(tb.astype(F32).astype(BF16), upto) + seen
        selected = (above[:, blk] | (tb & (prefix <= need))) & causal[:, blk]
        bias_blocks.append(jnp.where(selected, 0.0, -jnp.inf))
        seen = prefix[:, TRI - 1:TRI]
    mask_bias = jnp.concatenate(bias_blocks, axis=1)
    dist = (q_pos - key_pos).astype(F32)
    attn_scale = DSA_LATENT ** -0.5

    def head(h, _):
        off = pl.multiple_of(h * DSA_LATENT, DSA_LATENT)
        qh = q_ref[0, :, pl.ds(off, DSA_LATENT)]
        s = _nt_dot(qh, cn) * attn_scale - slope_ref[h] * dist + mask_bias
        m = jnp.max(s, axis=-1, keepdims=True)
        p = jnp.exp(s - m)
        l = jnp.sum(p, axis=-1, keepdims=True)
        o = _dot(p.astype(BF16), cn) / l
        o_ref[0, :, pl.ds(off, DSA_LATENT)] = o.astype(o_ref.dtype)
        return 0

    lax.fori_loop(0, DSA_HEADS, head, 0)


def _dsa_attention(slopes, proj, small, kvw, bsz, seq):
    tq = min(TQ_DSA, seq)
    k_sel = min(TOPK_MAX, seq // 4)
    qw = DSA_HEADS * DSA_LATENT
    qiw = IDX_HEADS * IDX_DIM
    return pl.pallas_call(
        functools.partial(_dsa_kernel, tq=tq, k_sel=k_sel),
        grid=(bsz, seq // tq),
        in_specs=[pl.BlockSpec(memory_space=pltpu.SMEM),
                  pl.BlockSpec((1, tq, qw), lambda b, i: (b, i, 0)),
                  pl.BlockSpec((1, seq, DSA_LATENT), lambda b, i: (b, 0, 0)),
                  pl.BlockSpec((1, DSA_LATENT), lambda b, i: (0, 0)),
                  pl.BlockSpec((1, tq, qiw), lambda b, i: (b, i, qw // qiw)),
                  pl.BlockSpec((1, seq, LANES), lambda b, i: (b, 0, (qw + qiw) // LANES)),
                  pl.BlockSpec((1, tq, LANES), lambda b, i: (b, i, 1))],
        out_specs=pl.BlockSpec((1, tq, qw), lambda b, i: (b, i, 0)),
        out_shape=jax.ShapeDtypeStruct((bsz, seq, qw), BF16),
        compiler_params=_cparams("parallel", "parallel"),
        name="dsa_attn",
    )(slopes, proj, small, kvw, proj, proj, small)


def _mix_tail(mix, h_ref, g_ref, b_ref, rw_ref, rb_ref, hout_ref, logit_ref):
    hn = _layer_norm(DEEPNORM_ALPHA * h_ref[...] + mix, g_ref[...], b_ref[...])
    hout_ref[...] = hn
    logit_ref[...] = _dot(hn.astype(BF16), rw_ref[...]) + rb_ref[...]


def _mix_even_kernel(oa_ref, ob_ref, wa_ref, wb_ref, h_ref, g_ref, b_ref, rw_ref, rb_ref, hout_ref, logit_ref):
    mix = _dot(oa_ref[...], wa_ref[...]) + _dot(ob_ref[...], wb_ref[...])
    _mix_tail(mix, h_ref, g_ref, b_ref, rw_ref, rb_ref, hout_ref, logit_ref)


def _mix_odd_kernel(o_ref, wuv_ref, wo_ref, h_ref, g_ref, b_ref, rw_ref, rb_ref, hout_ref, logit_ref):
    up = _dot(o_ref[...], wuv_ref[...]).astype(BF16)
    _mix_tail(_dot(up, wo_ref[...]), h_ref, g_ref, b_ref, rw_ref, rb_ref, hout_ref, logit_ref)


def _mix_call(kernel, name, acts, weights, h, g, b, rw, rb):
    t, d = h.shape
    tm = min(TM_PROJ, t)
    row = lambda n: pl.BlockSpec((tm, n), lambda i: (i, 0))
    full = lambda a: pl.BlockSpec(a.shape, lambda i: (0, 0))
    return pl.pallas_call(
        kernel,
        grid=(t // tm,),
        in_specs=[row(a.shape[1]) for a in acts] + [full(w) for w in weights]
                 + [row(d), full(g), full(b), full(rw), full(rb)],
        out_specs=[row(d), row(LANES)],
        out_shape=[jax.ShapeDtypeStruct((t, d), F32), jax.ShapeDtypeStruct((t, LANES), F32)],
        compiler_params=_cparams("parallel"),
        name=name,
    )(*acts, *weights, h, g, b, rw, rb)


def _route_kernel(logit_ref, idx_ref, gate_ref, rank_ref, cnt_ref, *, tm):
    @pl.when(pl.program_id(0) == 0)
    def _():
        cnt_ref[...] = jnp.zeros_like(cnt_ref)

    lane = lax.broadcasted_iota(I32, (tm, LANES), 1)
    lane_f = lane.astype(F32)
    x = jnp.where(lane < N_EXPERTS, logit_ref[...], -jnp.inf)
    vals, hots = [], []
    for k in range(TOP_K):
        m = jnp.max(x, axis=-1, keepdims=True)
        first = jnp.min(jnp.where(x == m, lane_f, float(LANES)), axis=-1, keepdims=True)
        hot = lane_f == first
        x = jnp.where(hot, -jnp.inf, x)
        vals.append(m)
        hots.append(hot)
        idx_ref[:, k:k + 1] = first.astype(I32)
    exps = [jnp.exp(v - vals[0]) for v in vals]
    denom = exps[0] + exps[1] + exps[2] + exps[3]
    for k in range(TOP_K):
        gate_ref[:, k:k + 1] = exps[k] / denom

    hot_sum = (hots[0] | hots[1] | hots[2] | hots[3]).astype(F32)
    r = lax.broadcasted_iota(I32, (tm, tm), 0)
    c = lax.broadcasted_iota(I32, (tm, tm), 1)
    earlier = (c < r).astype(BF16)
    before = _dot(earlier, hot_sum.astype(BF16)) + cnt_ref[...]
    for k in range(TOP_K):
        rank_ref[:, k:k + 1] = jnp.sum(jnp.where(hots[k], before, 0.0), axis=-1, keepdims=True).astype(I32)
    cnt_ref[...] += jnp.sum(hot_sum, axis=0, keepdims=True)


def _route(logits):
    t = logits.shape[0]
    tm = min(TM_ROUTE, t)
    narrow = pl.BlockSpec((tm, TOP_K), lambda i: (i, 0))
    return pl.pallas_call(
        functools.partial(_route_kernel, tm=tm),
        grid=(t // tm,),
        in_specs=[pl.BlockSpec((tm, LANES), lambda i: (i, 0))],
        out_specs=[narrow, narrow, narrow, pl.BlockSpec((1, LANES), lambda i: (0, 0))],
        out_shape=[jax.ShapeDtypeStruct((t, TOP_K), I32), jax.ShapeDtypeStruct((t, TOP_K), F32),
                   jax.ShapeDtypeStruct((t, TOP_K), I32), jax.ShapeDtypeStruct((1, LANES), F32)],
        compiler_params=_cparams("arbitrary"),
        name="route",
    )(logits)


def _row_copy(src_hbm, src_row, dst_ref, dst_row, sem):
    return pltpu.make_async_copy(src_hbm.at[pl.ds(src_row, 1), :], dst_ref.at[pl.ds(dst_row, 1), :], sem)


def _dispatch_kernel(pos_ref, x_hbm, xs_in, xs_out, sem, *, td):
    del xs_in
    base = pl.program_id(0) * td

    def issue(i, _):
        for k in range(TOP_K):
            _row_copy(x_hbm, base + i, xs_out, pos_ref[i * TOP_K + k], sem).start()
        return 0

    lax.fori_loop(0, td, issue, 0)

    def drain(i, _):
        for k in range(TOP_K):
            _row_copy(x_hbm, base + i, xs_out, pos_ref[i * TOP_K + k], sem).wait()
        return 0

    lax.fori_loop(0, td, drain, 0)


def _dispatch(pos_flat, x, n_rows):
    t, d = x.shape
    td = min(TD_DISPATCH, t)
    zeros = jnp.zeros((n_rows, d), x.dtype)
    return pl.pallas_call(
        functools.partial(_dispatch_kernel, td=td),
        grid=(t // td,),
        in_specs=[pl.BlockSpec((td * TOP_K,), lambda i: (i,), memory_space=pltpu.SMEM),
                  pl.BlockSpec(memory_space=pl.ANY),
                  pl.BlockSpec(memory_space=pl.ANY)],
        out_specs=pl.BlockSpec(memory_space=pl.ANY),
        out_shape=jax.ShapeDtypeStruct((n_rows, d), x.dtype),
        scratch_shapes=[pltpu.SemaphoreType.DMA],
        input_output_aliases={2: 0},
        compiler_params=_cparams("arbitrary"),
        name="moe_dispatch",
    )(pos_flat, x, zeros)


def _ffn_kernel(tile_expert_ref, n_used_ref, xs_ref, wgu_ref, bgu_ref, wd_ref, bd_ref, y_ref, wgu_bf, wd_bf):
    i = pl.program_id(0)
    d_ff = wd_ref.shape[2]
    new_expert = jnp.logical_or(i == 0, tile_expert_ref[i] != tile_expert_ref[jnp.maximum(i - 1, 0)])

    @pl.when(jnp.logical_and(new_expert, i < n_used_ref[0]))
    def _():
        wgu_bf[...] = wgu_ref[0, 0].astype(BF16)
        wd_bf[...] = wd_ref[0, 0].astype(BF16)

    @pl.when(i < n_used_ref[0])
    def _():
        x = xs_ref[...].astype(BF16)
        hgu = _dot(x, wgu_bf[...]) + bgu_ref[0, 0]
        gate = jnp.minimum(hgu[:, :d_ff], SWIGLU_LIMIT)
        up = jnp.clip(hgu[:, d_ff:], -SWIGLU_LIMIT, SWIGLU_LIMIT)
        act = gate * (1.0 / (1.0 + jnp.exp(-SWIGLU_ALPHA * gate))) * (up + 1.0)
        y_ref[...] = _dot(act.astype(BF16), wd_bf[...]) + bd_ref[0, 0]

    @pl.when(i >= n_used_ref[0])
    def _():
        y_ref[...] = jnp.zeros_like(y_ref)


def _expert_ffn(tile_expert, n_used, xs, w_gu, b_gu, w_down, b_down, layer):
    n_rows, d = xs.shape
    depth, e, _, f2 = w_gu.shape
    d_ff = w_down.shape[2]
    tm = TM_FFN
    grid_spec = pltpu.PrefetchScalarGridSpec(
        num_scalar_prefetch=2,
        grid=(n_rows // tm,),
        in_specs=[pl.BlockSpec((tm, d), lambda i, te, nu: (i, 0)),
                  pl.BlockSpec((1, 1, d, f2), lambda i, te, nu: (layer, te[i], 0, 0)),
                  pl.BlockSpec((1, 1, 1, f2), lambda i, te, nu: (layer, te[i], 0, 0)),
                  pl.BlockSpec((1, 1, d_ff, d), lambda i, te, nu: (layer, te[i], 0, 0)),
                  pl.BlockSpec((1, 1, 1, d), lambda i, te, nu: (layer, te[i], 0, 0))],
        out_specs=pl.BlockSpec((tm, d), lambda i, te, nu: (i, 0)),
        scratch_shapes=[pltpu.VMEM((d, f2), BF16), pltpu.VMEM((d_ff, d), BF16)],
    )
    return pl.pallas_call(
        _ffn_kernel,
        grid_spec=grid_spec,
        out_shape=jax.ShapeDtypeStruct((n_rows, d), F32),
        compiler_params=_cparams("arbitrary"),
        name="expert_ffn",
    )(tile_expert, n_used, xs, w_gu, b_gu.reshape(depth, e, 1, f2), w_down, b_down.reshape(depth, e, 1, d))


def _combine_kernel(pos_ref, gate_ref, h_ref, g_ref, b_ref, y_hbm, out_ref, buf, sem, *, tc):
    def issue(i, _):
        for k in range(TOP_K):
            _row_copy(y_hbm, pos_ref[i * TOP_K + k], buf.at[k], i, sem).start()
        return 0

    lax.fori_loop(0, tc, issue, 0)

    def drain(i, _):
        for k in range(TOP_K):
            _row_copy(y_hbm, pos_ref[i * TOP_K + k], buf.at[k], i, sem).wait()
        return 0

    lax.fori_loop(0, tc, drain, 0)

    gates = gate_ref[...]
    ffn = gates[:, 0:1] * buf[0]
    for k in range(1, TOP_K):
        ffn = ffn + gates[:, k:k + 1] * buf[k]
    out_ref[...] = _layer_norm(DEEPNORM_ALPHA * h_ref[...] + ffn, g_ref[...], b_ref[...])


def _combine(pos_flat, gates, h, g, b, y):
    t, d = h.shape
    tc = min(TC_COMBINE, t)
    return pl.pallas_call(
        functools.partial(_combine_kernel, tc=tc),
        grid=(t // tc,),
        in_specs=[pl.BlockSpec((tc * TOP_K,), lambda i: (i,), memory_space=pltpu.SMEM),
                  pl.BlockSpec((tc, TOP_K), lambda i: (i, 0)),
                  pl.BlockSpec((tc, d), lambda i: (i, 0)),
                  pl.BlockSpec((1, d), lambda i: (0, 0)),
                  pl.BlockSpec((1, d), lambda i: (0, 0)),
                  pl.BlockSpec(memory_space=pl.ANY)],
        out_specs=pl.BlockSpec((tc, d), lambda i: (i, 0)),
        out_shape=jax.ShapeDtypeStruct((t, d), F32),
        scratch_shapes=[pltpu.VMEM((TOP_K, tc, d), F32), pltpu.SemaphoreType.DMA],
        compiler_params=_cparams("arbitrary"),
        name="moe_combine",
    )(pos_flat, gates, h, g, b, y)


def _moe(h, logits, w_gu, b_gu, w_down, b_down, g, b, layer):
    t = h.shape[0]
    idx, gates, rank, counts = _route(logits)
    counts = counts[0, :N_EXPERTS].astype(I32)
    tiles = (counts + TM_FFN - 1) // TM_FFN
    tile_end = jnp.cumsum(tiles)
    start = (tile_end - tiles) * TM_FFN
    n_tiles = (t * TOP_K) // TM_FFN + N_EXPERTS
    tile_expert = jnp.minimum(jnp.searchsorted(tile_end, jnp.arange(n_tiles, dtype=I32), side="right"),
                              N_EXPERTS - 1).astype(I32)
    pos = (start[idx] + rank).reshape(-1).astype(I32)
    xs = _dispatch(pos, h, n_tiles * TM_FFN)
    y = _expert_ffn(tile_expert, tile_end[-1:].astype(I32), xs, w_gu, b_gu, w_down, b_down, layer)
    return _combine(pos, gates, h, g, b, y)


def _router_params(router_w, router_b):
    rw = jnp.pad(router_w, ((0, 0), (0, LANES - N_EXPERTS))).astype(BF16)
    rb = jnp.pad(router_b, (0, LANES - N_EXPERTS)).reshape(1, LANES)
    return rw, rb


def _alibi_slopes(n):
    return 2.0 ** (-8.0 * jnp.arange(1, n + 1, dtype=F32) / n)


def _even_layer(h, bsz, seq, w_in, w_out, lq1, lk1, lq2, lk2, subw, layer, g, b, rw, rb):
    proj = _project(h, w_in.astype(BF16), BF16).reshape(bsz, seq, -1)
    o_a = _sb_attention(proj, bsz, seq)
    lam_init = 0.8 - 0.6 * math.exp(-0.3 * layer)
    row = lambda v: v.reshape(1, -1)
    o_b = _diff_attention(proj, _alibi_slopes(DIFF_HEADS), row(lq1), row(lk1), row(lq2), row(lk2), row(subw),
                          bsz, seq, lam_init)
    t = bsz * seq
    sbw = SB_HEADS * HEAD_DIM
    wo = w_out.astype(BF16)
    return _mix_call(_mix_even_kernel, "mix_even", [o_a.reshape(t, -1), o_b.reshape(t, -1)],
                     [wo[:sbw], wo[sbw:]], h, row(g), row(b), rw, rb)


def _odd_layer(h, bsz, seq, w_in, kvw, w_uv, w_out, g, b, rw, rb):
    qw = DSA_HEADS * DSA_LATENT
    c0, c1 = qw, qw + DSA_LATENT
    i0, i1 = c1, c1 + IDX_HEADS * IDX_DIM
    k1 = i1 + IDX_DIM
    w_main = jnp.concatenate([w_in[:, :qw], w_in[:, i0:i1], w_in[:, i1:k1], w_in[:, i1:k1]], axis=1).astype(BF16)
    w_small = jnp.pad(jnp.concatenate([w_in[:, c0:c1], w_in[:, k1:]], axis=1),
                      ((0, 0), (0, LANES - IDX_HEADS))).astype(BF16)
    proj = _project(h, w_main, BF16).reshape(bsz, seq, -1)
    small = _project(h, w_small, F32).reshape(bsz, seq, -1)
    o = _dsa_attention(_alibi_slopes(DSA_HEADS), proj, small, kvw.reshape(1, -1), bsz, seq)
    eye = jnp.eye(DSA_HEADS, dtype=w_uv.dtype)
    w_bd = (w_uv[:, :, None, :] * eye[:, None, :, None]).reshape(qw, DSA_HEADS * DSA_VDIM).astype(BF16)
    row = lambda v: v.reshape(1, -1)
    return _mix_call(_mix_odd_kernel, "mix_odd", [o.reshape(bsz * seq, -1)], [w_bd, w_out.astype(BF16)],
                     h, row(g), row(b), rw, rb)


def kernel(x, ev_w_in, ev_w_out, ev_lambda_q1, ev_lambda_k1, ev_lambda_q2, ev_lambda_k2, ev_subln_w, od_w_in, od_kv_norm_w, od_w_uv, od_w_out, ln_mix_g, ln_mix_b, router_w, router_b, exp_w_gu, exp_b_gu, exp_w_down, exp_b_down, ln_ffn_g, ln_ffn_b):
    bsz, seq, d = x.shape
    h = x.reshape(bsz * seq, d)
    for layer in range(ln_mix_g.shape[0]):
        j = layer // 2
        rw, rb = _router_params(router_w[layer], router_b[layer])
        if layer % 2 == 0:
            h, logits = _even_layer(h, bsz, seq, ev_w_in[j], ev_w_out[j], ev_lambda_q1[j], ev_lambda_k1[j],
                                    ev_lambda_q2[j], ev_lambda_k2[j], ev_subln_w[j], layer,
                                    ln_mix_g[layer], ln_mix_b[layer], rw, rb)
        else:
            h, logits = _odd_layer(h, bsz, seq, od_w_in[j], od_kv_norm_w[j], od_w_uv[j], od_w_out[j],
                                   ln_mix_g[layer], ln_mix_b[layer], rw, rb)
        h = _moe(h, logits, exp_w_gu, exp_b_gu, exp_w_down, exp_b_down,
                 ln_ffn_g[layer].reshape(1, -1), ln_ffn_b[layer].reshape(1, -1), layer)
    return h.reshape(bsz, seq, d)
```

```python
import functools
import math

import jax
import jax.numpy as jnp
from jax import lax
from jax.experimental import pallas as pl
from jax.experimental.pallas import tpu as pltpu

F32, BF16, I32 = jnp.float32, jnp.bfloat16, jnp.int32

HEAD_DIM = 64
SB_HEADS = 8
DIFF_HEADS = 4
DIFF_SUB = 64
DIFF_VDIM = 128
DSA_HEADS = 16
DSA_LATENT = 128
DSA_VDIM = 64
IDX_HEADS = 8
IDX_DIM = 64
TOPK_MAX = 256
N_EXPERTS = 32
TOP_K = 4
SWIGLU_LIMIT = 7.0
SWIGLU_ALPHA = 1.702
LN_EPS = 1e-5
RMS_EPS = 1e-5
DEPTH = 2
DEEPNORM_ALPHA = (2 * DEPTH) ** 0.25

LANES = 128
VMEM_LIMIT = 56 * 1024 * 1024
INT_MIN = -(2 ** 31)

TM_PROJ = 512
TQ_ATT = 256
TQ_DSA = 128
TRI = 256
TM_ROUTE = 512
TM_FFN = 512
TD_DISPATCH = 512
TC_COMBINE = 256


def _cparams(*sem):
    return pltpu.CompilerParams(dimension_semantics=sem, vmem_limit_bytes=VMEM_LIMIT)


def _nt_dot(a, b):
    return lax.dot_general(a, b, (((1,), (1,)), ((), ())), preferred_element_type=F32)


def _dot(a, b):
    return jnp.dot(a, b, preferred_element_type=F32)


def _layer_norm(y, g, b):
    mu = jnp.mean(y, axis=-1, keepdims=True)
    d = y - mu
    var = jnp.mean(d * d, axis=-1, keepdims=True)
    return d * lax.rsqrt(var + LN_EPS) * g + b


def _col_chunk(n):
    for c in (512, 384, 256, 128):
        if n % c == 0:
            return c
    raise ValueError(n)


def _proj_kernel(x_ref, w_ref, o_ref):
    x = x_ref[...].astype(BF16)
    n = o_ref.shape[-1]
    c = _col_chunk(n)
    for j in range(0, n, c):
        o_ref[:, j:j + c] = _dot(x, w_ref[:, j:j + c]).astype(o_ref.dtype)


def _project(x, w, out_dtype):
    t, k = x.shape
    n = w.shape[1]
    tm = min(TM_PROJ, t)
    return pl.pallas_call(
        _proj_kernel,
        grid=(t // tm,),
        in_specs=[pl.BlockSpec((tm, k), lambda i: (i, 0)),
                  pl.BlockSpec((k, n), lambda i: (0, 0))],
        out_specs=pl.BlockSpec((tm, n), lambda i: (i, 0)),
        out_shape=jax.ShapeDtypeStruct((t, n), out_dtype),
        compiler_params=_cparams("parallel"),
        name="in_proj",
    )(x, w)


def _sb_kernel(q_ref, k_ref, v_ref, o_ref, *, tq):
    i = pl.program_id(2)
    scale = HEAD_DIM ** -0.5
    lane = lax.broadcasted_iota(I32, (1, LANES), 1)
    row = lax.broadcasted_iota(I32, (tq, tq), 0)
    col = lax.broadcasted_iota(I32, (tq, tq), 1)
    strict = col < row
    after = (row > col).astype(BF16)
    q = q_ref[0]
    acc = jnp.zeros((tq, LANES), F32)
    for hh in range(2):
        head_lanes = (lane // HEAD_DIM) == hh
        qm = jnp.where(head_lanes, q, jnp.zeros_like(q))

        def block(j, carry, masked, qm=qm, head_lanes=head_lanes):
            run, acc = carry
            start = pl.multiple_of(j * tq, tq)
            kb = k_ref[0, pl.ds(start, tq), :]
            vb = v_ref[0, pl.ds(start, tq), :]
            z = _nt_dot(qm, kb) * scale
            log_keep = jnp.minimum(-z, 0.0) - jnp.log1p(jnp.exp(-jnp.abs(z)))
            lk = jnp.where(strict, log_keep, 0.0) if masked else log_keep
            hi = lk.astype(BF16)
            lo = (lk - hi.astype(F32)).astype(BF16)
            log_after = _dot(hi, after) + _dot(lo, after) + run
            w = jnp.exp(z + log_keep + log_after)
            if masked:
                w = jnp.where(strict, w, 0.0)
            vm = jnp.where(head_lanes, vb, jnp.zeros_like(vb))
            acc = acc + _dot(w.astype(BF16), vm)
            run = run + jnp.sum(lk, axis=-1, keepdims=True)
            return run, acc

        carry = block(i, (jnp.zeros((tq, 1), F32), acc), True)
        carry = lax.fori_loop(0, i, lambda jj, c, block=block: block(i - 1 - jj, c, False), carry)
        acc = carry[1]
    o_ref[0] = acc.astype(o_ref.dtype)


def _sb_attention(proj, bsz, seq):
    tq = min(TQ_ATT, seq)
    pairs = SB_HEADS * HEAD_DIM // LANES
    return pl.pallas_call(
        functools.partial(_sb_kernel, tq=tq),
        grid=(bsz, pairs, seq // tq),
        in_specs=[pl.BlockSpec((1, tq, LANES), lambda b, p, i: (b, i, p)),
                  pl.BlockSpec((1, seq, LANES), lambda b, p, i: (b, 0, pairs + p)),
                  pl.BlockSpec((1, seq, LANES), lambda b, p, i: (b, 0, 2 * pairs + p))],
        out_specs=pl.BlockSpec((1, tq, LANES), lambda b, p, i: (b, i, p)),
        out_shape=jax.ShapeDtypeStruct((bsz, seq, SB_HEADS * HEAD_DIM), BF16),
        compiler_params=_cparams("parallel", "parallel", "parallel"),
        name="stickbreak_attn",
    )(proj, proj, proj)


def _diff_kernel(slope_ref, lq1_ref, lk1_ref, lq2_ref, lk2_ref, subw_ref, q_ref, k_ref, v_ref, o_ref,
                 *, tq, lam_init):
    h = pl.program_id(1)
    i = pl.program_id(2)
    scale = DIFF_SUB ** -0.5
    slope = slope_ref[h]
    lam = (jnp.exp(jnp.sum(lq1_ref[...] * lk1_ref[...], keepdims=True))
           - jnp.exp(jnp.sum(lq2_ref[...] * lk2_ref[...], keepdims=True)) + lam_init)
    lane = lax.broadcasted_iota(I32, (1, LANES), 1)
    row = lax.broadcasted_iota(I32, (tq, tq), 0)
    col = lax.broadcasted_iota(I32, (tq, tq), 1)
    causal = col <= row
    d0 = (row - col).astype(F32)
    q = q_ref[0]
    outs = []
    for c in range(2):
        qm = jnp.where((lane // DIFF_SUB) == c, q, jnp.zeros_like(q))

        def block(j, carry, masked, qm=qm):
            m, l, acc = carry
            start = pl.multiple_of(j * tq, tq)
            kb = k_ref[0, pl.ds(start, tq), :]
            vb = v_ref[0, pl.ds(start, tq), :]
            dist = d0 + ((i - j) * tq).astype(F32)
            s = _nt_dot(qm, kb) * scale - slope * dist
            if masked:
                s = jnp.where(causal, s, -jnp.inf)
            m_new = jnp.maximum(m, jnp.max(s, axis=-1, keepdims=True))
            a = jnp.exp(m - m_new)
            p = jnp.exp(s - m_new)
            l = a * l + jnp.sum(p, axis=-1, keepdims=True)
            acc = a * acc + _dot(p.astype(BF16), vb)
            return m_new, l, acc

        init = (jnp.full((tq, 1), -jnp.inf, F32), jnp.zeros((tq, 1), F32), jnp.zeros((tq, LANES), F32))
        carry = block(i, init, True)
        carry = lax.fori_loop(0, i, lambda jj, cr, block=block: block(jj, cr, False), carry)
        outs.append(carry[2] / carry[1])
    o = outs[0] - lam * outs[1]
    o = o * lax.rsqrt(jnp.mean(o * o, axis=-1, keepdims=True) + RMS_EPS) * subw_ref[...]
    o_ref[0] = (o * (1.0 - lam_init)).astype(o_ref.dtype)


def _diff_attention(proj, slopes, lq1, lk1, lq2, lk2, subw, bsz, seq, lam_init):
    tq = min(TQ_ATT, seq)
    q0 = 3 * SB_HEADS * HEAD_DIM // LANES
    k0 = q0 + DIFF_HEADS
    v0 = k0 + DIFF_HEADS
    vec = lambda n: pl.BlockSpec((1, n), lambda b, h, i: (0, 0))
    return pl.pallas_call(
        functools.partial(_diff_kernel, tq=tq, lam_init=lam_init),
        grid=(bsz, DIFF_HEADS, seq // tq),
        in_specs=[pl.BlockSpec(memory_space=pltpu.SMEM),
                  vec(DIFF_SUB), vec(DIFF_SUB), vec(DIFF_SUB), vec(DIFF_SUB), vec(DIFF_VDIM),
                  pl.BlockSpec((1, tq, LANES), lambda b, h, i: (b, i, q0 + h)),
                  pl.BlockSpec((1, seq, LANES), lambda b, h, i: (b, 0, k0 + h)),
                  pl.BlockSpec((1, seq, LANES), lambda b, h, i: (b, 0, v0 + h))],
        out_specs=pl.BlockSpec((1, tq, LANES), lambda b, h, i: (b, i, h)),
        out_shape=jax.ShapeDtypeStruct((bsz, seq, DIFF_HEADS * DIFF_VDIM), BF16),
        compiler_params=_cparams("parallel", "parallel", "parallel"),
        name="diff_attn",
    )(slopes, lq1, lk1, lq2, lk2, subw, proj, proj, proj)


def _dsa_kernel(slope_ref, q_ref, ckv_ref, kvw_ref, qi_ref, ki_ref, wi_ref, o_ref, *, tq, k_sel):
    i = pl.program_id(1)
    seq = ckv_ref.shape[1]
    t0 = i * tq
    c = ckv_ref[0]
    cn = (c * lax.rsqrt(jnp.mean(c * c, axis=-1, keepdims=True) + RMS_EPS) * kvw_ref[...]).astype(BF16)

    lane = lax.broadcasted_iota(I32, (1, LANES), 1)
    ki = ki_ref[0]
    wi = wi_ref[0] * (IDX_HEADS ** -0.5)
    idx_scale = IDX_DIM ** -0.5
    score = jnp.zeros((tq, seq), F32)
    for h in range(IDX_HEADS):
        qp = qi_ref[0, :, (h // 2) * LANES:(h // 2 + 1) * LANES]
        qm = jnp.where((lane // IDX_DIM) == (h % 2), qp, jnp.zeros_like(qp))
        rel = jnp.maximum(_nt_dot(qm, ki) * idx_scale, 0.0)
        score = score + rel * wi[:, h:h + 1]

    key_pos = lax.broadcasted_iota(I32, (tq, seq), 1)
    q_pos = t0 + lax.broadcasted_iota(I32, (tq, seq), 0)
    causal = key_pos <= q_pos
    score = jnp.where(score == 0.0, 0.0, score)
    bits = pltpu.bitcast(score, I32)
    skey = jnp.where(bits < 0, bits ^ 0x7FFFFFFF, bits)
    skey = jnp.where(causal, skey, INT_MIN)

    def count_ge(cand):
        return jnp.sum((skey >= cand).astype(F32), axis=-1, keepdims=True)

    kf = float(k_sel)
    thr = jnp.where(count_ge(jnp.zeros((tq, 1), I32)) >= kf, 0, INT_MIN).astype(I32)

    def bit_step(b, thr):
        cand = thr + jnp.left_shift(jnp.int32(1), 30 - b)
        return jnp.where(count_ge(cand) >= kf, cand, thr)

    thr = lax.fori_loop(0, 31, bit_step, thr)

    above = skey > thr
    tied = skey == thr
    need = kf - jnp.sum(above.astype(F32), axis=-1, keepdims=True)
    r2 = lax.broadcasted_iota(I32, (TRI, TRI), 0)
    c2 = lax.broadcasted_iota(I32, (TRI, TRI), 1)
    upto = (r2 <= c2).astype(BF16)
    seen = jnp.zeros((tq, 1), F32)
    bias_blocks = []
    for jb in range(seq // TRI):
        blk = slice(jb * TRI, (jb + 1) * TRI)
        tb = tied[:, blk]
        prefix = _dot(tb.astype(F32).astype(BF16), upto) + seen
        selected = (above[:, blk] | (tb & (prefix <= need))) & causal[:, blk]
        bias_blocks.append(jnp.where(selected, 0.0, -jnp.inf))
        seen = prefix[:, TRI - 1:TRI]
    mask_bias = jnp.concatenate(bias_blocks, axis=1)
    dist = (q_pos - key_pos).astype(F32)
    attn_scale = DSA_LATENT ** -0.5

    def head(h, _):
        off = pl.multiple_of(h * DSA_LATENT, DSA_LATENT)
        qh = q_ref[0, :, pl.ds(off, DSA_LATENT)]
        s = _nt_dot(qh, cn) * attn_scale - slope_ref[h] * dist + mask_bias
        m = jnp.max(s, axis=-1, keepdims=True)
        p = jnp.exp(s - m)
        l = jnp.sum(p, axis=-1, keepdims=True)
        o = _dot(p.astype(BF16), cn) / l
        o_ref[0, :, pl.ds(off, DSA_LATENT)] = o.astype(o_ref.dtype)
        return 0

    lax.fori_loop(0, DSA_HEADS, head, 0)


def _dsa_attention(slopes, proj, small, kvw, bsz, seq):
    tq = min(TQ_DSA, seq)
    k_sel = min(TOPK_MAX, seq // 4)
    qw = DSA_HEADS * DSA_LATENT
    qiw = IDX_HEADS * IDX_DIM
    return pl.pallas_call(
        functools.partial(_dsa_kernel, tq=tq, k_sel=k_sel),
        grid=(bsz, seq // tq),
        in_specs=[pl.BlockSpec(memory_space=pltpu.SMEM),
                  pl.BlockSpec((1, tq, qw), lambda b, i: (b, i, 0)),
                  pl.BlockSpec((1, seq, DSA_LATENT), lambda b, i: (b, 0, 0)),
                  pl.BlockSpec((1, DSA_LATENT), lambda b, i: (0, 0)),
                  pl.BlockSpec((1, tq, qiw), lambda b, i: (b, i, qw // qiw)),
                  pl.BlockSpec((1, seq, LANES), lambda b, i: (b, 0, (qw + qiw) // LANES)),
                  pl.BlockSpec((1, tq, LANES), lambda b, i: (b, i, 1))],
        out_specs=pl.BlockSpec((1, tq, qw), lambda b, i: (b, i, 0)),
        out_shape=jax.ShapeDtypeStruct((bsz, seq, qw), BF16),
        compiler_params=_cparams("parallel", "parallel"),
        name="dsa_attn",
    )(slopes, proj, small, kvw, proj, proj, small)


def _mix_tail(mix, h_ref, g_ref, b_ref, rw_ref, rb_ref, hout_ref, logit_ref):
    hn = _layer_norm(DEEPNORM_ALPHA * h_ref[...] + mix, g_ref[...], b_ref[...])
    hout_ref[...] = hn
    logit_ref[...] = _dot(hn.astype(BF16), rw_ref[...]) + rb_ref[...]


def _mix_even_kernel(oa_ref, ob_ref, wa_ref, wb_ref, h_ref, g_ref, b_ref, rw_ref, rb_ref, hout_ref, logit_ref):
    mix = _dot(oa_ref[...], wa_ref[...]) + _dot(ob_ref[...], wb_ref[...])
    _mix_tail(mix, h_ref, g_ref, b_ref, rw_ref, rb_ref, hout_ref, logit_ref)


def _mix_odd_kernel(o_ref, wuv_ref, wo_ref, h_ref, g_ref, b_ref, rw_ref, rb_ref, hout_ref, logit_ref):
    up = _dot(o_ref[...], wuv_ref[...]).astype(BF16)
    _mix_tail(_dot(up, wo_ref[...]), h_ref, g_ref, b_ref, rw_ref, rb_ref, hout_ref, logit_ref)


def _mix_call(kernel, name, acts, weights, h, g, b, rw, rb):
    t, d = h.shape
    tm = min(TM_PROJ, t)
    row = lambda n: pl.BlockSpec((tm, n), lambda i: (i, 0))
    full = lambda a: pl.BlockSpec(a.shape, lambda i: (0, 0))
    return pl.pallas_call(
        kernel,
        grid=(t // tm,),
        in_specs=[row(a.shape[1]) for a in acts] + [full(w) for w in weights]
                 + [row(d), full(g), full(b), full(rw), full(rb)],
        out_specs=[row(d), row(LANES)],
        out_shape=[jax.ShapeDtypeStruct((t, d), F32), jax.ShapeDtypeStruct((t, LANES), F32)],
        compiler_params=_cparams("parallel"),
        name=name,
    )(*acts, *weights, h, g, b, rw, rb)


def _route_kernel(logit_ref, idx_ref, gate_ref, rank_ref, cnt_ref, *, tm):
    @pl.when(pl.program_id(0) == 0)
    def _():
        cnt_ref[...] = jnp.zeros_like(cnt_ref)

    lane = lax.broadcasted_iota(I32, (tm, LANES), 1)
    lane_f = lane.astype(F32)
    x = jnp.where(lane < N_EXPERTS, logit_ref[...], -jnp.inf)
    vals, hots = [], []
    for k in range(TOP_K):
        m = jnp.max(x, axis=-1, keepdims=True)
        first = jnp.min(jnp.where(x == m, lane_f, float(LANES)), axis=-1, keepdims=True)
        hot = lane_f == first
        x = jnp.where(hot, -jnp.inf, x)
        vals.append(m)
        hots.append(hot)
        idx_ref[:, k:k + 1] = first.astype(I32)
    exps = [jnp.exp(v - vals[0]) for v in vals]
    denom = exps[0] + exps[1] + exps[2] + exps[3]
    for k in range(TOP_K):
        gate_ref[:, k:k + 1] = exps[k] / denom

    hot_sum = (hots[0] | hots[1] | hots[2] | hots[3]).astype(F32)
    r = lax.broadcasted_iota(I32, (tm, tm), 0)
    c = lax.broadcasted_iota(I32, (tm, tm), 1)
    earlier = (c < r).astype(BF16)
    before = _dot(earlier, hot_sum.astype(BF16)) + cnt_ref[...]
    for k in range(TOP_K):
        rank_ref[:, k:k + 1] = jnp.sum(jnp.where(hots[k], before, 0.0), axis=-1, keepdims=True).astype(I32)
    cnt_ref[...] += jnp.sum(hot_sum, axis=0, keepdims=True)


def _route(logits):
    t = logits.shape[0]
    tm = min(TM_ROUTE, t)
    narrow = pl.BlockSpec((tm, TOP_K), lambda i: (i, 0))
    return pl.pallas_call(
        functools.partial(_route_kernel, tm=tm),
        grid=(t // tm,),
        in_specs=[pl.BlockSpec((tm, LANES), lambda i: (i, 0))],
        out_specs=[narrow, narrow, narrow, pl.BlockSpec((1, LANES), lambda i: (0, 0))],
        out_shape=[jax.ShapeDtypeStruct((t, TOP_K), I32), jax.ShapeDtypeStruct((t, TOP_K), F32),
                   jax.ShapeDtypeStruct((t, TOP_K), I32), jax.ShapeDtypeStruct((1, LANES), F32)],
        compiler_params=_cparams("arbitrary"),
        name="route",
    )(logits)


def _row_copy(src_hbm, src_row, dst_ref, dst_row, sem):
    return pltpu.make_async_copy(src_hbm.at[pl.ds(src_row, 1), :], dst_ref.at[pl.ds(dst_row, 1), :], sem)


def _dispatch_kernel(pos_ref, x_ref, xs_in, xs_out, sem, *, td):
    del xs_in

    def issue(i, _):
        for k in range(TOP_K):
            _row_copy(x_ref, i, xs_out, pos_ref[i * TOP_K + k], sem).start()
        return 0

    lax.fori_loop(0, td, issue, 0)

    def drain(i, _):
        for k in range(TOP_K):
            _row_copy(x_ref, i, xs_out, pos_ref[i * TOP_K + k], sem).wait()
        return 0

    lax.fori_loop(0, td, drain, 0)


def _dispatch(pos_flat, x, n_rows):
    t, d = x.shape
    td = min(TD_DISPATCH, t)
    zeros = jnp.zeros((n_rows, d), x.dtype)
    return pl.pallas_call(
        functools.partial(_dispatch_kernel, td=td),
        grid=(t // td,),
        in_specs=[pl.BlockSpec((td * TOP_K,), lambda i: (i,), memory_space=pltpu.SMEM),
                  pl.BlockSpec((td, d), lambda i: (i, 0)),
                  pl.BlockSpec(memory_space=pl.ANY)],
        out_specs=pl.BlockSpec(memory_space=pl.ANY),
        out_shape=jax.ShapeDtypeStruct((n_rows, d), x.dtype),
        scratch_shapes=[pltpu.SemaphoreType.DMA],
        input_output_aliases={2: 0},
        compiler_params=_cparams("arbitrary"),
        name="moe_dispatch",
    )(pos_flat, x, zeros)


def _ffn_kernel(tile_expert_ref, n_used_ref, xs_ref, wgu_ref, bgu_ref, wd_ref, bd_ref, y_ref, wgu_bf, wd_bf):
    i = pl.program_id(0)
    d_ff = wd_ref.shape[2]
    new_expert = jnp.logical_or(i == 0, tile_expert_ref[i] != tile_expert_ref[jnp.maximum(i - 1, 0)])

    @pl.when(jnp.logical_and(new_expert, i < n_used_ref[0]))
    def _():
        wgu_bf[...] = wgu_ref[0, 0].astype(BF16)
        wd_bf[...] = wd_ref[0, 0].astype(BF16)

    @pl.when(i < n_used_ref[0])
    def _():
        x = xs_ref[...].astype(BF16)
        hgu = _dot(x, wgu_bf[...]) + bgu_ref[0, 0]
        gate = jnp.minimum(hgu[:, :d_ff], SWIGLU_LIMIT)
        up = jnp.clip(hgu[:, d_ff:], -SWIGLU_LIMIT, SWIGLU_LIMIT)
        act = gate * (1.0 / (1.0 + jnp.exp(-SWIGLU_ALPHA * gate))) * (up + 1.0)
        y_ref[...] = _dot(act.astype(BF16), wd_bf[...]) + bd_ref[0, 0]

    @pl.when(i >= n_used_ref[0])
    def _():
        y_ref[...] = jnp.zeros_like(y_ref)


def _expert_ffn(tile_expert, n_used, xs, w_gu, b_gu, w_down, b_down, layer):
    n_rows, d = xs.shape
    depth, e, _, f2 = w_gu.shape
    d_ff = w_down.shape[2]
    tm = TM_FFN
    grid_spec = pltpu.PrefetchScalarGridSpec(
        num_scalar_prefetch=2,
        grid=(n_rows // tm,),
        in_specs=[pl.BlockSpec((tm, d), lambda i, te, nu: (i, 0)),
                  pl.BlockSpec((1, 1, d, f2), lambda i, te, nu: (layer, te[i], 0, 0)),
                  pl.BlockSpec((1, 1, 1, f2), lambda i, te, nu: (layer, te[i], 0, 0)),
                  pl.BlockSpec((1, 1, d_ff, d), lambda i, te, nu: (layer, te[i], 0, 0)),
                  pl.BlockSpec((1, 1, 1, d), lambda i, te, nu: (layer, te[i], 0, 0))],
        out_specs=pl.BlockSpec((tm, d), lambda i, te, nu: (i, 0)),
        scratch_shapes=[pltpu.VMEM((d, f2), BF16), pltpu.VMEM((d_ff, d), BF16)],
    )
    return pl.pallas_call(
        _ffn_kernel,
        grid_spec=grid_spec,
        out_shape=jax.ShapeDtypeStruct((n_rows, d), F32),
        compiler_params=_cparams("arbitrary"),
        name="expert_ffn",
    )(tile_expert, n_used, xs, w_gu, b_gu.reshape(depth, e, 1, f2), w_down, b_down.reshape(depth, e, 1, d))


def _combine_kernel(pos_ref, gate_ref, h_ref, g_ref, b_ref, y_hbm, out_ref, buf, sem, *, tc):
    def issue(i, _):
        for k in range(TOP_K):
            _row_copy(y_hbm, pos_ref[i * TOP_K + k], buf.at[k], i, sem).start()
        return 0

    lax.fori_loop(0, tc, issue, 0)

    def drain(i, _):
        for k in range(TOP_K):
            _row_copy(y_hbm, pos_ref[i * TOP_K + k], buf.at[k], i, sem).wait()
        return 0

    lax.fori_loop(0, tc, drain, 0)

    gates = gate_ref[...]
    ffn = gates[:, 0:1] * buf[0]
    for k in range(1, TOP_K):
        ffn = ffn + gates[:, k:k + 1] * buf[k]
    out_ref[...] = _layer_norm(DEEPNORM_ALPHA * h_ref[...] + ffn, g_ref[...], b_ref[...])


def _combine(pos_flat, gates, h, g, b, y):
    t, d = h.shape
    tc = min(TC_COMBINE, t)
    return pl.pallas_call(
        functools.partial(_combine_kernel, tc=tc),
        grid=(t // tc,),
        in_specs=[pl.BlockSpec((tc * TOP_K,), lambda i: (i,), memory_space=pltpu.SMEM),
                  pl.BlockSpec((tc, TOP_K), lambda i: (i, 0)),
                  pl.BlockSpec((tc, d), lambda i: (i, 0)),
                  pl.BlockSpec((1, d), lambda i: (0, 0)),
                  pl.BlockSpec((1, d), lambda i: (0, 0)),
                  pl.BlockSpec(memory_space=pl.ANY)],
        out_specs=pl.BlockSpec((tc, d), lambda i: (i, 0)),
        out_shape=jax.ShapeDtypeStruct((t, d), F32),
        scratch_shapes=[pltpu.VMEM((TOP_K, tc, d), F32), pltpu.SemaphoreType.DMA],
        compiler_params=_cparams("arbitrary"),
        name="moe_combine",
    )(pos_flat, gates, h, g, b, y)


def _moe(h, logits, w_gu, b_gu, w_down, b_down, g, b, layer):
    t = h.shape[0]
    idx, gates, rank, counts = _route(logits)
    counts = counts[0, :N_EXPERTS].astype(I32)
    tiles = (counts + TM_FFN - 1) // TM_FFN
    tile_end = jnp.cumsum(tiles)
    start = (tile_end - tiles) * TM_FFN
    n_tiles = (t * TOP_K) // TM_FFN + N_EXPERTS
    tile_ids = jnp.arange(n_tiles, dtype=I32)
    tile_expert = jnp.minimum(jnp.sum((tile_end[None, :] <= tile_ids[:, None]).astype(I32), axis=1), N_EXPERTS - 1)
    pos = (start[idx] + rank).reshape(-1).astype(I32)
    xs = _dispatch(pos, h, n_tiles * TM_FFN)
    y = _expert_ffn(tile_expert, tile_end[-1:].astype(I32), xs, w_gu, b_gu, w_down, b_down, layer)
    return _combine(pos, gates, h, g, b, y)


def _router_params(router_w, router_b):
    rw = jnp.pad(router_w, ((0, 0), (0, LANES - N_EXPERTS))).astype(BF16)
    rb = jnp.pad(router_b, (0, LANES - N_EXPERTS)).reshape(1, LANES)
    return rw, rb


def _alibi_slopes(n):
    return 2.0 ** (-8.0 * jnp.arange(1, n + 1, dtype=F32) / n)


def _even_layer(h, bsz, seq, w_in, w_out, lq1, lk1, lq2, lk2, subw, layer, g, b, rw, rb):
    proj = _project(h, w_in.astype(BF16), BF16).reshape(bsz, seq, -1)
    o_a = _sb_attention(proj, bsz, seq)
    lam_init = 0.8 - 0.6 * math.exp(-0.3 * layer)
    row = lambda v: v.reshape(1, -1)
    o_b = _diff_attention(proj, _alibi_slopes(DIFF_HEADS), row(lq1), row(lk1), row(lq2), row(lk2), row(subw),
                          bsz, seq, lam_init)
    t = bsz * seq
    sbw = SB_HEADS * HEAD_DIM
    wo = w_out.astype(BF16)
    return _mix_call(_mix_even_kernel, "mix_even", [o_a.reshape(t, -1), o_b.reshape(t, -1)],
                     [wo[:sbw], wo[sbw:]], h, row(g), row(b), rw, rb)


def _odd_layer(h, bsz, seq, w_in, kvw, w_uv, w_out, g, b, rw, rb):
    qw = DSA_HEADS * DSA_LATENT
    c0, c1 = qw, qw + DSA_LATENT
    i0, i1 = c1, c1 + IDX_HEADS * IDX_DIM
    k1 = i1 + IDX_DIM
    w_main = jnp.concatenate([w_in[:, :qw], w_in[:, i0:i1], w_in[:, i1:k1], w_in[:, i1:k1]], axis=1).astype(BF16)
    w_small = jnp.pad(jnp.concatenate([w_in[:, c0:c1], w_in[:, k1:]], axis=1),
                      ((0, 0), (0, LANES - IDX_HEADS))).astype(BF16)
    proj = _project(h, w_main, BF16).reshape(bsz, seq, -1)
    small = _project(h, w_small, F32).reshape(bsz, seq, -1)
    o = _dsa_attention(_alibi_slopes(DSA_HEADS), proj, small, kvw.reshape(1, -1), bsz, seq)
    eye = jnp.eye(DSA_HEADS, dtype=w_uv.dtype)
    w_bd = (w_uv[:, :, None, :] * eye[:, None, :, None]).reshape(qw, DSA_HEADS * DSA_VDIM).astype(BF16)
    row = lambda v: v.reshape(1, -1)
    return _mix_call(_mix_odd_kernel, "mix_odd", [o.reshape(bsz * seq, -1)], [w_bd, w_out.astype(BF16)],
                     h, row(g), row(b), rw, rb)


def kernel(x, ev_w_in, ev_w_out, ev_lambda_q1, ev_lambda_k1, ev_lambda_q2, ev_lambda_k2, ev_subln_w, od_w_in, od_kv_norm_w, od_w_uv, od_w_out, ln_mix_g, ln_mix_b, router_w, router_b, exp_w_gu, exp_b_gu, exp_w_down, exp_b_down, ln_ffn_g, ln_ffn_b):
    bsz, seq, d = x.shape
    h = x.reshape(bsz * seq, d)
    for layer in range(ln_mix_g.shape[0]):
        j = layer // 2
        rw, rb = _router_params(router_w[layer], router_b[layer])
        if layer % 2 == 0:
            h, logits = _even_layer(h, bsz, seq, ev_w_in[j], ev_w_out[j], ev_lambda_q1[j], ev_lambda_k1[j],
                                    ev_lambda_q2[j], ev_lambda_k2[j], ev_subln_w[j], layer,
                                    ln_mix_g[layer], ln_mix_b[layer], rw, rb)
        else:
            h, logits = _odd_layer(h, bsz, seq, od_w_in[j], od_kv_norm_w[j], od_w_uv[j], od_w_out[j],
                                   ln_mix_g[layer], ln_mix_b[layer], rw, rb)
        h = _moe(h, logits, exp_w_gu, exp_b_gu, exp_w_down, exp_b_down,
                 ln_ffn_g[layer].reshape(1, -1), ln_ffn_b[layer].reshape(1, -1), layer)
    return h.reshape(bsz, seq, d)
```

```python
import functools
import math

import jax
import jax.numpy as jnp
from jax import lax
from jax.experimental import pallas as pl
from jax.experimental.pallas import tpu as pltpu

F32, BF16, I32 = jnp.float32, jnp.bfloat16, jnp.int32

HEAD_DIM = 64
SB_HEADS = 8
DIFF_HEADS = 4
DIFF_SUB = 64
DIFF_VDIM = 128
DSA_HEADS = 16
DSA_LATENT = 128
DSA_VDIM = 64
IDX_HEADS = 8
IDX_DIM = 64
TOPK_MAX = 256
N_EXPERTS = 32
TOP_K = 4
SWIGLU_LIMIT = 7.0
SWIGLU_ALPHA = 1.702
LN_EPS = 1e-5
RMS_EPS = 1e-5
DEPTH = 2
DEEPNORM_ALPHA = (2 * DEPTH) ** 0.25

LANES = 128
VMEM_LIMIT = 56 * 1024 * 1024
INT_MIN = -(2 ** 31)
MASKED = -1e30
LOG2_E = math.log2(math.e)

TM_PROJ = 512
TQ_ATT = 256
TQ_DSA = 128
TRI = 512
HG_DSA = 4
TM_ROUTE = 512
TM_FFN = 512
TD_DISPATCH = 512
TC_COMBINE = 256


def _cparams(*sem):
    return pltpu.CompilerParams(dimension_semantics=sem, vmem_limit_bytes=VMEM_LIMIT)


def _nt_dot(a, b):
    return lax.dot_general(a, b, (((1,), (1,)), ((), ())), preferred_element_type=F32)


def _dot(a, b):
    return jnp.dot(a, b, preferred_element_type=F32)


def _layer_norm(y, g, b):
    mu = jnp.mean(y, axis=-1, keepdims=True)
    d = y - mu
    var = jnp.mean(d * d, axis=-1, keepdims=True)
    return d * lax.rsqrt(var + LN_EPS) * g + b


def _col_chunk(n):
    for c in (512, 384, 256, 128):
        if n % c == 0:
            return c
    raise ValueError(n)


def _proj_kernel(x_ref, w_ref, o_ref):
    x = x_ref[...].astype(BF16)
    n = o_ref.shape[-1]
    c = _col_chunk(n)
    for j in range(0, n, c):
        o_ref[:, j:j + c] = _dot(x, w_ref[:, j:j + c]).astype(o_ref.dtype)


def _project(x, w, out_dtype):
    t, k = x.shape
    n = w.shape[1]
    tm = min(TM_PROJ, t)
    return pl.pallas_call(
        _proj_kernel,
        grid=(t // tm,),
        in_specs=[pl.BlockSpec((tm, k), lambda i: (i, 0)),
                  pl.BlockSpec((k, n), lambda i: (0, 0))],
        out_specs=pl.BlockSpec((tm, n), lambda i: (i, 0)),
        out_shape=jax.ShapeDtypeStruct((t, n), out_dtype),
        compiler_params=_cparams("parallel"),
        name="in_proj",
    )(x, w)


def _stack_halves(x, lane):
    zero = jnp.zeros_like(x)
    return jnp.concatenate([jnp.where(lane < HEAD_DIM, x, zero), jnp.where(lane >= HEAD_DIM, x, zero)], axis=0)


def _sb_kernel(q_ref, k_ref, v_ref, o_ref, *, tq):
    i = pl.program_id(2)
    scale = HEAD_DIM ** -0.5
    lane = lax.broadcasted_iota(I32, (1, LANES), 1)
    row = lax.broadcasted_iota(I32, (2 * tq, tq), 0)
    row = jnp.where(row >= tq, row - tq, row)
    col = lax.broadcasted_iota(I32, (2 * tq, tq), 1)
    strict = col < row
    r1 = lax.broadcasted_iota(I32, (tq, tq), 0)
    c1 = lax.broadcasted_iota(I32, (tq, tq), 1)
    after = (r1 > c1).astype(BF16)
    q2 = _stack_halves(q_ref[0], lane)

    def block(j, carry, masked):
        run, acc = carry
        start = j * tq
        kb = k_ref[0, pl.ds(start, tq), :]
        vb = v_ref[0, pl.ds(start, tq), :]
        z = _nt_dot(q2, kb) * scale
        log_keep = jnp.minimum(-z, 0.0) - jnp.log(1.0 + jnp.exp(-jnp.abs(z)))
        lk = jnp.where(strict, log_keep, 0.0) if masked else log_keep
        hi = lk.astype(BF16)
        lo = (lk - hi.astype(F32)).astype(BF16)
        log_after = _dot(hi, after) + _dot(lo, after) + run
        w = jnp.exp(z + log_keep + log_after)
        if masked:
            w = jnp.where(strict, w, 0.0)
        wb = w.astype(BF16)
        zero = jnp.zeros_like(vb)
        acc = (acc + _dot(wb[:tq], jnp.where(lane < HEAD_DIM, vb, zero))
               + _dot(wb[tq:], jnp.where(lane >= HEAD_DIM, vb, zero)))
        run = run + jnp.sum(lk, axis=-1, keepdims=True)
        return run, acc

    for n_left in range(k_ref.shape[1] // tq):
        @pl.when(i == n_left)
        def _(n_left=n_left):
            carry = block(n_left, (jnp.zeros((2 * tq, 1), F32), jnp.zeros((tq, LANES), F32)), True)
            for j in range(n_left - 1, -1, -1):
                carry = block(j, carry, False)
            o_ref[0] = carry[1].astype(o_ref.dtype)


def _sb_attention(proj, bsz, seq):
    tq = min(TQ_ATT, seq)
    pairs = SB_HEADS * HEAD_DIM // LANES
    return pl.pallas_call(
        functools.partial(_sb_kernel, tq=tq),
        grid=(bsz, pairs, seq // tq),
        in_specs=[pl.BlockSpec((1, tq, LANES), lambda b, p, i: (b, i, p)),
                  pl.BlockSpec((1, seq, LANES), lambda b, p, i: (b, 0, pairs + p)),
                  pl.BlockSpec((1, seq, LANES), lambda b, p, i: (b, 0, 2 * pairs + p))],
        out_specs=pl.BlockSpec((1, tq, LANES), lambda b, p, i: (b, i, p)),
        out_shape=jax.ShapeDtypeStruct((bsz, seq, SB_HEADS * HEAD_DIM), BF16),
        compiler_params=_cparams("parallel", "parallel", "parallel"),
        name="stickbreak_attn",
    )(proj, proj, proj)


def _diff_kernel(slope_ref, lq1_ref, lk1_ref, lq2_ref, lk2_ref, subw_ref, q_ref, k_ref, v_ref, o_ref,
                 *, tq, lam_init):
    h = pl.program_id(1)
    i = pl.program_id(2)
    scale = DIFF_SUB ** -0.5
    slope = slope_ref[h]
    lam = (jnp.exp(jnp.sum(lq1_ref[...] * lk1_ref[...], keepdims=True))
           - jnp.exp(jnp.sum(lq2_ref[...] * lk2_ref[...], keepdims=True)) + lam_init)
    lane = lax.broadcasted_iota(I32, (1, LANES), 1)
    row = lax.broadcasted_iota(I32, (2 * tq, tq), 0)
    row = jnp.where(row >= tq, row - tq, row)
    col = lax.broadcasted_iota(I32, (2 * tq, tq), 1)
    causal = col <= row
    key_off = lax.broadcasted_iota(I32, (1, tq), 1)
    q2 = _stack_halves(q_ref[0], lane)

    def block(j, carry, masked):
        m, l, acc = carry
        start = j * tq
        kb = k_ref[0, pl.ds(start, tq), :]
        vb = v_ref[0, pl.ds(start, tq), :]
        s = _nt_dot(q2, kb) * scale + slope * (start + key_off).astype(F32)
        if masked:
            s = jnp.where(causal, s, -jnp.inf)
        m_new = jnp.maximum(m, jnp.max(s, axis=-1, keepdims=True))
        a = jnp.exp(m - m_new)
        p = jnp.exp(s - m_new)
        l = a * l + jnp.sum(p, axis=-1, keepdims=True)
        acc = a * acc + _dot(p.astype(BF16), vb)
        return m_new, l, acc

    for n_left in range(k_ref.shape[1] // tq):
        @pl.when(i == n_left)
        def _(n_left=n_left):
            carry = (jnp.full((2 * tq, 1), -jnp.inf, F32), jnp.zeros((2 * tq, 1), F32),
                     jnp.zeros((2 * tq, LANES), F32))
            carry = block(n_left, carry, True)
            for j in range(n_left):
                carry = block(j, carry, False)
            out = carry[2] / carry[1]
            o = out[:tq] - lam * out[tq:]
            o = o * lax.rsqrt(jnp.mean(o * o, axis=-1, keepdims=True) + RMS_EPS) * subw_ref[...]
            o_ref[0] = (o * (1.0 - lam_init)).astype(o_ref.dtype)


def _diff_attention(proj, slopes, lq1, lk1, lq2, lk2, subw, bsz, seq, lam_init):
    tq = min(TQ_ATT, seq)
    q0 = 3 * SB_HEADS * HEAD_DIM // LANES
    k0 = q0 + DIFF_HEADS
    v0 = k0 + DIFF_HEADS
    vec = lambda n: pl.BlockSpec((1, n), lambda b, h, i: (0, 0))
    return pl.pallas_call(
        functools.partial(_diff_kernel, tq=tq, lam_init=lam_init),
        grid=(bsz, DIFF_HEADS, seq // tq),
        in_specs=[pl.BlockSpec(memory_space=pltpu.SMEM),
                  vec(DIFF_SUB), vec(DIFF_SUB), vec(DIFF_SUB), vec(DIFF_SUB), vec(DIFF_VDIM),
                  pl.BlockSpec((1, tq, LANES), lambda b, h, i: (b, i, q0 + h)),
                  pl.BlockSpec((1, seq, LANES), lambda b, h, i: (b, 0, k0 + h)),
                  pl.BlockSpec((1, seq, LANES), lambda b, h, i: (b, 0, v0 + h))],
        out_specs=pl.BlockSpec((1, tq, LANES), lambda b, h, i: (b, i, h)),
        out_shape=jax.ShapeDtypeStruct((bsz, seq, DIFF_HEADS * DIFF_VDIM), BF16),
        compiler_params=_cparams("parallel", "parallel", "parallel"),
        name="diff_attn",
    )(slopes, lq1, lk1, lq2, lk2, subw, proj, proj, proj)


def _tree_sum_rows(x):
    n = x.shape[0] // 8
    x = x.reshape(n, 8, x.shape[1])
    while n > 1:
        n //= 2
        x = x[:n] + x[n:]
    return x[0]


def _tree_max_rows(x):
    n = x.shape[0] // 8
    x = x.reshape(n, 8, x.shape[1])
    while n > 1:
        n //= 2
        x = jnp.maximum(x[:n], x[n:])
    return x[0]


def _dsa_kernel(q_ref, ckv_ref, kvw_ref, qi_ref, ki_ref, wi_ref, qx_ref, kx_ref, o_ref,
                cn_ref, cnt_ref, skey_ref, bias_ref, s_ref, *, tq, tk, k_sel, hg):
    i = pl.program_id(1)
    t0 = i * tq

    @pl.when(i == 0)
    def _():
        c = ckv_ref[0]
        cn = c * lax.rsqrt(jnp.mean(c * c, axis=-1, keepdims=True) + RMS_EPS) * kvw_ref[...]
        cn_ref[:, :DSA_LATENT] = cn.astype(BF16)
        cn_ref[:, DSA_LATENT:] = kx_ref[...]
        cnt_ref[...] = cn.T.astype(BF16)

    for k in range(ckv_ref.shape[1] // tk):
        @pl.when((t0 + tq - 1) // tk == k)
        def _(k=k):
            _dsa_block(q_ref, qi_ref, ki_ref, wi_ref, qx_ref, o_ref, cn_ref, cnt_ref, skey_ref, bias_ref, s_ref,
                       t0=t0, n_tiles=k + 1, tq=tq, tk=tk, k_sel=k_sel, hg=hg)


def _static_loop(n, body, carry):
    for j in range(n):
        carry = body(j, carry)
    return carry


def _dsa_block(q_ref, qi_ref, ki_ref, wi_ref, qx_ref, o_ref, cn_ref, cnt_ref, skey_ref, bias_ref, s_ref,
               *, t0, n_tiles, tq, tk, k_sel, hg):
    def tile_slice(j):
        return pl.ds(j * tk, tk)

    lane = lax.broadcasted_iota(I32, (1, LANES), 1)
    q_pos = t0 + lax.broadcasted_iota(I32, (tk, tq), 1)
    key_off = lax.broadcasted_iota(I32, (tk, tq), 0)

    w_t = (wi_ref[0] * (IDX_HEADS ** -0.5)).T
    idx_scale = IDX_DIM ** -0.5
    stacked = []
    for h in range(IDX_HEADS):
        qp = qi_ref[0, :, (h // 2) * LANES:(h // 2 + 1) * LANES]
        stacked.append(jnp.where((lane // IDX_DIM) == (h % 2), qp, jnp.zeros_like(qp)))
    q8 = jnp.concatenate(stacked, axis=0)

    def score_tile(j, _):
        d = _nt_dot(ki_ref[0, tile_slice(j), :], q8)
        score = jnp.zeros((tk, tq), F32)
        for h in range(IDX_HEADS):
            score = score + jnp.maximum(d[:, h * tq:(h + 1) * tq] * idx_scale, 0.0) * w_t[h:h + 1, :]
        score = jnp.where(score == 0.0, 0.0, score)
        bits = pltpu.bitcast(score, I32)
        skey = jnp.where(bits < 0, bits ^ 0x7FFFFFFF, bits)
        skey_ref[tile_slice(j), :] = jnp.where(j * tk + key_off <= q_pos, skey, INT_MIN)
        return 0

    _static_loop(n_tiles, score_tile, 0)

    def count(pred):
        def tile(j, cnt):
            return cnt + _tree_sum_rows(pred(skey_ref[tile_slice(j), :]).astype(F32))
        cnt = _static_loop(n_tiles, tile, jnp.zeros((8, tq), F32))
        return jnp.sum(cnt, axis=0, keepdims=True)

    kf = float(k_sel)
    thr = jnp.where(count(lambda sk: sk >= 0) >= kf, 0, INT_MIN).astype(I32)

    def bit_step(b, thr):
        cand = thr + jnp.left_shift(jnp.int32(1), 30 - b)
        return jnp.where(count(lambda sk: sk >= cand) >= kf, cand, thr)

    thr = lax.fori_loop(0, 31, bit_step, thr)
    need = kf - count(lambda sk: sk > thr)

    r2 = lax.broadcasted_iota(I32, (tk, tk), 0)
    c2 = lax.broadcasted_iota(I32, (tk, tk), 1)
    upto = (c2 <= r2).astype(BF16)

    def tie_tile(j, seen):
        sk = skey_ref[tile_slice(j), :]
        tied = sk == thr
        prefix = _dot(upto, tied.astype(F32).astype(BF16)) + seen
        selected = ((sk > thr) | (tied & (prefix <= need))) & (j * tk + key_off <= q_pos)
        bias_ref[tile_slice(j), :] = jnp.where(selected, 0.0, MASKED)
        return prefix[tk - 1:tk, :]

    _static_loop(n_tiles, tie_tile, jnp.zeros((1, tq), F32))

    log2_scale = (DSA_LATENT ** -0.5) * LOG2_E

    def head_group(g, _):
        heads = [g * hg + u for u in range(hg)]
        offs = [pl.multiple_of(h * DSA_LATENT, DSA_LATENT) for h in heads]
        qg = jnp.concatenate(
            [jnp.concatenate([q_ref[0, :, pl.ds(off, DSA_LATENT)],
                              jnp.broadcast_to(qx_ref[pl.ds(h, 1), :], (tq, LANES)).astype(BF16)], axis=1)
             for h, off in zip(heads, offs)], axis=0)

        def logits_tile(j, m):
            d = _nt_dot(cn_ref[tile_slice(j), :], qg)
            b = bias_ref[tile_slice(j), :]
            tops = []
            for u in range(hg):
                s = d[:, u * tq:(u + 1) * tq] * log2_scale + b
                s_ref[tile_slice(j), u * tq:(u + 1) * tq] = s
                tops.append(jnp.max(_tree_max_rows(s), axis=0, keepdims=True))
            return jnp.maximum(m, jnp.concatenate(tops, axis=1))

        m = _static_loop(n_tiles, logits_tile, jnp.full((1, hg * tq), MASKED, F32))

        def value_tile(j, carry):
            l, acc = carry
            p = jnp.exp2(s_ref[tile_slice(j), :] - m)
            l = l + jnp.sum(_tree_sum_rows(p), axis=0, keepdims=True)
            acc = acc + _dot(cnt_ref[:, tile_slice(j)], p.astype(BF16))
            return l, acc

        l, acc = _static_loop(n_tiles, value_tile,
                              (jnp.zeros((1, hg * tq), F32), jnp.zeros((DSA_LATENT, hg * tq), F32)))
        out = acc / l
        for u, off in enumerate(offs):
            o_ref[0, :, pl.ds(off, DSA_LATENT)] = out[:, u * tq:(u + 1) * tq].T.astype(o_ref.dtype)
        return 0

    lax.fori_loop(0, DSA_HEADS // hg, head_group, 0)


def _bf16_parts(x, n):
    parts = []
    for _ in range(n):
        p = x.astype(BF16).astype(F32)
        parts.append(p)
        x = x - p
    return parts


def _dsa_attention(proj, small, kvw, bsz, seq):
    tq = min(TQ_DSA, seq)
    tk = min(TRI, seq)
    k_sel = min(TOPK_MAX, seq // 4)
    qw = DSA_HEADS * DSA_LATENT
    qiw = IDX_HEADS * IDX_DIM
    slope = _alibi_slopes(DSA_HEADS) / (DSA_LATENT ** -0.5)
    s_parts = _bf16_parts(slope, 3)
    pos = jnp.arange(seq, dtype=I32)
    p_parts = [(pos // LANES * LANES).astype(F32), (pos % LANES).astype(F32)]
    qx = jnp.stack([sp for _ in p_parts for sp in s_parts], axis=1)
    kx = jnp.stack([pp for pp in p_parts for _ in s_parts], axis=1)
    qx = jnp.pad(qx, ((0, 0), (0, LANES - qx.shape[1])))
    kx = jnp.pad(kx, ((0, 0), (0, LANES - kx.shape[1]))).astype(BF16)
    return pl.pallas_call(
        functools.partial(_dsa_kernel, tq=tq, tk=tk, k_sel=k_sel, hg=HG_DSA),
        grid=(bsz, seq // tq),
        in_specs=[pl.BlockSpec((1, tq, qw), lambda b, i: (b, i, 0)),
                  pl.BlockSpec((1, seq, DSA_LATENT), lambda b, i: (b, 0, 0)),
                  pl.BlockSpec((1, DSA_LATENT), lambda b, i: (0, 0)),
                  pl.BlockSpec((1, tq, qiw), lambda b, i: (b, i, qw // qiw)),
                  pl.BlockSpec((1, seq, LANES), lambda b, i: (b, 0, (qw + qiw) // LANES)),
                  pl.BlockSpec((1, tq, LANES), lambda b, i: (b, i, 1)),
                  pl.BlockSpec((DSA_HEADS, LANES), lambda b, i: (0, 0)),
                  pl.BlockSpec((seq, LANES), lambda b, i: (0, 0))],
        out_specs=pl.BlockSpec((1, tq, qw), lambda b, i: (b, i, 0)),
        out_shape=jax.ShapeDtypeStruct((bsz, seq, qw), BF16),
        scratch_shapes=[pltpu.VMEM((seq, DSA_LATENT + LANES), BF16), pltpu.VMEM((DSA_LATENT, seq), BF16),
                        pltpu.VMEM((seq, tq), I32), pltpu.VMEM((seq, tq), F32),
                        pltpu.VMEM((seq, HG_DSA * tq), F32)],
        compiler_params=_cparams("parallel", "arbitrary"),
        name="dsa_attn",
    )(proj, small, kvw, proj, proj, small, qx, kx)


def _mix_tail(mix, h_ref, g_ref, b_ref, rw_ref, rb_ref, hout_ref, logit_ref):
    hn = _layer_norm(DEEPNORM_ALPHA * h_ref[...] + mix, g_ref[...], b_ref[...])
    hout_ref[...] = hn
    logit_ref[...] = _dot(hn.astype(BF16), rw_ref[...]) + rb_ref[...]


def _mix_even_kernel(oa_ref, ob_ref, wa_ref, wb_ref, h_ref, g_ref, b_ref, rw_ref, rb_ref, hout_ref, logit_ref):
    mix = _dot(oa_ref[...], wa_ref[...]) + _dot(ob_ref[...], wb_ref[...])
    _mix_tail(mix, h_ref, g_ref, b_ref, rw_ref, rb_ref, hout_ref, logit_ref)


def _mix_odd_kernel(o_ref, wuv_ref, wo_ref, h_ref, g_ref, b_ref, rw_ref, rb_ref, hout_ref, logit_ref):
    up = _dot(o_ref[...], wuv_ref[...]).astype(BF16)
    _mix_tail(_dot(up, wo_ref[...]), h_ref, g_ref, b_ref, rw_ref, rb_ref, hout_ref, logit_ref)


def _mix_call(kernel, name, acts, weights, h, g, b, rw, rb):
    t, d = h.shape
    tm = min(TM_PROJ, t)
    row = lambda n: pl.BlockSpec((tm, n), lambda i: (i, 0))
    full = lambda a: pl.BlockSpec(a.shape, lambda i: (0, 0))
    return pl.pallas_call(
        kernel,
        grid=(t // tm,),
        in_specs=[row(a.shape[1]) for a in acts] + [full(w) for w in weights]
                 + [row(d), full(g), full(b), full(rw), full(rb)],
        out_specs=[row(d), row(LANES)],
        out_shape=[jax.ShapeDtypeStruct((t, d), F32), jax.ShapeDtypeStruct((t, LANES), F32)],
        compiler_params=_cparams("parallel"),
        name=name,
    )(*acts, *weights, h, g, b, rw, rb)


def _route_kernel(logit_ref, idx_ref, gate_ref, rank_ref, cnt_ref, *, tm):
    @pl.when(pl.program_id(0) == 0)
    def _():
        cnt_ref[...] = jnp.zeros_like(cnt_ref)

    lane = lax.broadcasted_iota(I32, (tm, LANES), 1)
    lane_f = lane.astype(F32)
    x = jnp.where(lane < N_EXPERTS, logit_ref[...], -jnp.inf)
    vals, hots = [], []
    for k in range(TOP_K):
        m = jnp.max(x, axis=-1, keepdims=True)
        first = jnp.min(jnp.where(x == m, lane_f, float(LANES)), axis=-1, keepdims=True)
        hot = lane_f == first
        x = jnp.where(hot, -jnp.inf, x)
        vals.append(m)
        hots.append(hot)
        idx_ref[:, k:k + 1] = first.astype(I32)
    exps = [jnp.exp(v - vals[0]) for v in vals]
    denom = exps[0] + exps[1] + exps[2] + exps[3]
    for k in range(TOP_K):
        gate_ref[:, k:k + 1] = exps[k] / denom

    hot_sum = (hots[0] | hots[1] | hots[2] | hots[3]).astype(F32)
    r = lax.broadcasted_iota(I32, (tm, tm), 0)
    c = lax.broadcasted_iota(I32, (tm, tm), 1)
    earlier = (c < r).astype(BF16)
    before = _dot(earlier, hot_sum.astype(BF16)) + cnt_ref[...]
    for k in range(TOP_K):
        rank_ref[:, k:k + 1] = jnp.sum(jnp.where(hots[k], before, 0.0), axis=-1, keepdims=True).astype(I32)
    cnt_ref[...] += jnp.sum(hot_sum, axis=0, keepdims=True)


def _route(logits):
    t = logits.shape[0]
    tm = min(TM_ROUTE, t)
    narrow = pl.BlockSpec((tm, TOP_K), lambda i: (i, 0))
    return pl.pallas_call(
        functools.partial(_route_kernel, tm=tm),
        grid=(t // tm,),
        in_specs=[pl.BlockSpec((tm, LANES), lambda i: (i, 0))],
        out_specs=[narrow, narrow, narrow, pl.BlockSpec((1, LANES), lambda i: (0, 0))],
        out_shape=[jax.ShapeDtypeStruct((t, TOP_K), I32), jax.ShapeDtypeStruct((t, TOP_K), F32),
                   jax.ShapeDtypeStruct((t, TOP_K), I32), jax.ShapeDtypeStruct((1, LANES), F32)],
        compiler_params=_cparams("arbitrary"),
        name="route",
    )(logits)


def _row_copy(src_ref, src_row, dst_ref, dst_row, sem):
    return pltpu.make_async_copy(src_ref.at[pl.ds(src_row, 1), :], dst_ref.at[pl.ds(dst_row, 1), :], sem)


def _dispatch_kernel(pos_ref, x_ref, xs_in, xs_out, sem, *, td):
    del xs_in

    def issue(i, _):
        for k in range(TOP_K):
            _row_copy(x_ref, i, xs_out, pos_ref[i * TOP_K + k], sem).start()
        return 0

    lax.fori_loop(0, td, issue, 0)

    def drain(i, _):
        for k in range(TOP_K):
            _row_copy(x_ref, i, xs_out, pos_ref[i * TOP_K + k], sem).wait()
        return 0

    lax.fori_loop(0, td, drain, 0)


def _dispatch(pos_flat, x, n_rows):
    t, d = x.shape
    td = min(TD_DISPATCH, t)
    zeros = jnp.zeros((n_rows, d), x.dtype)
    return pl.pallas_call(
        functools.partial(_dispatch_kernel, td=td),
        grid=(t // td,),
        in_specs=[pl.BlockSpec((td * TOP_K,), lambda i: (i,), memory_space=pltpu.SMEM),
                  pl.BlockSpec((td, d), lambda i: (i, 0)),
                  pl.BlockSpec(memory_space=pl.ANY)],
        out_specs=pl.BlockSpec(memory_space=pl.ANY),
        out_shape=jax.ShapeDtypeStruct((n_rows, d), x.dtype),
        scratch_shapes=[pltpu.SemaphoreType.DMA],
        input_output_aliases={2: 0},
        compiler_params=_cparams("arbitrary"),
        name="moe_dispatch",
    )(pos_flat, x, zeros)


def _ffn_kernel(tile_expert_ref, n_used_ref, xs_ref, wgu_ref, bgu_ref, wd_ref, bd_ref, y_ref, wgu_bf, wd_bf):
    i = pl.program_id(0)
    d_ff = wd_ref.shape[2]
    new_expert = jnp.logical_or(i == 0, tile_expert_ref[i] != tile_expert_ref[jnp.maximum(i - 1, 0)])

    @pl.when(jnp.logical_and(new_expert, i < n_used_ref[0]))
    def _():
        wgu_bf[...] = wgu_ref[0, 0].astype(BF16)
        wd_bf[...] = wd_ref[0, 0].astype(BF16)

    @pl.when(i < n_used_ref[0])
    def _():
        x = xs_ref[...].astype(BF16)
        hgu = _dot(x, wgu_bf[...]) + bgu_ref[0, 0]
        gate = jnp.minimum(hgu[:, :d_ff], SWIGLU_LIMIT)
        up = jnp.clip(hgu[:, d_ff:], -SWIGLU_LIMIT, SWIGLU_LIMIT)
        act = gate * (1.0 / (1.0 + jnp.exp(-SWIGLU_ALPHA * gate))) * (up + 1.0)
        y_ref[...] = _dot(act.astype(BF16), wd_bf[...]) + bd_ref[0, 0]

    @pl.when(i >= n_used_ref[0])
    def _():
        y_ref[...] = jnp.zeros_like(y_ref)


def _expert_ffn(tile_expert, n_used, xs, w_gu, b_gu, w_down, b_down, layer):
    n_rows, d = xs.shape
    depth, e, _, f2 = w_gu.shape
    d_ff = w_down.shape[2]
    tm = TM_FFN
    grid_spec = pltpu.PrefetchScalarGridSpec(
        num_scalar_prefetch=2,
        grid=(n_rows // tm,),
        in_specs=[pl.BlockSpec((tm, d), lambda i, te, nu: (i, 0)),
                  pl.BlockSpec((1, 1, d, f2), lambda i, te, nu: (layer, te[i], 0, 0)),
                  pl.BlockSpec((1, 1, 1, f2), lambda i, te, nu: (layer, te[i], 0, 0)),
                  pl.BlockSpec((1, 1, d_ff, d), lambda i, te, nu: (layer, te[i], 0, 0)),
                  pl.BlockSpec((1, 1, 1, d), lambda i, te, nu: (layer, te[i], 0, 0))],
        out_specs=pl.BlockSpec((tm, d), lambda i, te, nu: (i, 0)),
        scratch_shapes=[pltpu.VMEM((d, f2), BF16), pltpu.VMEM((d_ff, d), BF16)],
    )
    return pl.pallas_call(
        _ffn_kernel,
        grid_spec=grid_spec,
        out_shape=jax.ShapeDtypeStruct((n_rows, d), F32),
        compiler_params=_cparams("arbitrary"),
        name="expert_ffn",
    )(tile_expert, n_used, xs, w_gu, b_gu.reshape(depth, e, 1, f2), w_down, b_down.reshape(depth, e, 1, d))


def _combine_kernel(pos_ref, gate_ref, h_ref, g_ref, b_ref, y_hbm, out_ref, buf, sem, *, tc):
    def issue(i, _):
        for k in range(TOP_K):
            _row_copy(y_hbm, pos_ref[i * TOP_K + k], buf.at[k], i, sem).start()
        return 0

    lax.fori_loop(0, tc, issue, 0)

    def drain(i, _):
        for k in range(TOP_K):
            _row_copy(y_hbm, pos_ref[i * TOP_K + k], buf.at[k], i, sem).wait()
        return 0

    lax.fori_loop(0, tc, drain, 0)

    gates = gate_ref[...]
    ffn = gates[:, 0:1] * buf[0]
    for k in range(1, TOP_K):
        ffn = ffn + gates[:, k:k + 1] * buf[k]
    out_ref[...] = _layer_norm(DEEPNORM_ALPHA * h_ref[...] + ffn, g_ref[...], b_ref[...])


def _combine(pos_flat, gates, h, g, b, y):
    t, d = h.shape
    tc = min(TC_COMBINE, t)
    return pl.pallas_call(
        functools.partial(_combine_kernel, tc=tc),
        grid=(t // tc,),
        in_specs=[pl.BlockSpec((tc * TOP_K,), lambda i: (i,), memory_space=pltpu.SMEM),
                  pl.BlockSpec((tc, TOP_K), lambda i: (i, 0)),
                  pl.BlockSpec((tc, d), lambda i: (i, 0)),
                  pl.BlockSpec((1, d), lambda i: (0, 0)),
                  pl.BlockSpec((1, d), lambda i: (0, 0)),
                  pl.BlockSpec(memory_space=pl.ANY)],
        out_specs=pl.BlockSpec((tc, d), lambda i: (i, 0)),
        out_shape=jax.ShapeDtypeStruct((t, d), F32),
        scratch_shapes=[pltpu.VMEM((TOP_K, tc, d), F32), pltpu.SemaphoreType.DMA],
        compiler_params=_cparams("arbitrary"),
        name="moe_combine",
    )(pos_flat, gates, h, g, b, y)


def _moe(h, logits, w_gu, b_gu, w_down, b_down, g, b, layer):
    t = h.shape[0]
    idx, gates, rank, counts = _route(logits)
    counts = counts[0, :N_EXPERTS].astype(I32)
    tiles = (counts + TM_FFN - 1) // TM_FFN
    tile_end = jnp.cumsum(tiles)
    start = (tile_end - tiles) * TM_FFN
    n_tiles = (t * TOP_K) // TM_FFN + N_EXPERTS
    tile_ids = jnp.arange(n_tiles, dtype=I32)
    tile_expert = jnp.minimum(jnp.sum((tile_end[None, :] <= tile_ids[:, None]).astype(I32), axis=1), N_EXPERTS - 1)
    pos = (start[idx] + rank).reshape(-1).astype(I32)
    xs = _dispatch(pos, h, n_tiles * TM_FFN)
    y = _expert_ffn(tile_expert, tile_end[-1:].astype(I32), xs, w_gu, b_gu, w_down, b_down, layer)
    return _combine(pos, gates, h, g, b, y)


def _router_params(router_w, router_b):
    rw = jnp.pad(router_w, ((0, 0), (0, LANES - N_EXPERTS))).astype(BF16)
    rb = jnp.pad(router_b, (0, LANES - N_EXPERTS)).reshape(1, LANES)
    return rw, rb


def _alibi_slopes(n):
    return 2.0 ** (-8.0 * jnp.arange(1, n + 1, dtype=F32) / n)


def _even_layer(h, bsz, seq, w_in, w_out, lq1, lk1, lq2, lk2, subw, layer, g, b, rw, rb):
    proj = _project(h, w_in.astype(BF16), BF16).reshape(bsz, seq, -1)
    o_a = _sb_attention(proj, bsz, seq)
    lam_init = 0.8 - 0.6 * math.exp(-0.3 * layer)
    row = lambda v: v.reshape(1, -1)
    o_b = _diff_attention(proj, _alibi_slopes(DIFF_HEADS), row(lq1), row(lk1), row(lq2), row(lk2), row(subw),
                          bsz, seq, lam_init)
    t = bsz * seq
    sbw = SB_HEADS * HEAD_DIM
    wo = w_out.astype(BF16)
    return _mix_call(_mix_even_kernel, "mix_even", [o_a.reshape(t, -1), o_b.reshape(t, -1)],
                     [wo[:sbw], wo[sbw:]], h, row(g), row(b), rw, rb)


def _odd_layer(h, bsz, seq, w_in, kvw, w_uv, w_out, g, b, rw, rb):
    qw = DSA_HEADS * DSA_LATENT
    c0, c1 = qw, qw + DSA_LATENT
    i0, i1 = c1, c1 + IDX_HEADS * IDX_DIM
    k1 = i1 + IDX_DIM
    w_main = jnp.concatenate([w_in[:, :qw], w_in[:, i0:i1], w_in[:, i1:k1], w_in[:, i1:k1]], axis=1).astype(BF16)
    w_small = jnp.pad(jnp.concatenate([w_in[:, c0:c1], w_in[:, k1:]], axis=1),
                      ((0, 0), (0, LANES - IDX_HEADS))).astype(BF16)
    proj = _project(h, w_main, BF16).reshape(bsz, seq, -1)
    small = _project(h, w_small, F32).reshape(bsz, seq, -1)
    o = _dsa_attention(proj, small, kvw.reshape(1, -1), bsz, seq)
    eye = jnp.eye(DSA_HEADS, dtype=w_uv.dtype)
    w_bd = (w_uv[:, :, None, :] * eye[:, None, :, None]).reshape(qw, DSA_HEADS * DSA_VDIM).astype(BF16)
    row = lambda v: v.reshape(1, -1)
    return _mix_call(_mix_odd_kernel, "mix_odd", [o.reshape(bsz * seq, -1)], [w_bd, w_out.astype(BF16)],
                     h, row(g), row(b), rw, rb)


def kernel(x, ev_w_in, ev_w_out, ev_lambda_q1, ev_lambda_k1, ev_lambda_q2, ev_lambda_k2, ev_subln_w, od_w_in, od_kv_norm_w, od_w_uv, od_w_out, ln_mix_g, ln_mix_b, router_w, router_b, exp_w_gu, exp_b_gu, exp_w_down, exp_b_down, ln_ffn_g, ln_ffn_b):
    bsz, seq, d = x.shape
    h = x.reshape(bsz * seq, d)
    for layer in range(ln_mix_g.shape[0]):
        j = layer // 2
        rw, rb = _router_params(router_w[layer], router_b[layer])
        if layer % 2 == 0:
            h, logits = _even_layer(h, bsz, seq, ev_w_in[j], ev_w_out[j], ev_lambda_q1[j], ev_lambda_k1[j],
                                    ev_lambda_q2[j], ev_lambda_k2[j], ev_subln_w[j], layer,
                                    ln_mix_g[layer], ln_mix_b[layer], rw, rb)
        else:
            h, logits = _odd_layer(h, bsz, seq, od_w_in[j], od_kv_norm_w[j], od_w_uv[j], od_w_out[j],
                                   ln_mix_g[layer], ln_mix_b[layer], rw, rb)
        h = _moe(h, logits, exp_w_gu, exp_b_gu, exp_w_down, exp_b_down,
                 ln_ffn_g[layer].reshape(1, -1), ln_ffn_b[layer].reshape(1, -1), layer)
    return h.reshape(bsz, seq, d)
```

```python
import functools
import math

import jax
import jax.numpy as jnp
from jax import lax
from jax.experimental import pallas as pl
from jax.experimental.pallas import tpu as pltpu

F32, BF16, I32 = jnp.float32, jnp.bfloat16, jnp.int32

HEAD_DIM = 64
SB_HEADS = 8
DIFF_HEADS = 4
DIFF_SUB = 64
DIFF_VDIM = 128
DSA_HEADS = 16
DSA_LATENT = 128
DSA_VDIM = 64
IDX_HEADS = 8
IDX_DIM = 64
TOPK_MAX = 256
N_EXPERTS = 32
TOP_K = 4
SWIGLU_LIMIT = 7.0
SWIGLU_ALPHA = 1.702
LN_EPS = 1e-5
RMS_EPS = 1e-5
DEPTH = 2
DEEPNORM_ALPHA = (2 * DEPTH) ** 0.25

LANES = 128
SUBLANES = 8
VMEM_LIMIT = 56 * 1024 * 1024
INT_MIN = -(2 ** 31)
MASKED = -1e30
LOG2_E = math.log2(math.e)

TM_PROJ = 512
TQ_ATT = 256
TQ_DSA = 128
TRI = 512
HG_DSA = 16
TM_ROUTE = 512
TM_FFN = 512
TD_DISPATCH = 512
TC_COMBINE = 256


def _cparams(*sem):
    return pltpu.CompilerParams(dimension_semantics=sem, vmem_limit_bytes=VMEM_LIMIT)


def _nt_dot(a, b):
    return lax.dot_general(a, b, (((1,), (1,)), ((), ())), preferred_element_type=F32)


def _dot(a, b):
    return jnp.dot(a, b, preferred_element_type=F32)


def _layer_norm(y, g, b):
    mu = jnp.mean(y, axis=-1, keepdims=True)
    d = y - mu
    var = jnp.mean(d * d, axis=-1, keepdims=True)
    return d * lax.rsqrt(var + LN_EPS) * g + b


def _col_chunk(n):
    for c in (512, 384, 256, 128):
        if n % c == 0:
            return c
    raise ValueError(n)


def _proj_kernel(x_ref, w_ref, o_ref):
    x = x_ref[...].astype(BF16)
    n = o_ref.shape[-1]
    c = _col_chunk(n)
    for j in range(0, n, c):
        o_ref[:, j:j + c] = _dot(x, w_ref[:, j:j + c]).astype(o_ref.dtype)


def _project(x, w, out_dtype):
    t, k = x.shape
    n = w.shape[1]
    tm = min(TM_PROJ, t)
    return pl.pallas_call(
        _proj_kernel,
        grid=(t // tm,),
        in_specs=[pl.BlockSpec((tm, k), lambda i: (i, 0)),
                  pl.BlockSpec((k, n), lambda i: (0, 0))],
        out_specs=pl.BlockSpec((tm, n), lambda i: (i, 0)),
        out_shape=jax.ShapeDtypeStruct((t, n), out_dtype),
        compiler_params=_cparams("parallel"),
        name="in_proj",
    )(x, w)


def _stack_halves(x, lane):
    zero = jnp.zeros_like(x)
    return jnp.concatenate([jnp.where(lane < HEAD_DIM, x, zero), jnp.where(lane >= HEAD_DIM, x, zero)], axis=0)


def _sb_kernel(q_ref, k_ref, v_ref, o_ref, *, tq):
    i = pl.program_id(2)
    scale = HEAD_DIM ** -0.5
    lane = lax.broadcasted_iota(I32, (1, LANES), 1)
    row = lax.broadcasted_iota(I32, (2 * tq, tq), 0)
    row = jnp.where(row >= tq, row - tq, row)
    col = lax.broadcasted_iota(I32, (2 * tq, tq), 1)
    strict = col < row
    r1 = lax.broadcasted_iota(I32, (tq, tq), 0)
    c1 = lax.broadcasted_iota(I32, (tq, tq), 1)
    after = (r1 > c1).astype(BF16)
    q2 = _stack_halves(q_ref[0], lane)

    def block(j, carry, masked):
        run, acc = carry
        start = j * tq
        kb = k_ref[0, pl.ds(start, tq), :]
        vb = v_ref[0, pl.ds(start, tq), :]
        z2 = _nt_dot(q2, kb) * (scale * LOG2_E)
        neg = -z2
        keep2 = jnp.minimum(neg, 0.0) - jnp.log2(1.0 + jnp.exp2(jnp.minimum(z2, neg)))
        lk = jnp.where(strict, keep2, 0.0) if masked else keep2
        hi = lk.astype(BF16)
        lo = (lk - hi.astype(F32)).astype(BF16)
        suffix = _dot(hi, after) + _dot(lo, after)
        w = jnp.exp2(z2 + keep2 + suffix + run)
        if masked:
            w = jnp.where(strict, w, 0.0)
        wb = w.astype(BF16)
        zero = jnp.zeros_like(vb)
        acc = (acc + _dot(wb[:tq], jnp.where(lane < HEAD_DIM, vb, zero))
               + _dot(wb[tq:], jnp.where(lane >= HEAD_DIM, vb, zero)))
        run = run + suffix[:, 0:1] + lk[:, 0:1]
        return run, acc

    for n_left in range(k_ref.shape[1] // tq):
        @pl.when(i == n_left)
        def _(n_left=n_left):
            carry = block(n_left, (jnp.zeros((2 * tq, 1), F32), jnp.zeros((tq, LANES), F32)), True)
            for j in range(n_left - 1, -1, -1):
                carry = block(j, carry, False)
            o_ref[0] = carry[1].astype(o_ref.dtype)


def _sb_attention(proj, bsz, seq):
    tq = min(TQ_ATT, seq)
    pairs = SB_HEADS * HEAD_DIM // LANES
    return pl.pallas_call(
        functools.partial(_sb_kernel, tq=tq),
        grid=(bsz, pairs, seq // tq),
        in_specs=[pl.BlockSpec((1, tq, LANES), lambda b, p, i: (b, i, p)),
                  pl.BlockSpec((1, seq, LANES), lambda b, p, i: (b, 0, pairs + p)),
                  pl.BlockSpec((1, seq, LANES), lambda b, p, i: (b, 0, 2 * pairs + p))],
        out_specs=pl.BlockSpec((1, tq, LANES), lambda b, p, i: (b, i, p)),
        out_shape=jax.ShapeDtypeStruct((bsz, seq, SB_HEADS * HEAD_DIM), BF16),
        compiler_params=_cparams("parallel", "parallel", "parallel"),
        name="stickbreak_attn",
    )(proj, proj, proj)


def _diff_kernel(slope_ref, lq1_ref, lk1_ref, lq2_ref, lk2_ref, subw_ref, q_ref, k_ref, v_ref, o_ref,
                 *, tq, lam_init):
    h = pl.program_id(1)
    i = pl.program_id(2)
    scale = DIFF_SUB ** -0.5
    slope = slope_ref[h]
    lam = (jnp.exp(jnp.sum(lq1_ref[...] * lk1_ref[...], keepdims=True))
           - jnp.exp(jnp.sum(lq2_ref[...] * lk2_ref[...], keepdims=True)) + lam_init)
    lane = lax.broadcasted_iota(I32, (1, LANES), 1)
    row = lax.broadcasted_iota(I32, (2 * tq, tq), 0)
    row = jnp.where(row >= tq, row - tq, row)
    col = lax.broadcasted_iota(I32, (2 * tq, tq), 1)
    causal = col <= row
    key_off = lax.broadcasted_iota(I32, (1, tq), 1)
    q2 = _stack_halves(q_ref[0], lane)

    def block(j, carry, masked):
        m, l, acc = carry
        start = j * tq
        kb = k_ref[0, pl.ds(start, tq), :]
        vb = v_ref[0, pl.ds(start, tq), :]
        s = _nt_dot(q2, kb) * (scale * LOG2_E) + (slope * LOG2_E) * (start + key_off).astype(F32)
        if masked:
            s = jnp.where(causal, s, -jnp.inf)
        m_new = jnp.maximum(m, jnp.max(s, axis=-1, keepdims=True))
        a = jnp.exp2(m - m_new)
        p = jnp.exp2(s - m_new)
        l = a * l + jnp.sum(p, axis=-1, keepdims=True)
        acc = a * acc + _dot(p.astype(BF16), vb)
        return m_new, l, acc

    for n_left in range(k_ref.shape[1] // tq):
        @pl.when(i == n_left)
        def _(n_left=n_left):
            carry = (jnp.full((2 * tq, 1), -jnp.inf, F32), jnp.zeros((2 * tq, 1), F32),
                     jnp.zeros((2 * tq, LANES), F32))
            carry = block(n_left, carry, True)
            for j in range(n_left):
                carry = block(j, carry, False)
            out = carry[2] / carry[1]
            o = out[:tq] - lam * out[tq:]
            o = o * lax.rsqrt(jnp.mean(o * o, axis=-1, keepdims=True) + RMS_EPS) * subw_ref[...]
            o_ref[0] = (o * (1.0 - lam_init)).astype(o_ref.dtype)


def _diff_attention(proj, slopes, lq1, lk1, lq2, lk2, subw, bsz, seq, lam_init):
    tq = min(TQ_ATT, seq)
    q0 = 3 * SB_HEADS * HEAD_DIM // LANES
    k0 = q0 + DIFF_HEADS
    v0 = k0 + DIFF_HEADS
    vec = lambda n: pl.BlockSpec((1, n), lambda b, h, i: (0, 0))
    return pl.pallas_call(
        functools.partial(_diff_kernel, tq=tq, lam_init=lam_init),
        grid=(bsz, DIFF_HEADS, seq // tq),
        in_specs=[pl.BlockSpec(memory_space=pltpu.SMEM),
                  vec(DIFF_SUB), vec(DIFF_SUB), vec(DIFF_SUB), vec(DIFF_SUB), vec(DIFF_VDIM),
                  pl.BlockSpec((1, tq, LANES), lambda b, h, i: (b, i, q0 + h)),
                  pl.BlockSpec((1, seq, LANES), lambda b, h, i: (b, 0, k0 + h)),
                  pl.BlockSpec((1, seq, LANES), lambda b, h, i: (b, 0, v0 + h))],
        out_specs=pl.BlockSpec((1, tq, LANES), lambda b, h, i: (b, i, h)),
        out_shape=jax.ShapeDtypeStruct((bsz, seq, DIFF_HEADS * DIFF_VDIM), BF16),
        compiler_params=_cparams("parallel", "parallel", "parallel"),
        name="diff_attn",
    )(slopes, lq1, lk1, lq2, lk2, subw, proj, proj, proj)


def _tree_sum_rows(x):
    n = x.shape[0] // 8
    x = x.reshape(n, 8, x.shape[1])
    while n > 1:
        n //= 2
        x = x[:n] + x[n:]
    return x[0]


def _tree_max_rows(x):
    n = x.shape[0] // 8
    x = x.reshape(n, 8, x.shape[1])
    while n > 1:
        n //= 2
        x = jnp.maximum(x[:n], x[n:])
    return x[0]


def _dsa_kernel(q_ref, ckv_ref, kvw_ref, qi_ref, ki_ref, wi_ref, qx_ref, kx_ref, o_ref,
                cn_ref, cnt_ref, skey_ref, bias_ref, s_ref, *, tq, tk, k_sel, hg):
    i = pl.program_id(1)
    t0 = i * tq

    @pl.when(i == 0)
    def _():
        c = ckv_ref[0]
        cn = c * lax.rsqrt(jnp.mean(c * c, axis=-1, keepdims=True) + RMS_EPS) * kvw_ref[...]
        cn_ref[:, :DSA_LATENT] = cn.astype(BF16)
        cn_ref[:, DSA_LATENT:] = kx_ref[...]
        cnt_ref[...] = cn.T.astype(BF16)

    for k in range(ckv_ref.shape[1] // tk):
        @pl.when((t0 + tq - 1) // tk == k)
        def _(k=k):
            _dsa_block(q_ref, qi_ref, ki_ref, wi_ref, qx_ref, o_ref, cn_ref, cnt_ref, skey_ref, bias_ref, s_ref,
                       t0=t0, n_tiles=k + 1, tq=tq, tk=tk, k_sel=k_sel, hg=hg)


def _static_loop(n, body, carry):
    for j in range(n):
        carry = body(j, carry)
    return carry


def _dsa_block(q_ref, qi_ref, ki_ref, wi_ref, qx_ref, o_ref, cn_ref, cnt_ref, skey_ref, bias_ref, s_ref,
               *, t0, n_tiles, tq, tk, k_sel, hg):
    def tile_slice(j):
        return pl.ds(j * tk, tk)

    lane = lax.broadcasted_iota(I32, (1, LANES), 1)
    q_pos = t0 + lax.broadcasted_iota(I32, (tk, tq), 1)
    key_off = lax.broadcasted_iota(I32, (tk, tq), 0)

    w_t = (wi_ref[0] * (IDX_HEADS ** -0.5)).T
    idx_scale = IDX_DIM ** -0.5
    stacked = []
    for h in range(IDX_HEADS):
        qp = qi_ref[0, :, (h // 2) * LANES:(h // 2 + 1) * LANES]
        stacked.append(jnp.where((lane // IDX_DIM) == (h % 2), qp, jnp.zeros_like(qp)))
    q8 = jnp.concatenate(stacked, axis=0)

    def score_tile(j, _):
        d = _nt_dot(ki_ref[0, tile_slice(j), :], q8)
        score = jnp.zeros((tk, tq), F32)
        for h in range(IDX_HEADS):
            score = score + jnp.maximum(d[:, h * tq:(h + 1) * tq] * idx_scale, 0.0) * w_t[h:h + 1, :]
        score = jnp.where(score == 0.0, 0.0, score)
        bits = pltpu.bitcast(score, I32)
        skey = jnp.where(bits < 0, bits ^ 0x7FFFFFFF, bits)
        skey_ref[tile_slice(j), :] = jnp.where(j * tk + key_off <= q_pos, skey, INT_MIN)
        return 0

    _static_loop(n_tiles, score_tile, 0)

    def count(pred):
        def tile(j, cnt):
            return cnt + _tree_sum_rows(pred(skey_ref[tile_slice(j), :]).astype(F32))
        cnt = _static_loop(n_tiles, tile, jnp.zeros((8, tq), F32))
        return jnp.sum(cnt, axis=0, keepdims=True)

    kf = float(k_sel)
    thr = jnp.where(count(lambda sk: sk >= 0) >= kf, 0, INT_MIN).astype(I32)

    def bit_step(b, thr):
        cand = thr + jnp.left_shift(jnp.int32(1), 30 - b)
        return jnp.where(count(lambda sk: sk >= cand) >= kf, cand, thr)

    thr = lax.fori_loop(0, 31, bit_step, thr)
    need = kf - count(lambda sk: sk > thr)

    r2 = lax.broadcasted_iota(I32, (tk, tk), 0)
    c2 = lax.broadcasted_iota(I32, (tk, tk), 1)
    upto = (c2 <= r2).astype(BF16)

    def tie_tile(j, seen):
        sk = skey_ref[tile_slice(j), :]
        tied = sk == thr
        prefix = _dot(upto, tied.astype(F32).astype(BF16)) + seen
        selected = ((sk > thr) | (tied & (prefix <= need))) & (j * tk + key_off <= q_pos)
        bias_ref[tile_slice(j), :] = jnp.where(selected, 0.0, MASKED)
        return prefix[tk - 1:tk, :]

    _static_loop(n_tiles, tie_tile, jnp.zeros((1, tq), F32))

    log2_scale = (DSA_LATENT ** -0.5) * LOG2_E

    def head_group(g, _):
        heads = [g * hg + u for u in range(hg)]
        offs = [pl.multiple_of(h * DSA_LATENT, DSA_LATENT) for h in heads]
        qg = jnp.concatenate(
            [jnp.concatenate([q_ref[0, :, pl.ds(off, DSA_LATENT)],
                              jnp.broadcast_to(qx_ref[pl.ds(h, 1), :], (tq, LANES)).astype(BF16)], axis=1)
             for h, off in zip(heads, offs)], axis=0)

        def logits_tile(j, m):
            d = _nt_dot(cn_ref[tile_slice(j), :], qg)
            b = bias_ref[tile_slice(j), :]
            tops = []
            for u in range(hg):
                s = d[:, u * tq:(u + 1) * tq] * log2_scale + b
                s_ref[tile_slice(j), u * tq:(u + 1) * tq] = s
                tops.append(jnp.max(_tree_max_rows(s), axis=0, keepdims=True))
            return jnp.maximum(m, jnp.concatenate(tops, axis=1))

        m = _static_loop(n_tiles, logits_tile, jnp.full((1, hg * tq), MASKED, F32))

        def value_tile(j, carry):
            l, acc = carry
            p = jnp.exp2(s_ref[tile_slice(j), :] - m)
            l = l + jnp.sum(_tree_sum_rows(p), axis=0, keepdims=True)
            acc = acc + _dot(cnt_ref[:, tile_slice(j)], p.astype(BF16))
            return l, acc

        l, acc = _static_loop(n_tiles, value_tile,
                              (jnp.zeros((1, hg * tq), F32), jnp.zeros((DSA_LATENT, hg * tq), F32)))
        out = acc / l
        for u, off in enumerate(offs):
            o_ref[0, :, pl.ds(off, DSA_LATENT)] = out[:, u * tq:(u + 1) * tq].T.astype(o_ref.dtype)
        return 0

    lax.fori_loop(0, DSA_HEADS // hg, head_group, 0)


def _bf16_parts(x, n):
    parts = []
    for _ in range(n):
        p = x.astype(BF16).astype(F32)
        parts.append(p)
        x = x - p
    return parts


def _dsa_attention(proj, small, kvw, bsz, seq):
    tq = min(TQ_DSA, seq)
    tk = min(TRI, seq)
    k_sel = min(TOPK_MAX, seq // 4)
    qw = DSA_HEADS * DSA_LATENT
    qiw = IDX_HEADS * IDX_DIM
    slope = _alibi_slopes(DSA_HEADS) / (DSA_LATENT ** -0.5)
    s_parts = _bf16_parts(slope, 3)
    pos = jnp.arange(seq, dtype=I32)
    p_parts = [(pos // LANES * LANES).astype(F32), (pos % LANES).astype(F32)]
    qx = jnp.stack([sp for _ in p_parts for sp in s_parts], axis=1)
    kx = jnp.stack([pp for pp in p_parts for _ in s_parts], axis=1)
    qx = jnp.pad(qx, ((0, 0), (0, LANES - qx.shape[1])))
    kx = jnp.pad(kx, ((0, 0), (0, LANES - kx.shape[1]))).astype(BF16)
    return pl.pallas_call(
        functools.partial(_dsa_kernel, tq=tq, tk=tk, k_sel=k_sel, hg=HG_DSA),
        grid=(bsz, seq // tq),
        in_specs=[pl.BlockSpec((1, tq, qw), lambda b, i: (b, i, 0)),
                  pl.BlockSpec((1, seq, DSA_LATENT), lambda b, i: (b, 0, 0)),
                  pl.BlockSpec((1, DSA_LATENT), lambda b, i: (0, 0)),
                  pl.BlockSpec((1, tq, qiw), lambda b, i: (b, i, qw // qiw)),
                  pl.BlockSpec((1, seq, LANES), lambda b, i: (b, 0, (qw + qiw) // LANES)),
                  pl.BlockSpec((1, tq, LANES), lambda b, i: (b, i, 1)),
                  pl.BlockSpec((DSA_HEADS, LANES), lambda b, i: (0, 0)),
                  pl.BlockSpec((seq, LANES), lambda b, i: (0, 0))],
        out_specs=pl.BlockSpec((1, tq, qw), lambda b, i: (b, i, 0)),
        out_shape=jax.ShapeDtypeStruct((bsz, seq, qw), BF16),
        scratch_shapes=[pltpu.VMEM((seq, DSA_LATENT + LANES), BF16), pltpu.VMEM((DSA_LATENT, seq), BF16),
                        pltpu.VMEM((seq, tq), I32), pltpu.VMEM((seq, tq), F32),
                        pltpu.VMEM((seq, HG_DSA * tq), F32)],
        compiler_params=_cparams("parallel", "arbitrary"),
        name="dsa_attn",
    )(proj, small, kvw, proj, proj, small, qx, kx)


def _mix_tail(mix, h_ref, g_ref, b_ref, rw_ref, rb_ref, hout_ref, logit_ref):
    hn = _layer_norm(DEEPNORM_ALPHA * h_ref[...] + mix, g_ref[...], b_ref[...])
    hout_ref[...] = hn
    logit_ref[...] = _dot(hn.astype(BF16), rw_ref[...]) + rb_ref[...]


def _mix_even_kernel(oa_ref, ob_ref, wa_ref, wb_ref, h_ref, g_ref, b_ref, rw_ref, rb_ref, hout_ref, logit_ref):
    mix = _dot(oa_ref[...], wa_ref[...]) + _dot(ob_ref[...], wb_ref[...])
    _mix_tail(mix, h_ref, g_ref, b_ref, rw_ref, rb_ref, hout_ref, logit_ref)


def _mix_odd_kernel(o_ref, wuv_ref, wo_ref, h_ref, g_ref, b_ref, rw_ref, rb_ref, hout_ref, logit_ref):
    up = _dot(o_ref[...], wuv_ref[...]).astype(BF16)
    _mix_tail(_dot(up, wo_ref[...]), h_ref, g_ref, b_ref, rw_ref, rb_ref, hout_ref, logit_ref)


def _mix_call(kernel, name, acts, weights, h, g, b, rw, rb):
    t, d = h.shape
    tm = min(TM_PROJ, t)
    row = lambda n: pl.BlockSpec((tm, n), lambda i: (i, 0))
    full = lambda a: pl.BlockSpec(a.shape, lambda i: (0, 0))
    return pl.pallas_call(
        kernel,
        grid=(t // tm,),
        in_specs=[row(a.shape[1]) for a in acts] + [full(w) for w in weights]
                 + [row(d), full(g), full(b), full(rw), full(rb)],
        out_specs=[row(d), row(LANES)],
        out_shape=[jax.ShapeDtypeStruct((t, d), F32), jax.ShapeDtypeStruct((t, LANES), F32)],
        compiler_params=_cparams("parallel"),
        name=name,
    )(*acts, *weights, h, g, b, rw, rb)


def _route_kernel(logit_ref, idx_ref, gate_ref, rank_ref, cnt_ref, *, tm):
    @pl.when(pl.program_id(0) == 0)
    def _():
        cnt_ref[...] = jnp.zeros_like(cnt_ref)

    lane = lax.broadcasted_iota(I32, (tm, LANES), 1)
    lane_f = lane.astype(F32)
    x = jnp.where(lane < N_EXPERTS, logit_ref[...], -jnp.inf)
    vals, hots = [], []
    for k in range(TOP_K):
        m = jnp.max(x, axis=-1, keepdims=True)
        first = jnp.min(jnp.where(x == m, lane_f, float(LANES)), axis=-1, keepdims=True)
        hot = lane_f == first
        x = jnp.where(hot, -jnp.inf, x)
        vals.append(m)
        hots.append(hot)
        idx_ref[:, k:k + 1] = first.astype(I32)
    exps = [jnp.exp(v - vals[0]) for v in vals]
    denom = exps[0] + exps[1] + exps[2] + exps[3]
    for k in range(TOP_K):
        gate_ref[:, k:k + 1] = exps[k] / denom

    hot_sum = (hots[0] | hots[1] | hots[2] | hots[3]).astype(F32)
    r = lax.broadcasted_iota(I32, (tm, tm), 0)
    c = lax.broadcasted_iota(I32, (tm, tm), 1)
    earlier = (c < r).astype(BF16)
    before = _dot(earlier, hot_sum.astype(BF16)) + cnt_ref[...]
    for k in range(TOP_K):
        rank_ref[:, k:k + 1] = jnp.sum(jnp.where(hots[k], before, 0.0), axis=-1, keepdims=True).astype(I32)
    cnt_ref[...] += jnp.sum(hot_sum, axis=0, keepdims=True)


def _route(logits):
    t = logits.shape[0]
    tm = min(TM_ROUTE, t)
    narrow = pl.BlockSpec((tm, TOP_K), lambda i: (i, 0))
    return pl.pallas_call(
        functools.partial(_route_kernel, tm=tm),
        grid=(t // tm,),
        in_specs=[pl.BlockSpec((tm, LANES), lambda i: (i, 0))],
        out_specs=[narrow, narrow, narrow, pl.BlockSpec((1, LANES), lambda i: (0, 0))],
        out_shape=[jax.ShapeDtypeStruct((t, TOP_K), I32), jax.ShapeDtypeStruct((t, TOP_K), F32),
                   jax.ShapeDtypeStruct((t, TOP_K), I32), jax.ShapeDtypeStruct((1, LANES), F32)],
        compiler_params=_cparams("arbitrary"),
        name="route",
    )(logits)


def _row_copy(src_ref, src_row, dst_ref, dst_row, sem):
    return pltpu.make_async_copy(src_ref.at[pl.ds(src_row, 1), :], dst_ref.at[pl.ds(dst_row, 1), :], sem)


def _dispatch_kernel(pad_start_ref, pad_count_ref, pos_ref, x_ref, xs_out, zero_ref, sem, pad_sem, *, td):
    @pl.when(pl.program_id(0) == 0)
    def _():
        zero_ref[...] = jnp.zeros_like(zero_ref)

        def pads(e, start_not_wait):
            def fill(present, n_rows, off):
                copy = pltpu.make_async_copy(zero_ref.at[pl.ds(0, n_rows), :], xs_out.at[pl.ds(off, n_rows), :],
                                             pad_sem)

                @pl.when(jnp.logical_and(present, start_not_wait))
                def _():
                    copy.start()

                @pl.when(jnp.logical_and(present, jnp.logical_not(start_not_wait)))
                def _():
                    copy.wait()

            off = pad_start_ref[e]
            count = pad_count_ref[e]
            head = jnp.minimum(count, (-off) & (SUBLANES - 1))
            for r in range(SUBLANES - 1):
                fill(r < head, 1, off + r)
            off = off + head
            count = count - head
            chunk = zero_ref.shape[0] // 2
            while chunk >= SUBLANES:
                present = (count & chunk) != 0
                fill(present, chunk, pl.multiple_of(off, SUBLANES))
                off = off + jnp.where(present, chunk, 0)
                chunk //= 2
            for r in range(SUBLANES - 1):
                fill(r < (count & (SUBLANES - 1)), 1, off + r)
            return start_not_wait

        lax.fori_loop(0, N_EXPERTS, pads, True)
        lax.fori_loop(0, N_EXPERTS, pads, False)

    def issue(i, _):
        for k in range(TOP_K):
            _row_copy(x_ref, i, xs_out, pos_ref[i * TOP_K + k], sem).start()
        return 0

    lax.fori_loop(0, td, issue, 0)

    def drain(i, _):
        for k in range(TOP_K):
            _row_copy(x_ref, i, xs_out, pos_ref[i * TOP_K + k], sem).wait()
        return 0

    lax.fori_loop(0, td, drain, 0)


def _dispatch(pad_start, pad_count, pos_flat, x, n_rows):
    t, d = x.shape
    td = min(TD_DISPATCH, t)
    grid_spec = pltpu.PrefetchScalarGridSpec(
        num_scalar_prefetch=2,
        grid=(t // td,),
        in_specs=[pl.BlockSpec((td * TOP_K,), lambda i, ps, pc: (i,), memory_space=pltpu.SMEM),
                  pl.BlockSpec((td, d), lambda i, ps, pc: (i, 0))],
        out_specs=pl.BlockSpec(memory_space=pl.ANY),
        scratch_shapes=[pltpu.VMEM((TM_FFN, d), x.dtype), pltpu.SemaphoreType.DMA, pltpu.SemaphoreType.DMA],
    )
    return pl.pallas_call(
        functools.partial(_dispatch_kernel, td=td),
        grid_spec=grid_spec,
        out_shape=jax.ShapeDtypeStruct((n_rows, d), x.dtype),
        compiler_params=_cparams("arbitrary"),
        name="moe_dispatch",
    )(pad_start, pad_count, pos_flat, x)


def _ffn_kernel(tile_expert_ref, n_used_ref, xs_ref, wgu_ref, bgu_ref, wd_ref, bd_ref, y_ref, wgu_bf, wd_bf):
    i = pl.program_id(0)
    d_ff = wd_ref.shape[2]
    new_expert = jnp.logical_or(i == 0, tile_expert_ref[i] != tile_expert_ref[jnp.maximum(i - 1, 0)])

    @pl.when(jnp.logical_and(new_expert, i < n_used_ref[0]))
    def _():
        wgu_bf[...] = wgu_ref[0, 0].astype(BF16)
        wd_bf[...] = wd_ref[0, 0].astype(BF16)

    @pl.when(i < n_used_ref[0])
    def _():
        x = xs_ref[...].astype(BF16)
        hgu = _dot(x, wgu_bf[...]) + bgu_ref[0, 0]
        gate = jnp.minimum(hgu[:, :d_ff], SWIGLU_LIMIT)
        up = jnp.clip(hgu[:, d_ff:], -SWIGLU_LIMIT, SWIGLU_LIMIT)
        act = gate * (1.0 / (1.0 + jnp.exp(-SWIGLU_ALPHA * gate))) * (up + 1.0)
        y_ref[...] = _dot(act.astype(BF16), wd_bf[...]) + bd_ref[0, 0]

    @pl.when(i >= n_used_ref[0])
    def _():
        y_ref[...] = jnp.zeros_like(y_ref)


def _expert_ffn(tile_expert, n_used, xs, w_gu, b_gu, w_down, b_down, layer):
    n_rows, d = xs.shape
    depth, e, _, f2 = w_gu.shape
    d_ff = w_down.shape[2]
    tm = TM_FFN
    grid_spec = pltpu.PrefetchScalarGridSpec(
        num_scalar_prefetch=2,
        grid=(n_rows // tm,),
        in_specs=[pl.BlockSpec((tm, d), lambda i, te, nu: (jnp.minimum(i, nu[0] - 1), 0)),
                  pl.BlockSpec((1, 1, d, f2), lambda i, te, nu: (layer, te[i], 0, 0)),
                  pl.BlockSpec((1, 1, 1, f2), lambda i, te, nu: (layer, te[i], 0, 0)),
                  pl.BlockSpec((1, 1, d_ff, d), lambda i, te, nu: (layer, te[i], 0, 0)),
                  pl.BlockSpec((1, 1, 1, d), lambda i, te, nu: (layer, te[i], 0, 0))],
        out_specs=pl.BlockSpec((tm, d), lambda i, te, nu: (i, 0)),
        scratch_shapes=[pltpu.VMEM((d, f2), BF16), pltpu.VMEM((d_ff, d), BF16)],
    )
    return pl.pallas_call(
        _ffn_kernel,
        grid_spec=grid_spec,
        out_shape=jax.ShapeDtypeStruct((n_rows, d), F32),
        compiler_params=_cparams("arbitrary"),
        name="expert_ffn",
    )(tile_expert, n_used, xs, w_gu, b_gu.reshape(depth, e, 1, f2), w_down, b_down.reshape(depth, e, 1, d))


def _combine_kernel(pos_ref, gate_ref, h_ref, g_ref, b_ref, y_hbm, out_ref, buf, sem, *, tc):
    def issue(i, _):
        for k in range(TOP_K):
            _row_copy(y_hbm, pos_ref[i * TOP_K + k], buf.at[k], i, sem).start()
        return 0

    lax.fori_loop(0, tc, issue, 0)

    def drain(i, _):
        for k in range(TOP_K):
            _row_copy(y_hbm, pos_ref[i * TOP_K + k], buf.at[k], i, sem).wait()
        return 0

    lax.fori_loop(0, tc, drain, 0)

    gates = gate_ref[...]
    ffn = gates[:, 0:1] * buf[0]
    for k in range(1, TOP_K):
        ffn = ffn + gates[:, k:k + 1] * buf[k]
    out_ref[...] = _layer_norm(DEEPNORM_ALPHA * h_ref[...] + ffn, g_ref[...], b_ref[...])


def _combine(pos_flat, gates, h, g, b, y):
    t, d = h.shape
    tc = min(TC_COMBINE, t)
    return pl.pallas_call(
        functools.partial(_combine_kernel, tc=tc),
        grid=(t // tc,),
        in_specs=[pl.BlockSpec((tc * TOP_K,), lambda i: (i,), memory_space=pltpu.SMEM),
                  pl.BlockSpec((tc, TOP_K), lambda i: (i, 0)),
                  pl.BlockSpec((tc, d), lambda i: (i, 0)),
                  pl.BlockSpec((1, d), lambda i: (0, 0)),
                  pl.BlockSpec((1, d), lambda i: (0, 0)),
                  pl.BlockSpec(memory_space=pl.ANY)],
        out_specs=pl.BlockSpec((tc, d), lambda i: (i, 0)),
        out_shape=jax.ShapeDtypeStruct((t, d), F32),
        scratch_shapes=[pltpu.VMEM((TOP_K, tc, d), F32), pltpu.SemaphoreType.DMA],
        compiler_params=_cparams("arbitrary"),
        name="moe_combine",
    )(pos_flat, gates, h, g, b, y)


def _moe(h, logits, w_gu, b_gu, w_down, b_down, g, b, layer):
    t = h.shape[0]
    idx, gates, rank, counts = _route(logits)
    counts = counts[0, :N_EXPERTS].astype(I32)
    tiles = (counts + TM_FFN - 1) // TM_FFN
    tile_end = jnp.cumsum(tiles)
    start = (tile_end - tiles) * TM_FFN
    n_tiles = (t * TOP_K) // TM_FFN + N_EXPERTS
    tile_ids = jnp.arange(n_tiles, dtype=I32)
    tile_expert = jnp.minimum(jnp.sum((tile_end[None, :] <= tile_ids[:, None]).astype(I32), axis=1), N_EXPERTS - 1)
    pos = (start[idx] + rank).reshape(-1).astype(I32)
    xs = _dispatch(start + counts, tiles * TM_FFN - counts, pos, h, n_tiles * TM_FFN)
    y = _expert_ffn(tile_expert, tile_end[-1:].astype(I32), xs, w_gu, b_gu, w_down, b_down, layer)
    return _combine(pos, gates, h, g, b, y)


def _router_params(router_w, router_b):
    rw = jnp.pad(router_w, ((0, 0), (0, LANES - N_EXPERTS))).astype(BF16)
    rb = jnp.pad(router_b, (0, LANES - N_EXPERTS)).reshape(1, LANES)
    return rw, rb


def _alibi_slopes(n):
    return 2.0 ** (-8.0 * jnp.arange(1, n + 1, dtype=F32) / n)


def _even_layer(h, bsz, seq, w_in, w_out, lq1, lk1, lq2, lk2, subw, layer, g, b, rw, rb):
    proj = _project(h, w_in.astype(BF16), BF16).reshape(bsz, seq, -1)
    o_a = _sb_attention(proj, bsz, seq)
    lam_init = 0.8 - 0.6 * math.exp(-0.3 * layer)
    row = lambda v: v.reshape(1, -1)
    o_b = _diff_attention(proj, _alibi_slopes(DIFF_HEADS), row(lq1), row(lk1), row(lq2), row(lk2), row(subw),
                          bsz, seq, lam_init)
    t = bsz * seq
    sbw = SB_HEADS * HEAD_DIM
    wo = w_out.astype(BF16)
    return _mix_call(_mix_even_kernel, "mix_even", [o_a.reshape(t, -1), o_b.reshape(t, -1)],
                     [wo[:sbw], wo[sbw:]], h, row(g), row(b), rw, rb)


def _odd_layer(h, bsz, seq, w_in, kvw, w_uv, w_out, g, b, rw, rb):
    qw = DSA_HEADS * DSA_LATENT
    c0, c1 = qw, qw + DSA_LATENT
    i0, i1 = c1, c1 + IDX_HEADS * IDX_DIM
    k1 = i1 + IDX_DIM
    w_main = jnp.concatenate([w_in[:, :qw], w_in[:, i0:i1], w_in[:, i1:k1], w_in[:, i1:k1]], axis=1).astype(BF16)
    w_small = jnp.pad(jnp.concatenate([w_in[:, c0:c1], w_in[:, k1:]], axis=1),
                      ((0, 0), (0, LANES - IDX_HEADS))).astype(BF16)
    proj = _project(h, w_main, BF16).reshape(bsz, seq, -1)
    small = _project(h, w_small, F32).reshape(bsz, seq, -1)
    o = _dsa_attention(proj, small, kvw.reshape(1, -1), bsz, seq)
    eye = jnp.eye(DSA_HEADS, dtype=w_uv.dtype)
    w_bd = (w_uv[:, :, None, :] * eye[:, None, :, None]).reshape(qw, DSA_HEADS * DSA_VDIM).astype(BF16)
    row = lambda v: v.reshape(1, -1)
    return _mix_call(_mix_odd_kernel, "mix_odd", [o.reshape(bsz * seq, -1)], [w_bd, w_out.astype(BF16)],
                     h, row(g), row(b), rw, rb)


def kernel(x, ev_w_in, ev_w_out, ev_lambda_q1, ev_lambda_k1, ev_lambda_q2, ev_lambda_k2, ev_subln_w, od_w_in, od_kv_norm_w, od_w_uv, od_w_out, ln_mix_g, ln_mix_b, router_w, router_b, exp_w_gu, exp_b_gu, exp_w_down, exp_b_down, ln_ffn_g, ln_ffn_b):
    bsz, seq, d = x.shape
    h = x.reshape(bsz * seq, d)
    for layer in range(ln_mix_g.shape[0]):
        j = layer // 2
        rw, rb = _router_params(router_w[layer], router_b[layer])
        if layer % 2 == 0:
            h, logits = _even_layer(h, bsz, seq, ev_w_in[j], ev_w_out[j], ev_lambda_q1[j], ev_lambda_k1[j],
                                    ev_lambda_q2[j], ev_lambda_k2[j], ev_subln_w[j], layer,
                                    ln_mix_g[layer], ln_mix_b[layer], rw, rb)
        else:
            h, logits = _odd_layer(h, bsz, seq, od_w_in[j], od_kv_norm_w[j], od_w_uv[j], od_w_out[j],
                                   ln_mix_g[layer], ln_mix_b[layer], rw, rb)
        h = _moe(h, logits, exp_w_gu, exp_b_gu, exp_w_down, exp_b_down,
                 ln_ffn_g[layer].reshape(1, -1), ln_ffn_b[layer].reshape(1, -1), layer)
    return h.reshape(bsz, seq, d)
```

```python
import functools
import math

import jax
import jax.numpy as jnp
from jax import lax
from jax.experimental import pallas as pl
from jax.experimental.pallas import tpu as pltpu

F32, BF16, I32 = jnp.float32, jnp.bfloat16, jnp.int32

HEAD_DIM = 64
SB_HEADS = 8
DIFF_HEADS = 4
DIFF_SUB = 64
DIFF_VDIM = 128
DSA_HEADS = 16
DSA_LATENT = 128
DSA_VDIM = 64
IDX_HEADS = 8
IDX_DIM = 64
TOPK_MAX = 256
N_EXPERTS = 32
TOP_K = 4
SWIGLU_LIMIT = 7.0
SWIGLU_ALPHA = 1.702
LN_EPS = 1e-5
RMS_EPS = 1e-5
DEPTH = 2
DEEPNORM_ALPHA = (2 * DEPTH) ** 0.25

LANES = 128
SUBLANES = 8
VMEM_LIMIT = 56 * 1024 * 1024
INT_MIN = -(2 ** 31)
MASKED = -1e30
DEAD_LOG2 = -152.0
LOG2_E = math.log2(math.e)

TM_PROJ = 512
TQ_ATT = 256
TQ_DSA = 128
TRI = 512
HG_DSA = 16
TM_ROUTE = 512
TM_FFN = 512
TD_DISPATCH = 512
TC_COMBINE = 256


def _cparams(*sem):
    return pltpu.CompilerParams(dimension_semantics=sem, vmem_limit_bytes=VMEM_LIMIT)


def _nt_dot(a, b):
    return lax.dot_general(a, b, (((1,), (1,)), ((), ())), preferred_element_type=F32)


def _dot(a, b):
    return jnp.dot(a, b, preferred_element_type=F32)


def _layer_norm(y, g, b):
    mu = jnp.mean(y, axis=-1, keepdims=True)
    d = y - mu
    var = jnp.mean(d * d, axis=-1, keepdims=True)
    return d * lax.rsqrt(var + LN_EPS) * g + b


def _col_chunk(n):
    for c in (512, 384, 256, 128):
        if n % c == 0:
            return c
    raise ValueError(n)


def _proj_kernel(x_ref, w_ref, o_ref):
    x = x_ref[...].astype(BF16)
    n = o_ref.shape[-1]
    c = _col_chunk(n)
    for j in range(0, n, c):
        o_ref[:, j:j + c] = _dot(x, w_ref[:, j:j + c]).astype(o_ref.dtype)


def _project(x, w, out_dtype):
    t, k = x.shape
    n = w.shape[1]
    tm = min(TM_PROJ, t)
    return pl.pallas_call(
        _proj_kernel,
        grid=(t // tm,),
        in_specs=[pl.BlockSpec((tm, k), lambda i: (i, 0)),
                  pl.BlockSpec((k, n), lambda i: (0, 0))],
        out_specs=pl.BlockSpec((tm, n), lambda i: (i, 0)),
        out_shape=jax.ShapeDtypeStruct((t, n), out_dtype),
        compiler_params=_cparams("parallel"),
        name="in_proj",
    )(x, w)


def _stack_halves(x, lane):
    zero = jnp.zeros_like(x)
    return jnp.concatenate([jnp.where(lane < HEAD_DIM, x, zero), jnp.where(lane >= HEAD_DIM, x, zero)], axis=0)


def _sb_kernel(q_ref, k_ref, v_ref, o_ref, *, tq):
    i = pl.program_id(2)
    scale = HEAD_DIM ** -0.5
    lane = lax.broadcasted_iota(I32, (1, LANES), 1)
    row = lax.broadcasted_iota(I32, (2 * tq, tq), 0)
    row = jnp.where(row >= tq, row - tq, row)
    col = lax.broadcasted_iota(I32, (2 * tq, tq), 1)
    strict = col < row
    r1 = lax.broadcasted_iota(I32, (tq, tq), 0)
    c1 = lax.broadcasted_iota(I32, (tq, tq), 1)
    after = (r1 > c1).astype(BF16)
    q2 = _stack_halves(q_ref[0], lane)

    def block(j, carry, masked):
        run, acc = carry
        start = pl.multiple_of(j * tq, tq)
        kb = k_ref[0, pl.ds(start, tq), :]
        vb = v_ref[0, pl.ds(start, tq), :]
        z2 = _nt_dot(q2, kb) * (scale * LOG2_E)
        neg = -z2
        keep2 = jnp.minimum(neg, 0.0) - jnp.log2(1.0 + jnp.exp2(jnp.minimum(z2, neg)))
        lk = jnp.where(strict, keep2, 0.0) if masked else keep2
        hi = lk.astype(BF16)
        lo = (lk - hi.astype(F32)).astype(BF16)
        suffix = _dot(hi, after) + _dot(lo, after)
        w = jnp.exp2(z2 + keep2 + suffix + run)
        if masked:
            w = jnp.where(strict, w, 0.0)
        wb = w.astype(BF16)
        zero = jnp.zeros_like(vb)
        acc = (acc + _dot(wb[:tq], jnp.where(lane < HEAD_DIM, vb, zero))
               + _dot(wb[tq:], jnp.where(lane >= HEAD_DIM, vb, zero)))
        run = run + suffix[:, 0:1] + lk[:, 0:1]
        return run, acc

    run, acc = block(i, (jnp.zeros((2 * tq, 1), F32), jnp.zeros((tq, LANES), F32)), True)

    def alive(c):
        return jnp.logical_and(c[0] >= 0, c[1] > DEAD_LOG2)

    def step(c):
        j, _, run, acc = c
        run, acc = block(j, (run, acc), False)
        return j - 1, jnp.max(run), run, acc

    _, _, _, acc = lax.while_loop(alive, step, (i - 1, jnp.max(run), run, acc))
    o_ref[0] = acc.astype(o_ref.dtype)


def _sb_attention(proj, bsz, seq):
    tq = min(TQ_ATT, seq)
    pairs = SB_HEADS * HEAD_DIM // LANES
    return pl.pallas_call(
        functools.partial(_sb_kernel, tq=tq),
        grid=(bsz, pairs, seq // tq),
        in_specs=[pl.BlockSpec((1, tq, LANES), lambda b, p, i: (b, i, p)),
                  pl.BlockSpec((1, seq, LANES), lambda b, p, i: (b, 0, pairs + p)),
                  pl.BlockSpec((1, seq, LANES), lambda b, p, i: (b, 0, 2 * pairs + p))],
        out_specs=pl.BlockSpec((1, tq, LANES), lambda b, p, i: (b, i, p)),
        out_shape=jax.ShapeDtypeStruct((bsz, seq, SB_HEADS * HEAD_DIM), BF16),
        compiler_params=_cparams("parallel", "parallel", "parallel"),
        name="stickbreak_attn",
    )(proj, proj, proj)


def _diff_kernel(slope_ref, lq1_ref, lk1_ref, lq2_ref, lk2_ref, subw_ref, q_ref, k_ref, v_ref, o_ref,
                 *, tq, lam_init):
    h = pl.program_id(1)
    i = pl.program_id(2)
    scale = DIFF_SUB ** -0.5
    slope = slope_ref[h]
    lam = (jnp.exp(jnp.sum(lq1_ref[...] * lk1_ref[...], keepdims=True))
           - jnp.exp(jnp.sum(lq2_ref[...] * lk2_ref[...], keepdims=True)) + lam_init)
    lane = lax.broadcasted_iota(I32, (1, LANES), 1)
    row = lax.broadcasted_iota(I32, (2 * tq, tq), 0)
    row = jnp.where(row >= tq, row - tq, row)
    col = lax.broadcasted_iota(I32, (2 * tq, tq), 1)
    causal = col <= row
    key_off = lax.broadcasted_iota(I32, (1, tq), 1)
    q2 = _stack_halves(q_ref[0], lane)

    def block(j, carry, masked):
        m, l, acc = carry
        start = j * tq
        kb = k_ref[0, pl.ds(start, tq), :]
        vb = v_ref[0, pl.ds(start, tq), :]
        s = _nt_dot(q2, kb) * (scale * LOG2_E) + (slope * LOG2_E) * (start + key_off).astype(F32)
        if masked:
            s = jnp.where(causal, s, -jnp.inf)
        m_new = jnp.maximum(m, jnp.max(s, axis=-1, keepdims=True))
        a = jnp.exp2(m - m_new)
        p = jnp.exp2(s - m_new)
        l = a * l + jnp.sum(p, axis=-1, keepdims=True)
        acc = a * acc + _dot(p.astype(BF16), vb)
        return m_new, l, acc

    for n_left in range(k_ref.shape[1] // tq):
        @pl.when(i == n_left)
        def _(n_left=n_left):
            carry = (jnp.full((2 * tq, 1), -jnp.inf, F32), jnp.zeros((2 * tq, 1), F32),
                     jnp.zeros((2 * tq, LANES), F32))
            carry = block(n_left, carry, True)
            for j in range(n_left):
                carry = block(j, carry, False)
            out = carry[2] / carry[1]
            o = out[:tq] - lam * out[tq:]
            o = o * lax.rsqrt(jnp.mean(o * o, axis=-1, keepdims=True) + RMS_EPS) * subw_ref[...]
            o_ref[0] = (o * (1.0 - lam_init)).astype(o_ref.dtype)


def _diff_attention(proj, slopes, lq1, lk1, lq2, lk2, subw, bsz, seq, lam_init):
    tq = min(TQ_ATT, seq)
    q0 = 3 * SB_HEADS * HEAD_DIM // LANES
    k0 = q0 + DIFF_HEADS
    v0 = k0 + DIFF_HEADS
    vec = lambda n: pl.BlockSpec((1, n), lambda b, h, i: (0, 0))
    return pl.pallas_call(
        functools.partial(_diff_kernel, tq=tq, lam_init=lam_init),
        grid=(bsz, DIFF_HEADS, seq // tq),
        in_specs=[pl.BlockSpec(memory_space=pltpu.SMEM),
                  vec(DIFF_SUB), vec(DIFF_SUB), vec(DIFF_SUB), vec(DIFF_SUB), vec(DIFF_VDIM),
                  pl.BlockSpec((1, tq, LANES), lambda b, h, i: (b, i, q0 + h)),
                  pl.BlockSpec((1, seq, LANES), lambda b, h, i: (b, 0, k0 + h)),
                  pl.BlockSpec((1, seq, LANES), lambda b, h, i: (b, 0, v0 + h))],
        out_specs=pl.BlockSpec((1, tq, LANES), lambda b, h, i: (b, i, h)),
        out_shape=jax.ShapeDtypeStruct((bsz, seq, DIFF_HEADS * DIFF_VDIM), BF16),
        compiler_params=_cparams("parallel", "parallel", "parallel"),
        name="diff_attn",
    )(slopes, lq1, lk1, lq2, lk2, subw, proj, proj, proj)


def _tree_sum_rows(x):
    n = x.shape[0] // 8
    x = x.reshape(n, 8, x.shape[1])
    while n > 1:
        n //= 2
        x = x[:n] + x[n:]
    return x[0]


def _tree_max_rows(x):
    n = x.shape[0] // 8
    x = x.reshape(n, 8, x.shape[1])
    while n > 1:
        n //= 2
        x = jnp.maximum(x[:n], x[n:])
    return x[0]


def _dsa_kernel(q_ref, ckv_ref, kvw_ref, qi_ref, ki_ref, wi_ref, qx_ref, kx_ref, o_ref,
                cn_ref, cnt_ref, skey_ref, bias_ref, s_ref, *, tq, tk, k_sel, hg):
    i = pl.program_id(1)
    t0 = i * tq

    @pl.when(i == 0)
    def _():
        c = ckv_ref[0]
        cn = c * lax.rsqrt(jnp.mean(c * c, axis=-1, keepdims=True) + RMS_EPS) * kvw_ref[...]
        cn_ref[:, :DSA_LATENT] = cn.astype(BF16)
        cn_ref[:, DSA_LATENT:] = kx_ref[...]
        cnt_ref[...] = cn.T.astype(BF16)

    for k in range(ckv_ref.shape[1] // tk):
        @pl.when((t0 + tq - 1) // tk == k)
        def _(k=k):
            _dsa_block(q_ref, qi_ref, ki_ref, wi_ref, qx_ref, o_ref, cn_ref, cnt_ref, skey_ref, bias_ref, s_ref,
                       t0=t0, n_tiles=k + 1, tq=tq, tk=tk, k_sel=k_sel, hg=hg)


def _static_loop(n, body, carry):
    for j in range(n):
        carry = body(j, carry)
    return carry


def _dsa_block(q_ref, qi_ref, ki_ref, wi_ref, qx_ref, o_ref, cn_ref, cnt_ref, skey_ref, bias_ref, s_ref,
               *, t0, n_tiles, tq, tk, k_sel, hg):
    def tile_slice(j):
        return pl.ds(j * tk, tk)

    lane = lax.broadcasted_iota(I32, (1, LANES), 1)
    q_pos = t0 + lax.broadcasted_iota(I32, (tk, tq), 1)
    key_off = lax.broadcasted_iota(I32, (tk, tq), 0)

    w_t = (wi_ref[0] * (IDX_HEADS ** -0.5)).T
    idx_scale = IDX_DIM ** -0.5
    stacked = []
    for h in range(IDX_HEADS):
        qp = qi_ref[0, :, (h // 2) * LANES:(h // 2 + 1) * LANES]
        stacked.append(jnp.where((lane // IDX_DIM) == (h % 2), qp, jnp.zeros_like(qp)))
    q8 = jnp.concatenate(stacked, axis=0)

    def score_tile(j, _):
        d = _nt_dot(ki_ref[0, tile_slice(j), :], q8)
        score = jnp.zeros((tk, tq), F32)
        for h in range(IDX_HEADS):
            score = score + jnp.maximum(d[:, h * tq:(h + 1) * tq] * idx_scale, 0.0) * w_t[h:h + 1, :]
        score = jnp.where(score == 0.0, 0.0, score)
        bits = pltpu.bitcast(score, I32)
        skey = jnp.where(bits < 0, bits ^ 0x7FFFFFFF, bits)
        skey_ref[tile_slice(j), :] = jnp.where(j * tk + key_off <= q_pos, skey, INT_MIN)
        return 0

    _static_loop(n_tiles, score_tile, 0)

    def count(pred):
        def tile(j, cnt):
            return cnt + _tree_sum_rows(pred(skey_ref[tile_slice(j), :]).astype(F32))
        cnt = _static_loop(n_tiles, tile, jnp.zeros((8, tq), F32))
        return jnp.sum(cnt, axis=0, keepdims=True)

    kf = float(k_sel)
    thr = jnp.where(count(lambda sk: sk >= 0) >= kf, 0, INT_MIN).astype(I32)

    def bit_step(b, thr):
        cand = thr + jnp.left_shift(jnp.int32(1), 30 - b)
        return jnp.where(count(lambda sk: sk >= cand) >= kf, cand, thr)

    thr = lax.fori_loop(0, 31, bit_step, thr)
    need = kf - count(lambda sk: sk > thr)

    r2 = lax.broadcasted_iota(I32, (tk, tk), 0)
    c2 = lax.broadcasted_iota(I32, (tk, tk), 1)
    upto = (c2 <= r2).astype(BF16)

    def tie_tile(j, seen):
        sk = skey_ref[tile_slice(j), :]
        tied = sk == thr
        prefix = _dot(upto, tied.astype(F32).astype(BF16)) + seen
        selected = ((sk > thr) | (tied & (prefix <= need))) & (j * tk + key_off <= q_pos)
        bias_ref[tile_slice(j), :] = jnp.where(selected, 0.0, MASKED)
        return prefix[tk - 1:tk, :]

    _static_loop(n_tiles, tie_tile, jnp.zeros((1, tq), F32))

    log2_scale = (DSA_LATENT ** -0.5) * LOG2_E

    def head_group(g, _):
        heads = [g * hg + u for u in range(hg)]
        offs = [pl.multiple_of(h * DSA_LATENT, DSA_LATENT) for h in heads]
        qg = jnp.concatenate(
            [jnp.concatenate([q_ref[0, :, pl.ds(off, DSA_LATENT)],
                              jnp.broadcast_to(qx_ref[pl.ds(h, 1), :], (tq, LANES)).astype(BF16)], axis=1)
             for h, off in zip(heads, offs)], axis=0)

        def logits_tile(j, m):
            d = _nt_dot(cn_ref[tile_slice(j), :], qg)
            b = bias_ref[tile_slice(j), :]
            tops = []
            for u in range(hg):
                s = d[:, u * tq:(u + 1) * tq] * log2_scale + b
                s_ref[tile_slice(j), u * tq:(u + 1) * tq] = s
                tops.append(jnp.max(_tree_max_rows(s), axis=0, keepdims=True))
            return jnp.maximum(m, jnp.concatenate(tops, axis=1))

        m = _static_loop(n_tiles, logits_tile, jnp.full((1, hg * tq), MASKED, F32))

        def value_tile(j, carry):
            l, acc = carry
            p = jnp.exp2(s_ref[tile_slice(j), :] - m)
            l = l + jnp.sum(_tree_sum_rows(p), axis=0, keepdims=True)
            acc = acc + _dot(cnt_ref[:, tile_slice(j)], p.astype(BF16))
            return l, acc

        l, acc = _static_loop(n_tiles, value_tile,
                              (jnp.zeros((1, hg * tq), F32), jnp.zeros((DSA_LATENT, hg * tq), F32)))
        out = acc / l
        for u, off in enumerate(offs):
            o_ref[0, :, pl.ds(off, DSA_LATENT)] = out[:, u * tq:(u + 1) * tq].T.astype(o_ref.dtype)
        return 0

    lax.fori_loop(0, DSA_HEADS // hg, head_group, 0)


def _bf16_parts(x, n):
    parts = []
    for _ in range(n):
        p = x.astype(BF16).astype(F32)
        parts.append(p)
        x = x - p
    return parts


def _dsa_attention(proj, small, kvw, bsz, seq):
    tq = min(TQ_DSA, seq)
    tk = min(TRI, seq)
    k_sel = min(TOPK_MAX, seq // 4)
    qw = DSA_HEADS * DSA_LATENT
    qiw = IDX_HEADS * IDX_DIM
    slope = _alibi_slopes(DSA_HEADS) / (DSA_LATENT ** -0.5)
    s_parts = _bf16_parts(slope, 3)
    pos = jnp.arange(seq, dtype=I32)
    p_parts = [(pos // LANES * LANES).astype(F32), (pos % LANES).astype(F32)]
    qx = jnp.stack([sp for _ in p_parts for sp in s_parts], axis=1)
    kx = jnp.stack([pp for pp in p_parts for _ in s_parts], axis=1)
    qx = jnp.pad(qx, ((0, 0), (0, LANES - qx.shape[1])))
    kx = jnp.pad(kx, ((0, 0), (0, LANES - kx.shape[1]))).astype(BF16)
    return pl.pallas_call(
        functools.partial(_dsa_kernel, tq=tq, tk=tk, k_sel=k_sel, hg=HG_DSA),
        grid=(bsz, seq // tq),
        in_specs=[pl.BlockSpec((1, tq, qw), lambda b, i: (b, i, 0)),
                  pl.BlockSpec((1, seq, DSA_LATENT), lambda b, i: (b, 0, 0)),
                  pl.BlockSpec((1, DSA_LATENT), lambda b, i: (0, 0)),
                  pl.BlockSpec((1, tq, qiw), lambda b, i: (b, i, qw // qiw)),
                  pl.BlockSpec((1, seq, LANES), lambda b, i: (b, 0, (qw + qiw) // LANES)),
                  pl.BlockSpec((1, tq, LANES), lambda b, i: (b, i, 1)),
                  pl.BlockSpec((DSA_HEADS, LANES), lambda b, i: (0, 0)),
                  pl.BlockSpec((seq, LANES), lambda b, i: (0, 0))],
        out_specs=pl.BlockSpec((1, tq, qw), lambda b, i: (b, i, 0)),
        out_shape=jax.ShapeDtypeStruct((bsz, seq, qw), BF16),
        scratch_shapes=[pltpu.VMEM((seq, DSA_LATENT + LANES), BF16), pltpu.VMEM((DSA_LATENT, seq), BF16),
                        pltpu.VMEM((seq, tq), I32), pltpu.VMEM((seq, tq), F32),
                        pltpu.VMEM((seq, HG_DSA * tq), F32)],
        compiler_params=_cparams("parallel", "arbitrary"),
        name="dsa_attn",
    )(proj, small, kvw, proj, proj, small, qx, kx)


def _mix_tail(mix, h_ref, g_ref, b_ref, rw_ref, rb_ref, hout_ref, logit_ref):
    hn = _layer_norm(DEEPNORM_ALPHA * h_ref[...] + mix, g_ref[...], b_ref[...])
    hout_ref[...] = hn
    logit_ref[...] = _dot(hn.astype(BF16), rw_ref[...]) + rb_ref[...]


def _mix_even_kernel(oa_ref, ob_ref, wa_ref, wb_ref, h_ref, g_ref, b_ref, rw_ref, rb_ref, hout_ref, logit_ref):
    mix = _dot(oa_ref[...], wa_ref[...]) + _dot(ob_ref[...], wb_ref[...])
    _mix_tail(mix, h_ref, g_ref, b_ref, rw_ref, rb_ref, hout_ref, logit_ref)


def _mix_odd_kernel(o_ref, wuv_ref, wo_ref, h_ref, g_ref, b_ref, rw_ref, rb_ref, hout_ref, logit_ref):
    up = _dot(o_ref[...], wuv_ref[...]).astype(BF16)
    _mix_tail(_dot(up, wo_ref[...]), h_ref, g_ref, b_ref, rw_ref, rb_ref, hout_ref, logit_ref)


def _mix_call(kernel, name, acts, weights, h, g, b, rw, rb):
    t, d = h.shape
    tm = min(TM_PROJ, t)
    row = lambda n: pl.BlockSpec((tm, n), lambda i: (i, 0))
    full = lambda a: pl.BlockSpec(a.shape, lambda i: (0, 0))
    return pl.pallas_call(
        kernel,
        grid=(t // tm,),
        in_specs=[row(a.shape[1]) for a in acts] + [full(w) for w in weights]
                 + [row(d), full(g), full(b), full(rw), full(rb)],
        out_specs=[row(d), row(LANES)],
        out_shape=[jax.ShapeDtypeStruct((t, d), F32), jax.ShapeDtypeStruct((t, LANES), F32)],
        compiler_params=_cparams("parallel"),
        name=name,
    )(*acts, *weights, h, g, b, rw, rb)


def _route_kernel(logit_ref, idx_ref, gate_ref, rank_ref, cnt_ref, *, tm):
    @pl.when(pl.program_id(0) == 0)
    def _():
        cnt_ref[...] = jnp.zeros_like(cnt_ref)

    lane = lax.broadcasted_iota(I32, (tm, LANES), 1)
    lane_f = lane.astype(F32)
    x = jnp.where(lane < N_EXPERTS, logit_ref[...], -jnp.inf)
    vals, hots = [], []
    for k in range(TOP_K):
        m = jnp.max(x, axis=-1, keepdims=True)
        first = jnp.min(jnp.where(x == m, lane_f, float(LANES)), axis=-1, keepdims=True)
        hot = lane_f == first
        x = jnp.where(hot, -jnp.inf, x)
        vals.append(m)
        hots.append(hot)
        idx_ref[:, k:k + 1] = first.astype(I32)
    exps = [jnp.exp(v - vals[0]) for v in vals]
    denom = exps[0] + exps[1] + exps[2] + exps[3]
    for k in range(TOP_K):
        gate_ref[:, k:k + 1] = exps[k] / denom

    hot_sum = (hots[0] | hots[1] | hots[2] | hots[3]).astype(F32)
    r = lax.broadcasted_iota(I32, (tm, tm), 0)
    c = lax.broadcasted_iota(I32, (tm, tm), 1)
    earlier = (c < r).astype(BF16)
    before = _dot(earlier, hot_sum.astype(BF16)) + cnt_ref[...]
    for k in range(TOP_K):
        rank_ref[:, k:k + 1] = jnp.sum(jnp.where(hots[k], before, 0.0), axis=-1, keepdims=True).astype(I32)
    cnt_ref[...] += jnp.sum(hot_sum, axis=0, keepdims=True)


def _route(logits):
    t = logits.shape[0]
    tm = min(TM_ROUTE, t)
    narrow = pl.BlockSpec((tm, TOP_K), lambda i: (i, 0))
    return pl.pallas_call(
        functools.partial(_route_kernel, tm=tm),
        grid=(t // tm,),
        in_specs=[pl.BlockSpec((tm, LANES), lambda i: (i, 0))],
        out_specs=[narrow, narrow, narrow, pl.BlockSpec((1, LANES), lambda i: (0, 0))],
        out_shape=[jax.ShapeDtypeStruct((t, TOP_K), I32), jax.ShapeDtypeStruct((t, TOP_K), F32),
                   jax.ShapeDtypeStruct((t, TOP_K), I32), jax.ShapeDtypeStruct((1, LANES), F32)],
        compiler_params=_cparams("arbitrary"),
        name="route",
    )(logits)


def _row_copy(src_ref, src_row, dst_ref, dst_row, sem):
    return pltpu.make_async_copy(src_ref.at[pl.ds(src_row, 1), :], dst_ref.at[pl.ds(dst_row, 1), :], sem)


def _dispatch_kernel(pad_start_ref, pad_count_ref, pos_ref, x_ref, xs_out, zero_ref, sem, pad_sem, *, td):
    @pl.when(pl.program_id(0) == 0)
    def _():
        zero_ref[...] = jnp.zeros_like(zero_ref)

        def pads(e, start_not_wait):
            def fill(present, n_rows, off):
                copy = pltpu.make_async_copy(zero_ref.at[pl.ds(0, n_rows), :], xs_out.at[pl.ds(off, n_rows), :],
                                             pad_sem)

                @pl.when(jnp.logical_and(present, start_not_wait))
                def _():
                    copy.start()

                @pl.when(jnp.logical_and(present, jnp.logical_not(start_not_wait)))
                def _():
                    copy.wait()

            off = pad_start_ref[e]
            count = pad_count_ref[e]
            head = jnp.minimum(count, (-off) & (SUBLANES - 1))
            for r in range(SUBLANES - 1):
                fill(r < head, 1, off + r)
            off = off + head
            count = count - head
            chunk = zero_ref.shape[0] // 2
            while chunk >= SUBLANES:
                present = (count & chunk) != 0
                fill(present, chunk, pl.multiple_of(off, SUBLANES))
                off = off + jnp.where(present, chunk, 0)
                chunk //= 2
            for r in range(SUBLANES - 1):
                fill(r < (count & (SUBLANES - 1)), 1, off + r)
            return start_not_wait

        lax.fori_loop(0, N_EXPERTS, pads, True)
        lax.fori_loop(0, N_EXPERTS, pads, False)

    def issue(i, _):
        for k in range(TOP_K):
            _row_copy(x_ref, i, xs_out, pos_ref[i * TOP_K + k], sem).start()
        return 0

    lax.fori_loop(0, td, issue, 0)

    def drain(i, _):
        for k in range(TOP_K):
            _row_copy(x_ref, i, xs_out, pos_ref[i * TOP_K + k], sem).wait()
        return 0

    lax.fori_loop(0, td, drain, 0)


def _dispatch(pad_start, pad_count, pos_flat, x, n_rows):
    t, d = x.shape
    td = min(TD_DISPATCH, t)
    grid_spec = pltpu.PrefetchScalarGridSpec(
        num_scalar_prefetch=2,
        grid=(t // td,),
        in_specs=[pl.BlockSpec((td * TOP_K,), lambda i, ps, pc: (i,), memory_space=pltpu.SMEM),
                  pl.BlockSpec((td, d), lambda i, ps, pc: (i, 0))],
        out_specs=pl.BlockSpec(memory_space=pl.ANY),
        scratch_shapes=[pltpu.VMEM((TM_FFN, d), x.dtype), pltpu.SemaphoreType.DMA, pltpu.SemaphoreType.DMA],
    )
    return pl.pallas_call(
        functools.partial(_dispatch_kernel, td=td),
        grid_spec=grid_spec,
        out_shape=jax.ShapeDtypeStruct((n_rows, d), x.dtype),
        compiler_params=_cparams("arbitrary"),
        name="moe_dispatch",
    )(pad_start, pad_count, pos_flat, x)


def _ffn_kernel(tile_expert_ref, n_used_ref, xs_ref, wgu_ref, bgu_ref, wd_ref, bd_ref, y_ref, wgu_bf, wd_bf):
    i = pl.program_id(0)
    d_ff = wd_ref.shape[2]
    new_expert = jnp.logical_or(i == 0, tile_expert_ref[i] != tile_expert_ref[jnp.maximum(i - 1, 0)])

    @pl.when(jnp.logical_and(new_expert, i < n_used_ref[0]))
    def _():
        wgu_bf[...] = wgu_ref[0, 0].astype(BF16)
        wd_bf[...] = wd_ref[0, 0].astype(BF16)

    @pl.when(i < n_used_ref[0])
    def _():
        x = xs_ref[...].astype(BF16)
        hgu = _dot(x, wgu_bf[...]) + bgu_ref[0, 0]
        gate = jnp.minimum(hgu[:, :d_ff], SWIGLU_LIMIT)
        up = jnp.clip(hgu[:, d_ff:], -SWIGLU_LIMIT, SWIGLU_LIMIT)
        act = gate * (1.0 / (1.0 + jnp.exp(-SWIGLU_ALPHA * gate))) * (up + 1.0)
        y_ref[...] = _dot(act.astype(BF16), wd_bf[...]) + bd_ref[0, 0]

    @pl.when(i >= n_used_ref[0])
    def _():
        y_ref[...] = jnp.zeros_like(y_ref)


def _expert_ffn(tile_expert, n_used, xs, w_gu, b_gu, w_down, b_down, layer):
    n_rows, d = xs.shape
    depth, e, _, f2 = w_gu.shape
    d_ff = w_down.shape[2]
    tm = TM_FFN
    grid_spec = pltpu.PrefetchScalarGridSpec(
        num_scalar_prefetch=2,
        grid=(n_rows // tm,),
        in_specs=[pl.BlockSpec((tm, d), lambda i, te, nu: (jnp.minimum(i, nu[0] - 1), 0)),
                  pl.BlockSpec((1, 1, d, f2), lambda i, te, nu: (layer, te[i], 0, 0)),
                  pl.BlockSpec((1, 1, 1, f2), lambda i, te, nu: (layer, te[i], 0, 0)),
                  pl.BlockSpec((1, 1, d_ff, d), lambda i, te, nu: (layer, te[i], 0, 0)),
                  pl.BlockSpec((1, 1, 1, d), lambda i, te, nu: (layer, te[i], 0, 0))],
        out_specs=pl.BlockSpec((tm, d), lambda i, te, nu: (i, 0)),
        scratch_shapes=[pltpu.VMEM((d, f2), BF16), pltpu.VMEM((d_ff, d), BF16)],
    )
    return pl.pallas_call(
        _ffn_kernel,
        grid_spec=grid_spec,
        out_shape=jax.ShapeDtypeStruct((n_rows, d), F32),
        compiler_params=_cparams("arbitrary"),
        name="expert_ffn",
    )(tile_expert, n_used, xs, w_gu, b_gu.reshape(depth, e, 1, f2), w_down, b_down.reshape(depth, e, 1, d))


def _combine_kernel(pos_ref, gate_ref, h_ref, g_ref, b_ref, y_hbm, out_ref, buf, sem, *, tc):
    def issue(i, _):
        for k in range(TOP_K):
            _row_copy(y_hbm, pos_ref[i * TOP_K + k], buf.at[k], i, sem).start()
        return 0

    lax.fori_loop(0, tc, issue, 0)

    def drain(i, _):
        for k in range(TOP_K):
            _row_copy(y_hbm, pos_ref[i * TOP_K + k], buf.at[k], i, sem).wait()
        return 0

    lax.fori_loop(0, tc, drain, 0)

    gates = gate_ref[...]
    ffn = gates[:, 0:1] * buf[0]
    for k in range(1, TOP_K):
        ffn = ffn + gates[:, k:k + 1] * buf[k]
    out_ref[...] = _layer_norm(DEEPNORM_ALPHA * h_ref[...] + ffn, g_ref[...], b_ref[...])


def _combine(pos_flat, gates, h, g, b, y):
    t, d = h.shape
    tc = min(TC_COMBINE, t)
    return pl.pallas_call(
        functools.partial(_combine_kernel, tc=tc),
        grid=(t // tc,),
        in_specs=[pl.BlockSpec((tc * TOP_K,), lambda i: (i,), memory_space=pltpu.SMEM),
                  pl.BlockSpec((tc, TOP_K), lambda i: (i, 0)),
                  pl.BlockSpec((tc, d), lambda i: (i, 0)),
                  pl.BlockSpec((1, d), lambda i: (0, 0)),
                  pl.BlockSpec((1, d), lambda i: (0, 0)),
                  pl.BlockSpec(memory_space=pl.ANY)],
        out_specs=pl.BlockSpec((tc, d), lambda i: (i, 0)),
        out_shape=jax.ShapeDtypeStruct((t, d), F32),
        scratch_shapes=[pltpu.VMEM((TOP_K, tc, d), F32), pltpu.SemaphoreType.DMA],
        compiler_params=_cparams("arbitrary"),
        name="moe_combine",
    )(pos_flat, gates, h, g, b, y)


def _moe(h, logits, w_gu, b_gu, w_down, b_down, g, b, layer):
    t = h.shape[0]
    idx, gates, rank, counts = _route(logits)
    counts = counts[0, :N_EXPERTS].astype(I32)
    tiles = (counts + TM_FFN - 1) // TM_FFN
    tile_end = jnp.cumsum(tiles)
    start = (tile_end - tiles) * TM_FFN
    n_tiles = (t * TOP_K) // TM_FFN + N_EXPERTS
    tile_ids = jnp.arange(n_tiles, dtype=I32)
    tile_expert = jnp.minimum(jnp.sum((tile_end[None, :] <= tile_ids[:, None]).astype(I32), axis=1), N_EXPERTS - 1)
    pos = (start[idx] + rank).reshape(-1).astype(I32)
    xs = _dispatch(start + counts, tiles * TM_FFN - counts, pos, h, n_tiles * TM_FFN)
    y = _expert_ffn(tile_expert, tile_end[-1:].astype(I32), xs, w_gu, b_gu, w_down, b_down, layer)
    return _combine(pos, gates, h, g, b, y)


def _router_params(router_w, router_b):
    rw = jnp.pad(router_w, ((0, 0), (0, LANES - N_EXPERTS))).astype(BF16)
    rb = jnp.pad(router_b, (0, LANES - N_EXPERTS)).reshape(1, LANES)
    return rw, rb


def _alibi_slopes(n):
    return 2.0 ** (-8.0 * jnp.arange(1, n + 1, dtype=F32) / n)


def _even_layer(h, bsz, seq, w_in, w_out, lq1, lk1, lq2, lk2, subw, layer, g, b, rw, rb):
    proj = _project(h, w_in.astype(BF16), BF16).reshape(bsz, seq, -1)
    o_a = _sb_attention(proj, bsz, seq)
    lam_init = 0.8 - 0.6 * math.exp(-0.3 * layer)
    row = lambda v: v.reshape(1, -1)
    o_b = _diff_attention(proj, _alibi_slopes(DIFF_HEADS), row(lq1), row(lk1), row(lq2), row(lk2), row(subw),
                          bsz, seq, lam_init)
    t = bsz * seq
    sbw = SB_HEADS * HEAD_DIM
    wo = w_out.astype(BF16)
    return _mix_call(_mix_even_kernel, "mix_even", [o_a.reshape(t, -1), o_b.reshape(t, -1)],
                     [wo[:sbw], wo[sbw:]], h, row(g), row(b), rw, rb)


def _odd_layer(h, bsz, seq, w_in, kvw, w_uv, w_out, g, b, rw, rb):
    qw = DSA_HEADS * DSA_LATENT
    c0, c1 = qw, qw + DSA_LATENT
    i0, i1 = c1, c1 + IDX_HEADS * IDX_DIM
    k1 = i1 + IDX_DIM
    w_main = jnp.concatenate([w_in[:, :qw], w_in[:, i0:i1], w_in[:, i1:k1], w_in[:, i1:k1]], axis=1).astype(BF16)
    w_small = jnp.pad(jnp.concatenate([w_in[:, c0:c1], w_in[:, k1:]], axis=1),
                      ((0, 0), (0, LANES - IDX_HEADS))).astype(BF16)
    proj = _project(h, w_main, BF16).reshape(bsz, seq, -1)
    small = _project(h, w_small, F32).reshape(bsz, seq, -1)
    o = _dsa_attention(proj, small, kvw.reshape(1, -1), bsz, seq)
    eye = jnp.eye(DSA_HEADS, dtype=w_uv.dtype)
    w_bd = (w_uv[:, :, None, :] * eye[:, None, :, None]).reshape(qw, DSA_HEADS * DSA_VDIM).astype(BF16)
    row = lambda v: v.reshape(1, -1)
    return _mix_call(_mix_odd_kernel, "mix_odd", [o.reshape(bsz * seq, -1)], [w_bd, w_out.astype(BF16)],
                     h, row(g), row(b), rw, rb)


def kernel(x, ev_w_in, ev_w_out, ev_lambda_q1, ev_lambda_k1, ev_lambda_q2, ev_lambda_k2, ev_subln_w, od_w_in, od_kv_norm_w, od_w_uv, od_w_out, ln_mix_g, ln_mix_b, router_w, router_b, exp_w_gu, exp_b_gu, exp_w_down, exp_b_down, ln_ffn_g, ln_ffn_b):
    bsz, seq, d = x.shape
    h = x.reshape(bsz * seq, d)
    for layer in range(ln_mix_g.shape[0]):
        j = layer // 2
        rw, rb = _router_params(router_w[layer], router_b[layer])
        if layer % 2 == 0:
            h, logits = _even_layer(h, bsz, seq, ev_w_in[j], ev_w_out[j], ev_lambda_q1[j], ev_lambda_k1[j],
                                    ev_lambda_q2[j], ev_lambda_k2[j], ev_subln_w[j], layer,
                                    ln_mix_g[layer], ln_mix_b[layer], rw, rb)
        else:
            h, logits = _odd_layer(h, bsz, seq, od_w_in[j], od_kv_norm_w[j], od_w_uv[j], od_w_out[j],
                                   ln_mix_g[layer], ln_mix_b[layer], rw, rb)
        h = _moe(h, logits, exp_w_gu, exp_b_gu, exp_w_down, exp_b_down,
                 ln_ffn_g[layer].reshape(1, -1), ln_ffn_b[layer].reshape(1, -1), layer)
    return h.reshape(bsz, seq, d)
```

```python
import functools
import math

import jax
import jax.numpy as jnp
from jax import lax
from jax.experimental import pallas as pl
from jax.experimental.pallas import tpu as pltpu

F32, BF16, I32 = jnp.float32, jnp.bfloat16, jnp.int32

HEAD_DIM = 64
SB_HEADS = 8
DIFF_HEADS = 4
DIFF_SUB = 64
DIFF_VDIM = 128
DSA_HEADS = 16
DSA_LATENT = 128
DSA_VDIM = 64
IDX_HEADS = 8
IDX_DIM = 64
TOPK_MAX = 256
N_EXPERTS = 32
TOP_K = 4
SWIGLU_LIMIT = 7.0
SWIGLU_ALPHA = 1.702
LN_EPS = 1e-5
RMS_EPS = 1e-5
DEPTH = 2
DEEPNORM_ALPHA = (2 * DEPTH) ** 0.25

LANES = 128
SUBLANES = 8
BF16_ROWS = 16
VMEM_LIMIT = 56 * 1024 * 1024
INT_MIN = -(2 ** 31)
MASKED = -1e30
DEAD_LOG2 = -152.0
LOG2_E = math.log2(math.e)

TM_PROJ = 512
TQ_ATT = 256
TQ_DSA = 128
TRI = 512
HG_DSA = 16
TM_ROUTE = 512
TM_FFN = 512
TD_DISPATCH = 512
TC_COMBINE = 256


def _cparams(*sem):
    return pltpu.CompilerParams(dimension_semantics=sem, vmem_limit_bytes=VMEM_LIMIT)


def _nt_dot(a, b):
    return lax.dot_general(a, b, (((1,), (1,)), ((), ())), preferred_element_type=F32)


def _dot(a, b):
    return jnp.dot(a, b, preferred_element_type=F32)


def _layer_norm(y, g, b):
    mu = jnp.mean(y, axis=-1, keepdims=True)
    d = y - mu
    var = jnp.mean(d * d, axis=-1, keepdims=True)
    return d * lax.rsqrt(var + LN_EPS) * g + b


def _col_chunk(n):
    for c in (512, 384, 256, 128):
        if n % c == 0:
            return c
    raise ValueError(n)


def _proj_kernel(x_ref, w_ref, o_ref):
    x = x_ref[...].astype(BF16)
    n = o_ref.shape[-1]
    c = _col_chunk(n)
    for j in range(0, n, c):
        o_ref[:, j:j + c] = _dot(x, w_ref[:, j:j + c]).astype(o_ref.dtype)


def _project(x, w, out_dtype):
    t, k = x.shape
    n = w.shape[1]
    tm = min(TM_PROJ, t)
    return pl.pallas_call(
        _proj_kernel,
        grid=(t // tm,),
        in_specs=[pl.BlockSpec((tm, k), lambda i: (i, 0)),
                  pl.BlockSpec((k, n), lambda i: (0, 0))],
        out_specs=pl.BlockSpec((tm, n), lambda i: (i, 0)),
        out_shape=jax.ShapeDtypeStruct((t, n), out_dtype),
        compiler_params=_cparams("parallel"),
        name="in_proj",
    )(x, w)


def _stack_halves(x, lane):
    zero = jnp.zeros_like(x)
    return jnp.concatenate([jnp.where(lane < HEAD_DIM, x, zero), jnp.where(lane >= HEAD_DIM, x, zero)], axis=0)


def _sb_kernel(q_ref, k_ref, v_ref, o_ref, *, tq):
    i = pl.program_id(2)
    scale = HEAD_DIM ** -0.5
    lane = lax.broadcasted_iota(I32, (1, LANES), 1)
    row = lax.broadcasted_iota(I32, (2 * tq, tq), 0)
    row = jnp.where(row >= tq, row - tq, row)
    col = lax.broadcasted_iota(I32, (2 * tq, tq), 1)
    strict = col < row
    r1 = lax.broadcasted_iota(I32, (tq, tq), 0)
    c1 = lax.broadcasted_iota(I32, (tq, tq), 1)
    after = (r1 > c1).astype(BF16)
    q2 = _stack_halves(q_ref[0], lane)

    def block(j, carry, masked):
        run, acc = carry
        start = pl.multiple_of(j * tq, tq)
        kb = k_ref[0, pl.ds(start, tq), :]
        vb = v_ref[0, pl.ds(start, tq), :]
        z2 = _nt_dot(q2, kb) * (scale * LOG2_E)
        neg = -z2
        keep2 = jnp.minimum(neg, 0.0) - jnp.log2(1.0 + jnp.exp2(jnp.minimum(z2, neg)))
        lk = jnp.where(strict, keep2, 0.0) if masked else keep2
        hi = lk.astype(BF16)
        lo = (lk - hi.astype(F32)).astype(BF16)
        suffix = _dot(hi, after) + _dot(lo, after)
        w = jnp.exp2(z2 + keep2 + suffix + run)
        if masked:
            w = jnp.where(strict, w, 0.0)
        wb = w.astype(BF16)
        zero = jnp.zeros_like(vb)
        acc = (acc + _dot(wb[:tq], jnp.where(lane < HEAD_DIM, vb, zero))
               + _dot(wb[tq:], jnp.where(lane >= HEAD_DIM, vb, zero)))
        run = run + suffix[:, 0:1] + lk[:, 0:1]
        return run, acc

    run, acc = block(i, (jnp.zeros((2 * tq, 1), F32), jnp.zeros((tq, LANES), F32)), True)

    def alive(c):
        return jnp.logical_and(c[0] >= 0, c[1] > DEAD_LOG2)

    def step(c):
        j, _, run, acc = c
        run, acc = block(j, (run, acc), False)
        return j - 1, jnp.max(run), run, acc

    _, _, _, acc = lax.while_loop(alive, step, (i - 1, jnp.max(run), run, acc))
    o_ref[0] = acc.astype(o_ref.dtype)


def _sb_attention(proj, bsz, seq):
    tq = min(TQ_ATT, seq)
    pairs = SB_HEADS * HEAD_DIM // LANES
    return pl.pallas_call(
        functools.partial(_sb_kernel, tq=tq),
        grid=(bsz, pairs, seq // tq),
        in_specs=[pl.BlockSpec((1, tq, LANES), lambda b, p, i: (b, i, p)),
                  pl.BlockSpec((1, seq, LANES), lambda b, p, i: (b, 0, pairs + p)),
                  pl.BlockSpec((1, seq, LANES), lambda b, p, i: (b, 0, 2 * pairs + p))],
        out_specs=pl.BlockSpec((1, tq, LANES), lambda b, p, i: (b, i, p)),
        out_shape=jax.ShapeDtypeStruct((bsz, seq, SB_HEADS * HEAD_DIM), BF16),
        compiler_params=_cparams("parallel", "parallel", "parallel"),
        name="stickbreak_attn",
    )(proj, proj, proj)


def _diff_kernel(slope_ref, lq1_ref, lk1_ref, lq2_ref, lk2_ref, subw_ref, q_ref, k_ref, v_ref, o_ref,
                 *, tq, lam_init):
    h = pl.program_id(1)
    i = pl.program_id(2)
    scale = DIFF_SUB ** -0.5
    slope = slope_ref[h]
    lam = (jnp.exp(jnp.sum(lq1_ref[...] * lk1_ref[...], keepdims=True))
           - jnp.exp(jnp.sum(lq2_ref[...] * lk2_ref[...], keepdims=True)) + lam_init)
    lane = lax.broadcasted_iota(I32, (1, LANES), 1)
    row = lax.broadcasted_iota(I32, (2 * tq, tq), 0)
    row = jnp.where(row >= tq, row - tq, row)
    col = lax.broadcasted_iota(I32, (2 * tq, tq), 1)
    causal = col <= row
    key_off = lax.broadcasted_iota(I32, (1, tq), 1)
    q2 = _stack_halves(q_ref[0], lane)

    def block(j, carry, masked):
        m, l, acc = carry
        start = j * tq
        kb = k_ref[0, pl.ds(start, tq), :]
        vb = v_ref[0, pl.ds(start, tq), :]
        s = _nt_dot(q2, kb) * (scale * LOG2_E) + (slope * LOG2_E) * (start + key_off).astype(F32)
        if masked:
            s = jnp.where(causal, s, -jnp.inf)
        m_new = jnp.maximum(m, jnp.max(s, axis=-1, keepdims=True))
        a = jnp.exp2(m - m_new)
        p = jnp.exp2(s - m_new)
        l = a * l + jnp.sum(p, axis=-1, keepdims=True)
        acc = a * acc + _dot(p.astype(BF16), vb)
        return m_new, l, acc

    for n_left in range(k_ref.shape[1] // tq):
        @pl.when(i == n_left)
        def _(n_left=n_left):
            carry = (jnp.full((2 * tq, 1), -jnp.inf, F32), jnp.zeros((2 * tq, 1), F32),
                     jnp.zeros((2 * tq, LANES), F32))
            carry = block(n_left, carry, True)
            for j in range(n_left):
                carry = block(j, carry, False)
            out = carry[2] / carry[1]
            o = out[:tq] - lam * out[tq:]
            o = o * lax.rsqrt(jnp.mean(o * o, axis=-1, keepdims=True) + RMS_EPS) * subw_ref[...]
            o_ref[0] = (o * (1.0 - lam_init)).astype(o_ref.dtype)


def _diff_attention(proj, slopes, lq1, lk1, lq2, lk2, subw, bsz, seq, lam_init):
    tq = min(TQ_ATT, seq)
    q0 = 3 * SB_HEADS * HEAD_DIM // LANES
    k0 = q0 + DIFF_HEADS
    v0 = k0 + DIFF_HEADS
    vec = lambda n: pl.BlockSpec((1, n), lambda b, h, i: (0, 0))
    return pl.pallas_call(
        functools.partial(_diff_kernel, tq=tq, lam_init=lam_init),
        grid=(bsz, DIFF_HEADS, seq // tq),
        in_specs=[pl.BlockSpec(memory_space=pltpu.SMEM),
                  vec(DIFF_SUB), vec(DIFF_SUB), vec(DIFF_SUB), vec(DIFF_SUB), vec(DIFF_VDIM),
                  pl.BlockSpec((1, tq, LANES), lambda b, h, i: (b, i, q0 + h)),
                  pl.BlockSpec((1, seq, LANES), lambda b, h, i: (b, 0, k0 + h)),
                  pl.BlockSpec((1, seq, LANES), lambda b, h, i: (b, 0, v0 + h))],
        out_specs=pl.BlockSpec((1, tq, LANES), lambda b, h, i: (b, i, h)),
        out_shape=jax.ShapeDtypeStruct((bsz, seq, DIFF_HEADS * DIFF_VDIM), BF16),
        compiler_params=_cparams("parallel", "parallel", "parallel"),
        name="diff_attn",
    )(slopes, lq1, lk1, lq2, lk2, subw, proj, proj, proj)


def _tree_sum_rows(x):
    n = x.shape[0] // 8
    x = x.reshape(n, 8, x.shape[1])
    while n > 1:
        n //= 2
        x = x[:n] + x[n:]
    return x[0]


def _tree_max_rows(x):
    n = x.shape[0] // 8
    x = x.reshape(n, 8, x.shape[1])
    while n > 1:
        n //= 2
        x = jnp.maximum(x[:n], x[n:])
    return x[0]


def _dsa_kernel(q_ref, ckv_ref, kvw_ref, qi_ref, ki_ref, wi_ref, qx_ref, kx_ref, wup_ref, o_ref,
                cn_ref, cnt_ref, skey_ref, bias_ref, s_ref, *, tq, tk, k_sel, hg):
    i = pl.program_id(1)
    t0 = i * tq

    @pl.when(i == 0)
    def _():
        c = ckv_ref[0]
        cn = c * lax.rsqrt(jnp.mean(c * c, axis=-1, keepdims=True) + RMS_EPS) * kvw_ref[...]
        cn_ref[:, :DSA_LATENT] = cn.astype(BF16)
        cn_ref[:, DSA_LATENT:] = kx_ref[...]
        cnt_ref[:DSA_LATENT, :] = cn.T.astype(BF16)
        row = lax.broadcasted_iota(I32, (BF16_ROWS, cnt_ref.shape[1]), 0)
        cnt_ref[DSA_LATENT:, :] = jnp.where(row == 0, 1.0, 0.0).astype(BF16)

    for k in range(ckv_ref.shape[1] // tk):
        @pl.when((t0 + tq - 1) // tk == k)
        def _(k=k):
            _dsa_block(q_ref, qi_ref, ki_ref, wi_ref, qx_ref, wup_ref, o_ref, cn_ref, cnt_ref, skey_ref, bias_ref,
                       s_ref, t0=t0, n_tiles=k + 1, tq=tq, tk=tk, k_sel=k_sel, hg=hg)


def _static_loop(n, body, carry):
    for j in range(n):
        carry = body(j, carry)
    return carry


def _dsa_block(q_ref, qi_ref, ki_ref, wi_ref, qx_ref, wup_ref, o_ref, cn_ref, cnt_ref, skey_ref, bias_ref, s_ref,
               *, t0, n_tiles, tq, tk, k_sel, hg):
    def tile_slice(j):
        return pl.ds(j * tk, tk)

    lane = lax.broadcasted_iota(I32, (1, LANES), 1)
    q_pos = t0 + lax.broadcasted_iota(I32, (tk, tq), 1)
    key_off = lax.broadcasted_iota(I32, (tk, tq), 0)

    w_t = (wi_ref[0] * ((IDX_HEADS ** -0.5) * (IDX_DIM ** -0.5))).T
    stacked = []
    for h in range(IDX_HEADS):
        qp = qi_ref[0, :, (h // 2) * LANES:(h // 2 + 1) * LANES]
        stacked.append(jnp.where((lane // IDX_DIM) == (h % 2), qp, jnp.zeros_like(qp)))
    q8 = jnp.concatenate(stacked, axis=0)

    def score_tile(j, _):
        d = _nt_dot(ki_ref[0, tile_slice(j), :], q8)
        score = jnp.zeros((tk, tq), F32)
        for h in range(IDX_HEADS):
            score = score + jnp.maximum(d[:, h * tq:(h + 1) * tq], 0.0) * w_t[h:h + 1, :]
        score = jnp.where(score == 0.0, 0.0, score)
        bits = pltpu.bitcast(score, I32)
        skey = jnp.where(bits < 0, bits ^ 0x7FFFFFFF, bits)
        skey_ref[tile_slice(j), :] = jnp.where(j * tk + key_off <= q_pos, skey, INT_MIN)
        return 0

    _static_loop(n_tiles, score_tile, 0)

    def count(pred):
        def tile(j, cnt):
            return cnt + _tree_sum_rows(pred(skey_ref[tile_slice(j), :]).astype(F32))
        cnt = _static_loop(n_tiles, tile, jnp.zeros((8, tq), F32))
        return jnp.sum(cnt, axis=0, keepdims=True)

    kf = float(k_sel)
    thr = jnp.where(count(lambda sk: sk >= 0) >= kf, 0, INT_MIN).astype(I32)

    def bit_step(b, thr):
        cand = thr + jnp.left_shift(jnp.int32(1), 30 - b)
        return jnp.where(count(lambda sk: sk >= cand) >= kf, cand, thr)

    thr = lax.fori_loop(0, 31, bit_step, thr)
    need = kf - count(lambda sk: sk > thr)

    r2 = lax.broadcasted_iota(I32, (tk, tk), 0)
    c2 = lax.broadcasted_iota(I32, (tk, tk), 1)
    upto = (c2 <= r2).astype(BF16)

    def tie_tile(j, seen):
        sk = skey_ref[tile_slice(j), :]
        tied = sk == thr
        prefix = _dot(upto, tied.astype(F32).astype(BF16)) + seen
        selected = ((sk > thr) | (tied & (prefix <= need))) & (j * tk + key_off <= q_pos)
        bias_ref[tile_slice(j), :] = jnp.where(selected, 0.0, MASKED)
        return prefix[tk - 1:tk, :]

    _static_loop(n_tiles, tie_tile, jnp.zeros((1, tq), F32))

    log2_scale = (DSA_LATENT ** -0.5) * LOG2_E

    def head_group(g, _):
        heads = [g * hg + u for u in range(hg)]
        offs = [pl.multiple_of(h * DSA_LATENT, DSA_LATENT) for h in heads]
        qg = jnp.concatenate(
            [jnp.concatenate([q_ref[0, :, pl.ds(off, DSA_LATENT)],
                              jnp.broadcast_to(qx_ref[pl.ds(h, 1), :], (tq, LANES)).astype(BF16)], axis=1)
             for h, off in zip(heads, offs)], axis=0)

        def logits_tile(j, m):
            d = _nt_dot(cn_ref[tile_slice(j), :], qg)
            b = bias_ref[tile_slice(j), :]
            tops = []
            for u in range(hg):
                s = d[:, u * tq:(u + 1) * tq] * log2_scale + b
                s_ref[tile_slice(j), u * tq:(u + 1) * tq] = s
                tops.append(jnp.max(_tree_max_rows(s), axis=0, keepdims=True))
            return jnp.maximum(m, jnp.concatenate(tops, axis=1))

        m = _static_loop(n_tiles, logits_tile, jnp.full((1, hg * tq), MASKED, F32))

        def value_tile(j, acc):
            p = jnp.exp2((s_ref[tile_slice(j), :] - m).astype(BF16))
            return acc + _dot(cnt_ref[:, tile_slice(j)], p)

        acc = _static_loop(n_tiles, value_tile, jnp.zeros((DSA_LATENT + BF16_ROWS, hg * tq), F32))
        out = (acc[:DSA_LATENT] / acc[DSA_LATENT:DSA_LATENT + 1]).astype(BF16)
        for u in range(0, hg, 2):
            pair = jnp.concatenate([out[:, u * tq:(u + 1) * tq], out[:, (u + 1) * tq:(u + 2) * tq]], axis=0)
            up = _dot(wup_ref[(g * hg + u) // 2], pair)
            off = pl.multiple_of((g * hg + u) * DSA_VDIM, 2 * DSA_VDIM)
            o_ref[0, :, pl.ds(off, 2 * DSA_VDIM)] = up.T.astype(o_ref.dtype)
        return 0

    lax.fori_loop(0, DSA_HEADS // hg, head_group, 0)


def _bf16_parts(x, n):
    parts = []
    for _ in range(n):
        p = x.astype(BF16).astype(F32)
        parts.append(p)
        x = x - p
    return parts


def _dsa_attention(proj, small, kvw, w_uv, bsz, seq):
    tq = min(TQ_DSA, seq)
    tk = min(TRI, seq)
    k_sel = min(TOPK_MAX, seq // 4)
    qw = DSA_HEADS * DSA_LATENT
    qiw = IDX_HEADS * IDX_DIM
    slope = _alibi_slopes(DSA_HEADS) / (DSA_LATENT ** -0.5)
    s_parts = _bf16_parts(slope, 3)
    pos = jnp.arange(seq, dtype=I32)
    p_parts = [(pos // LANES * LANES).astype(F32), (pos % LANES).astype(F32)]
    qx = jnp.stack([sp for _ in p_parts for sp in s_parts], axis=1)
    kx = jnp.stack([pp for pp in p_parts for _ in s_parts], axis=1)
    qx = jnp.pad(qx, ((0, 0), (0, LANES - qx.shape[1])))
    kx = jnp.pad(kx, ((0, 0), (0, LANES - kx.shape[1]))).astype(BF16)
    wt = jnp.swapaxes(w_uv, 1, 2).reshape(DSA_HEADS // 2, 2, DSA_VDIM, DSA_LATENT)
    zero = jnp.zeros_like(wt[:, 0])
    wup = jnp.concatenate([jnp.concatenate([wt[:, 0], zero], axis=2),
                           jnp.concatenate([zero, wt[:, 1]], axis=2)], axis=1).astype(BF16)
    return pl.pallas_call(
        functools.partial(_dsa_kernel, tq=tq, tk=tk, k_sel=k_sel, hg=HG_DSA),
        grid=(bsz, seq // tq),
        in_specs=[pl.BlockSpec((1, tq, qw), lambda b, i: (b, i, 0)),
                  pl.BlockSpec((1, seq, DSA_LATENT), lambda b, i: (b, 0, 0)),
                  pl.BlockSpec((1, DSA_LATENT), lambda b, i: (0, 0)),
                  pl.BlockSpec((1, tq, qiw), lambda b, i: (b, i, qw // qiw)),
                  pl.BlockSpec((1, seq, LANES), lambda b, i: (b, 0, (qw + qiw) // LANES)),
                  pl.BlockSpec((1, tq, LANES), lambda b, i: (b, i, 1)),
                  pl.BlockSpec((DSA_HEADS, LANES), lambda b, i: (0, 0)),
                  pl.BlockSpec((seq, LANES), lambda b, i: (0, 0)),
                  pl.BlockSpec(wup.shape, lambda b, i: (0, 0, 0))],
        out_specs=pl.BlockSpec((1, tq, DSA_HEADS * DSA_VDIM), lambda b, i: (b, i, 0)),
        out_shape=jax.ShapeDtypeStruct((bsz, seq, DSA_HEADS * DSA_VDIM), BF16),
        scratch_shapes=[pltpu.VMEM((seq, DSA_LATENT + LANES), BF16),
                        pltpu.VMEM((DSA_LATENT + BF16_ROWS, seq), BF16),
                        pltpu.VMEM((seq, tq), I32), pltpu.VMEM((seq, tq), F32),
                        pltpu.VMEM((seq, HG_DSA * tq), F32)],
        compiler_params=_cparams("parallel", "arbitrary"),
        name="dsa_attn",
    )(proj, small, kvw, proj, proj, small, qx, kx, wup)


def _mix_tail(mix, h_ref, g_ref, b_ref, rw_ref, rb_ref, hout_ref, logit_ref):
    hn = _layer_norm(DEEPNORM_ALPHA * h_ref[...] + mix, g_ref[...], b_ref[...])
    hout_ref[...] = hn
    logit_ref[...] = _dot(hn.astype(BF16), rw_ref[...]) + rb_ref[...]


def _mix_even_kernel(oa_ref, ob_ref, wa_ref, wb_ref, h_ref, g_ref, b_ref, rw_ref, rb_ref, hout_ref, logit_ref):
    mix = _dot(oa_ref[...], wa_ref[...]) + _dot(ob_ref[...], wb_ref[...])
    _mix_tail(mix, h_ref, g_ref, b_ref, rw_ref, rb_ref, hout_ref, logit_ref)


def _mix_odd_kernel(o_ref, wo_ref, h_ref, g_ref, b_ref, rw_ref, rb_ref, hout_ref, logit_ref):
    _mix_tail(_dot(o_ref[...], wo_ref[...]), h_ref, g_ref, b_ref, rw_ref, rb_ref, hout_ref, logit_ref)


def _mix_call(kernel, name, acts, weights, h, g, b, rw, rb):
    t, d = h.shape
    tm = min(TM_PROJ, t)
    row = lambda n: pl.BlockSpec((tm, n), lambda i: (i, 0))
    full = lambda a: pl.BlockSpec(a.shape, lambda i: (0, 0))
    return pl.pallas_call(
        kernel,
        grid=(t // tm,),
        in_specs=[row(a.shape[1]) for a in acts] + [full(w) for w in weights]
                 + [row(d), full(g), full(b), full(rw), full(rb)],
        out_specs=[row(d), row(LANES)],
        out_shape=[jax.ShapeDtypeStruct((t, d), F32), jax.ShapeDtypeStruct((t, LANES), F32)],
        compiler_params=_cparams("parallel"),
        name=name,
    )(*acts, *weights, h, g, b, rw, rb)


def _route_kernel(logit_ref, idx_ref, gate_ref, rank_ref, cnt_ref, *, tm):
    @pl.when(pl.program_id(0) == 0)
    def _():
        cnt_ref[...] = jnp.zeros_like(cnt_ref)

    lane = lax.broadcasted_iota(I32, (tm, LANES), 1)
    lane_f = lane.astype(F32)
    x = jnp.where(lane < N_EXPERTS, logit_ref[...], -jnp.inf)
    vals, hots = [], []
    for k in range(TOP_K):
        m = jnp.max(x, axis=-1, keepdims=True)
        first = jnp.min(jnp.where(x == m, lane_f, float(LANES)), axis=-1, keepdims=True)
        hot = lane_f == first
        x = jnp.where(hot, -jnp.inf, x)
        vals.append(m)
        hots.append(hot)
        idx_ref[:, k:k + 1] = first.astype(I32)
    exps = [jnp.exp(v - vals[0]) for v in vals]
    denom = exps[0] + exps[1] + exps[2] + exps[3]
    for k in range(TOP_K):
        gate_ref[:, k:k + 1] = exps[k] / denom

    hot_sum = (hots[0] | hots[1] | hots[2] | hots[3]).astype(F32)
    r = lax.broadcasted_iota(I32, (tm, tm), 0)
    c = lax.broadcasted_iota(I32, (tm, tm), 1)
    earlier = (c < r).astype(BF16)
    before = _dot(earlier, hot_sum.astype(BF16)) + cnt_ref[...]
    for k in range(TOP_K):
        rank_ref[:, k:k + 1] = jnp.sum(jnp.where(hots[k], before, 0.0), axis=-1, keepdims=True).astype(I32)
    cnt_ref[...] += jnp.sum(hot_sum, axis=0, keepdims=True)


def _route(logits):
    t = logits.shape[0]
    tm = min(TM_ROUTE, t)
    narrow = pl.BlockSpec((tm, TOP_K), lambda i: (i, 0))
    return pl.pallas_call(
        functools.partial(_route_kernel, tm=tm),
        grid=(t // tm,),
        in_specs=[pl.BlockSpec((tm, LANES), lambda i: (i, 0))],
        out_specs=[narrow, narrow, narrow, pl.BlockSpec((1, LANES), lambda i: (0, 0))],
        out_shape=[jax.ShapeDtypeStruct((t, TOP_K), I32), jax.ShapeDtypeStruct((t, TOP_K), F32),
                   jax.ShapeDtypeStruct((t, TOP_K), I32), jax.ShapeDtypeStruct((1, LANES), F32)],
        compiler_params=_cparams("arbitrary"),
        name="route",
    )(logits)


def _row_copy(src_ref, src_row, dst_ref, dst_row, sem):
    return pltpu.make_async_copy(src_ref.at[pl.ds(src_row, 1), :], dst_ref.at[pl.ds(dst_row, 1), :], sem)


def _dispatch_kernel(pad_start_ref, pad_count_ref, n_used_ref, pos_ref, x_ref, xs_out, zero_ref, sem, pad_sem,
                     *, td):
    @pl.when(pl.program_id(0) == 0)
    def _():
        zero_ref[...] = jnp.zeros_like(zero_ref)

        def fill_for(start_not_wait):
            def fill(present, n_rows, off):
                copy = pltpu.make_async_copy(zero_ref.at[pl.ds(0, n_rows), :], xs_out.at[pl.ds(off, n_rows), :],
                                             pad_sem)

                @pl.when(jnp.logical_and(present, start_not_wait))
                def _():
                    copy.start()

                @pl.when(jnp.logical_and(present, jnp.logical_not(start_not_wait)))
                def _():
                    copy.wait()
            return fill

        def pads(e, start_not_wait):
            fill = fill_for(start_not_wait)
            off = pad_start_ref[e]
            count = pad_count_ref[e]
            head = jnp.minimum(count, (-off) & (SUBLANES - 1))
            for r in range(SUBLANES - 1):
                fill(r < head, 1, off + r)
            off = off + head
            count = count - head
            chunk = zero_ref.shape[0] // 2
            while chunk >= SUBLANES:
                present = (count & chunk) != 0
                fill(present, chunk, pl.multiple_of(off, SUBLANES))
                off = off + jnp.where(present, chunk, 0)
                chunk //= 2
            for r in range(SUBLANES - 1):
                fill(r < (count & (SUBLANES - 1)), 1, off + r)
            return start_not_wait

        def tail(j, start_not_wait):
            tile = zero_ref.shape[0]
            fill_for(start_not_wait)(j >= n_used_ref[0], tile, pl.multiple_of(j * tile, tile))
            return start_not_wait

        n_tiles = xs_out.shape[0] // zero_ref.shape[0]
        lax.fori_loop(0, N_EXPERTS, pads, True)
        lax.fori_loop(0, n_tiles, tail, True)
        lax.fori_loop(0, N_EXPERTS, pads, False)
        lax.fori_loop(0, n_tiles, tail, False)

    def issue(i, _):
        for k in range(TOP_K):
            _row_copy(x_ref, i, xs_out, pos_ref[i * TOP_K + k], sem).start()
        return 0

    lax.fori_loop(0, td, issue, 0)

    def drain(i, _):
        for k in range(TOP_K):
            _row_copy(x_ref, i, xs_out, pos_ref[i * TOP_K + k], sem).wait()
        return 0

    lax.fori_loop(0, td, drain, 0)


def _dispatch(pad_start, pad_count, n_used, pos_flat, x, n_rows):
    t, d = x.shape
    td = min(TD_DISPATCH, t)
    grid_spec = pltpu.PrefetchScalarGridSpec(
        num_scalar_prefetch=3,
        grid=(t // td,),
        in_specs=[pl.BlockSpec((td * TOP_K,), lambda i, ps, pc, nu: (i,), memory_space=pltpu.SMEM),
                  pl.BlockSpec((td, d), lambda i, ps, pc, nu: (i, 0))],
        out_specs=pl.BlockSpec(memory_space=pl.ANY),
        scratch_shapes=[pltpu.VMEM((TM_FFN, d), x.dtype), pltpu.SemaphoreType.DMA, pltpu.SemaphoreType.DMA],
    )
    return pl.pallas_call(
        functools.partial(_dispatch_kernel, td=td),
        grid_spec=grid_spec,
        out_shape=jax.ShapeDtypeStruct((n_rows, d), x.dtype),
        compiler_params=_cparams("arbitrary"),
        name="moe_dispatch",
    )(pad_start, pad_count, n_used, pos_flat, x)


def _ffn_kernel(tile_expert_ref, n_used_ref, xs_ref, wgu_ref, bgu_ref, wd_ref, bd_ref, y_ref, wgu_bf, wd_bf):
    i = pl.program_id(0)
    d_ff = wd_ref.shape[2]
    new_expert = jnp.logical_or(i == 0, tile_expert_ref[i] != tile_expert_ref[jnp.maximum(i - 1, 0)])

    @pl.when(jnp.logical_and(new_expert, i < n_used_ref[0]))
    def _():
        wgu_bf[...] = wgu_ref[0, 0].astype(BF16)
        wd_bf[...] = wd_ref[0, 0].astype(BF16)

    @pl.when(i < n_used_ref[0])
    def _():
        x = xs_ref[...].astype(BF16)
        hgu = _dot(x, wgu_bf[...]) + bgu_ref[0, 0]
        gate = jnp.minimum(hgu[:, :d_ff], SWIGLU_LIMIT)
        up = jnp.clip(hgu[:, d_ff:], -SWIGLU_LIMIT, SWIGLU_LIMIT)
        act = gate * (1.0 / (1.0 + jnp.exp(-SWIGLU_ALPHA * gate))) * (up + 1.0)
        y_ref[...] = _dot(act.astype(BF16), wd_bf[...]) + bd_ref[0, 0]

    @pl.when(i >= n_used_ref[0])
    def _():
        y_ref[...] = jnp.zeros_like(y_ref)


def _expert_ffn(tile_expert, n_used, xs, w_gu, b_gu, w_down, b_down, layer):
    n_rows, d = xs.shape
    depth, e, _, f2 = w_gu.shape
    d_ff = w_down.shape[2]
    tm = TM_FFN
    grid_spec = pltpu.PrefetchScalarGridSpec(
        num_scalar_prefetch=2,
        grid=(n_rows // tm,),
        in_specs=[pl.BlockSpec((tm, d), lambda i, te, nu: (jnp.minimum(i, nu[0] - 1), 0)),
                  pl.BlockSpec((1, 1, d, f2), lambda i, te, nu: (layer, te[i], 0, 0)),
                  pl.BlockSpec((1, 1, 1, f2), lambda i, te, nu: (layer, te[i], 0, 0)),
                  pl.BlockSpec((1, 1, d_ff, d), lambda i, te, nu: (layer, te[i], 0, 0)),
                  pl.BlockSpec((1, 1, 1, d), lambda i, te, nu: (layer, te[i], 0, 0))],
        out_specs=pl.BlockSpec((tm, d), lambda i, te, nu: (i, 0)),
        scratch_shapes=[pltpu.VMEM((d, f2), BF16), pltpu.VMEM((d_ff, d), BF16)],
    )
    return pl.pallas_call(
        _ffn_kernel,
        grid_spec=grid_spec,
        out_shape=jax.ShapeDtypeStruct((n_rows, d), F32),
        compiler_params=_cparams("arbitrary"),
        name="expert_ffn",
    )(tile_expert, n_used, xs, w_gu, b_gu.reshape(depth, e, 1, f2), w_down, b_down.reshape(depth, e, 1, d))


def _combine_kernel(pos_ref, pos_next_ref, gate_ref, h_ref, g_ref, b_ref, y_hbm, out_ref, buf, sems, *, tc):
    i = pl.program_id(0)
    slot = i % 2

    def gather(rows_ref, into):
        def issue(t, _):
            for k in range(TOP_K):
                _row_copy(y_hbm, rows_ref[t * TOP_K + k], buf.at[into, k], t, sems.at[into]).start()
            return 0
        lax.fori_loop(0, tc, issue, 0)

    @pl.when(i == 0)
    def _():
        gather(pos_ref, 0)

    @pl.when(i + 1 < pl.num_programs(0))
    def _():
        gather(pos_next_ref, 1 - slot)

    def drain(t, _):
        for k in range(TOP_K):
            _row_copy(y_hbm, pos_ref[t * TOP_K + k], buf.at[slot, k], t, sems.at[slot]).wait()
        return 0

    lax.fori_loop(0, tc, drain, 0)

    gates = gate_ref[...]
    ffn = gates[:, 0:1] * buf[slot, 0]
    for k in range(1, TOP_K):
        ffn = ffn + gates[:, k:k + 1] * buf[slot, k]
    out_ref[...] = _layer_norm(DEEPNORM_ALPHA * h_ref[...] + ffn, g_ref[...], b_ref[...])


def _combine(pos_flat, gates, h, g, b, y):
    t, d = h.shape
    tc = min(TC_COMBINE, t)
    last = t // tc - 1
    return pl.pallas_call(
        functools.partial(_combine_kernel, tc=tc),
        grid=(t // tc,),
        in_specs=[pl.BlockSpec((tc * TOP_K,), lambda i: (i,), memory_space=pltpu.SMEM),
                  pl.BlockSpec((tc * TOP_K,), lambda i: (jnp.minimum(i + 1, last),), memory_space=pltpu.SMEM),
                  pl.BlockSpec((tc, TOP_K), lambda i: (i, 0)),
                  pl.BlockSpec((tc, d), lambda i: (i, 0)),
                  pl.BlockSpec((1, d), lambda i: (0, 0)),
                  pl.BlockSpec((1, d), lambda i: (0, 0)),
                  pl.BlockSpec(memory_space=pl.ANY)],
        out_specs=pl.BlockSpec((tc, d), lambda i: (i, 0)),
        out_shape=jax.ShapeDtypeStruct((t, d), F32),
        scratch_shapes=[pltpu.VMEM((2, TOP_K, tc, d), F32), pltpu.SemaphoreType.DMA((2,))],
        compiler_params=_cparams("arbitrary"),
        name="moe_combine",
    )(pos_flat, pos_flat, gates, h, g, b, y)


def _moe(h, logits, w_gu, b_gu, w_down, b_down, g, b, layer):
    t = h.shape[0]
    idx, gates, rank, counts = _route(logits)
    counts = counts[0, :N_EXPERTS].astype(I32)
    tiles = (counts + TM_FFN - 1) // TM_FFN
    tile_end = jnp.cumsum(tiles)
    start = (tile_end - tiles) * TM_FFN
    n_tiles = (t * TOP_K) // TM_FFN + N_EXPERTS
    tile_ids = jnp.arange(n_tiles, dtype=I32)
    tile_expert = jnp.minimum(jnp.sum((tile_end[None, :] <= tile_ids[:, None]).astype(I32), axis=1), N_EXPERTS - 1)
    pos = (start[idx] + rank).reshape(-1).astype(I32)
    n_used = tile_end[-1:].astype(I32)
    xs = _dispatch(start + counts, tiles * TM_FFN - counts, n_used, pos, h, n_tiles * TM_FFN)
    y = _expert_ffn(tile_expert, n_used, xs, w_gu, b_gu, w_down, b_down, layer)
    return _combine(pos, gates, h, g, b, y)


def _router_params(router_w, router_b):
    rw = jnp.pad(router_w, ((0, 0), (0, LANES - N_EXPERTS))).astype(BF16)
    rb = jnp.pad(router_b, (0, LANES - N_EXPERTS)).reshape(1, LANES)
    return rw, rb


def _alibi_slopes(n):
    return 2.0 ** (-8.0 * jnp.arange(1, n + 1, dtype=F32) / n)


def _even_layer(h, bsz, seq, w_in, w_out, lq1, lk1, lq2, lk2, subw, layer, g, b, rw, rb):
    proj = _project(h, w_in.astype(BF16), BF16).reshape(bsz, seq, -1)
    o_a = _sb_attention(proj, bsz, seq)
    lam_init = 0.8 - 0.6 * math.exp(-0.3 * layer)
    row = lambda v: v.reshape(1, -1)
    o_b = _diff_attention(proj, _alibi_slopes(DIFF_HEADS), row(lq1), row(lk1), row(lq2), row(lk2), row(subw),
                          bsz, seq, lam_init)
    t = bsz * seq
    sbw = SB_HEADS * HEAD_DIM
    wo = w_out.astype(BF16)
    return _mix_call(_mix_even_kernel, "mix_even", [o_a.reshape(t, -1), o_b.reshape(t, -1)],
                     [wo[:sbw], wo[sbw:]], h, row(g), row(b), rw, rb)


def _odd_layer(h, bsz, seq, w_in, kvw, w_uv, w_out, g, b, rw, rb):
    qw = DSA_HEADS * DSA_LATENT
    c0, c1 = qw, qw + DSA_LATENT
    i0, i1 = c1, c1 + IDX_HEADS * IDX_DIM
    k1 = i1 + IDX_DIM
    w_main = jnp.concatenate([w_in[:, :qw], w_in[:, i0:i1], w_in[:, i1:k1], w_in[:, i1:k1]], axis=1).astype(BF16)
    w_small = jnp.pad(jnp.concatenate([w_in[:, c0:c1], w_in[:, k1:]], axis=1),
                      ((0, 0), (0, LANES - IDX_HEADS))).astype(BF16)
    proj = _project(h, w_main, BF16).reshape(bsz, seq, -1)
    small = _project(h, w_small, F32).reshape(bsz, seq, -1)
    o = _dsa_attention(proj, small, kvw.reshape(1, -1), w_uv, bsz, seq)
    row = lambda v: v.reshape(1, -1)
    return _mix_call(_mix_odd_kernel, "mix_odd", [o.reshape(bsz * seq, -1)], [w_out.astype(BF16)],
                     h, row(g), row(b), rw, rb)


def kernel(x, ev_w_in, ev_w_out, ev_lambda_q1, ev_lambda_k1, ev_lambda_q2, ev_lambda_k2, ev_subln_w, od_w_in, od_kv_norm_w, od_w_uv, od_w_out, ln_mix_g, ln_mix_b, router_w, router_b, exp_w_gu, exp_b_gu, exp_w_down, exp_b_down, ln_ffn_g, ln_ffn_b):
    bsz, seq, d = x.shape
    h = x.reshape(bsz * seq, d)
    for layer in range(ln_mix_g.shape[0]):
        j = layer // 2
        rw, rb = _router_params(router_w[layer], router_b[layer])
        if layer % 2 == 0:
            h, logits = _even_layer(h, bsz, seq, ev_w_in[j], ev_w_out[j], ev_lambda_q1[j], ev_lambda_k1[j],
                                    ev_lambda_q2[j], ev_lambda_k2[j], ev_subln_w[j], layer,
                                    ln_mix_g[layer], ln_mix_b[layer], rw, rb)
        else:
            h, logits = _odd_layer(h, bsz, seq, od_w_in[j], od_kv_norm_w[j], od_w_uv[j], od_w_out[j],
                                   ln_mix_g[layer], ln_mix_b[layer], rw, rb)
        h = _moe(h, logits, exp_w_gu, exp_b_gu, exp_w_down, exp_b_down,
                 ln_ffn_g[layer].reshape(1, -1), ln_ffn_b[layer].reshape(1, -1), layer)
    return h.reshape(bsz, seq, d)
```

```python
import functools
import math

import jax
import jax.numpy as jnp
from jax import lax
from jax.experimental import pallas as pl
from jax.experimental.pallas import tpu as pltpu

F32, BF16, I32 = jnp.float32, jnp.bfloat16, jnp.int32

HEAD_DIM = 64
SB_HEADS = 8
DIFF_HEADS = 4
DIFF_SUB = 64
DIFF_VDIM = 128
DSA_HEADS = 16
DSA_LATENT = 128
DSA_VDIM = 64
IDX_HEADS = 8
IDX_DIM = 64
TOPK_MAX = 256
N_EXPERTS = 32
TOP_K = 4
SWIGLU_LIMIT = 7.0
SWIGLU_ALPHA = 1.702
LN_EPS = 1e-5
RMS_EPS = 1e-5
DEPTH = 2
DEEPNORM_ALPHA = (2 * DEPTH) ** 0.25

LANES = 128
SUBLANES = 8
BF16_ROWS = 16
VMEM_LIMIT = 56 * 1024 * 1024
INT_MIN = -(2 ** 31)
MASKED = -1e30
DEAD_LOG2 = -152.0
LOG2_E = math.log2(math.e)

TM_PROJ = 512
TQ_ATT = 256
TQ_DSA = 128
TRI = 256
HG_DSA = 16
TM_ROUTE = 512
TM_FFN = 512
TD_DISPATCH = 512
TC_COMBINE = 512


def _cparams(*sem):
    return pltpu.CompilerParams(dimension_semantics=sem, vmem_limit_bytes=VMEM_LIMIT)


def _nt_dot(a, b):
    return lax.dot_general(a, b, (((1,), (1,)), ((), ())), preferred_element_type=F32)


def _dot(a, b):
    return jnp.dot(a, b, preferred_element_type=F32)


def _layer_norm(y, g, b):
    mu = jnp.mean(y, axis=-1, keepdims=True)
    d = y - mu
    var = jnp.mean(d * d, axis=-1, keepdims=True)
    return d * lax.rsqrt(var + LN_EPS) * g + b


def _col_chunk(n):
    for c in (512, 384, 256, 128):
        if n % c == 0:
            return c
    raise ValueError(n)


def _proj_kernel(x_ref, *refs):
    n_out = len(refs) // 2
    x = x_ref[...].astype(BF16)
    for w_ref, o_ref in zip(refs[:n_out], refs[n_out:]):
        n = o_ref.shape[-1]
        c = _col_chunk(n)
        for j in range(0, n, c):
            o_ref[:, j:j + c] = _dot(x, w_ref[:, j:j + c]).astype(o_ref.dtype)


def _project(x, weights, out_dtypes):
    t, k = x.shape
    tm = min(TM_PROJ, t)
    return pl.pallas_call(
        _proj_kernel,
        grid=(t // tm,),
        in_specs=[pl.BlockSpec((tm, k), lambda i: (i, 0))]
                 + [pl.BlockSpec(w.shape, lambda i: (0, 0)) for w in weights],
        out_specs=[pl.BlockSpec((tm, w.shape[1]), lambda i: (i, 0)) for w in weights],
        out_shape=[jax.ShapeDtypeStruct((t, w.shape[1]), dt) for w, dt in zip(weights, out_dtypes)],
        compiler_params=_cparams("parallel"),
        name="in_proj",
    )(x, *weights)


def _stack_halves(x, lane):
    zero = jnp.zeros_like(x)
    return jnp.concatenate([jnp.where(lane < HEAD_DIM, x, zero), jnp.where(lane >= HEAD_DIM, x, zero)], axis=0)


def _sb_kernel(q_ref, k_ref, v_ref, o_ref, *, tq):
    i = pl.program_id(2)
    scale = HEAD_DIM ** -0.5
    lane = lax.broadcasted_iota(I32, (1, LANES), 1)
    row = lax.broadcasted_iota(I32, (2 * tq, tq), 0)
    row = jnp.where(row >= tq, row - tq, row)
    col = lax.broadcasted_iota(I32, (2 * tq, tq), 1)
    strict = col < row
    r1 = lax.broadcasted_iota(I32, (tq, tq), 0)
    c1 = lax.broadcasted_iota(I32, (tq, tq), 1)
    after = (r1 > c1).astype(BF16)
    q2 = _stack_halves(q_ref[0], lane)

    def block(j, carry, masked):
        run, acc = carry
        start = pl.multiple_of(j * tq, tq)
        kb = k_ref[0, pl.ds(start, tq), :]
        vb = v_ref[0, pl.ds(start, tq), :]
        z2 = _nt_dot(q2, kb) * (scale * LOG2_E)
        neg = -z2
        keep2 = jnp.minimum(neg, 0.0) - jnp.log2(1.0 + jnp.exp2(jnp.minimum(z2, neg)))
        lk = jnp.where(strict, keep2, 0.0) if masked else keep2
        hi = lk.astype(BF16)
        lo = (lk - hi.astype(F32)).astype(BF16)
        suffix = _dot(hi, after) + _dot(lo, after)
        w = jnp.exp2(z2 + keep2 + suffix + run)
        if masked:
            w = jnp.where(strict, w, 0.0)
        wb = w.astype(BF16)
        zero = jnp.zeros_like(vb)
        acc = (acc + _dot(wb[:tq], jnp.where(lane < HEAD_DIM, vb, zero))
               + _dot(wb[tq:], jnp.where(lane >= HEAD_DIM, vb, zero)))
        run = run + suffix[:, 0:1] + lk[:, 0:1]
        return run, acc

    run, acc = block(i, (jnp.zeros((2 * tq, 1), F32), jnp.zeros((tq, LANES), F32)), True)

    def alive(c):
        return jnp.logical_and(c[0] >= 0, c[1] > DEAD_LOG2)

    def step(c):
        j, _, run, acc = c
        run, acc = block(j, (run, acc), False)
        return j - 1, jnp.max(run), run, acc

    _, _, _, acc = lax.while_loop(alive, step, (i - 1, jnp.max(run), run, acc))
    o_ref[0] = acc.astype(o_ref.dtype)


def _sb_attention(proj, bsz, seq):
    tq = min(TQ_ATT, seq)
    pairs = SB_HEADS * HEAD_DIM // LANES
    return pl.pallas_call(
        functools.partial(_sb_kernel, tq=tq),
        grid=(bsz, pairs, seq // tq),
        in_specs=[pl.BlockSpec((1, tq, LANES), lambda b, p, i: (b, i, p)),
                  pl.BlockSpec((1, seq, LANES), lambda b, p, i: (b, 0, pairs + p)),
                  pl.BlockSpec((1, seq, LANES), lambda b, p, i: (b, 0, 2 * pairs + p))],
        out_specs=pl.BlockSpec((1, tq, LANES), lambda b, p, i: (b, i, p)),
        out_shape=jax.ShapeDtypeStruct((bsz, seq, SB_HEADS * HEAD_DIM), BF16),
        compiler_params=_cparams("parallel", "parallel", "parallel"),
        name="stickbreak_attn",
    )(proj, proj, proj)


def _diff_kernel(slope_ref, lq1_ref, lk1_ref, lq2_ref, lk2_ref, subw_ref, q_ref, k_ref, v_ref, o_ref,
                 *, tq, lam_init):
    h = pl.program_id(1)
    i = pl.program_id(2)
    scale = DIFF_SUB ** -0.5
    slope = slope_ref[h]
    lam = (jnp.exp(jnp.sum(lq1_ref[...] * lk1_ref[...], keepdims=True))
           - jnp.exp(jnp.sum(lq2_ref[...] * lk2_ref[...], keepdims=True)) + lam_init)
    lane = lax.broadcasted_iota(I32, (1, LANES), 1)
    row = lax.broadcasted_iota(I32, (2 * tq, tq), 0)
    row = jnp.where(row >= tq, row - tq, row)
    col = lax.broadcasted_iota(I32, (2 * tq, tq), 1)
    causal = col <= row
    key_off = lax.broadcasted_iota(I32, (1, tq), 1)
    q2 = _stack_halves(q_ref[0], lane)

    def block(j, carry, masked):
        m, l, acc = carry
        start = j * tq
        kb = k_ref[0, pl.ds(start, tq), :]
        vb = v_ref[0, pl.ds(start, tq), :]
        s = _nt_dot(q2, kb) * (scale * LOG2_E) + (slope * LOG2_E) * (start + key_off).astype(F32)
        if masked:
            s = jnp.where(causal, s, -jnp.inf)
        m_new = jnp.maximum(m, jnp.max(s, axis=-1, keepdims=True))
        a = jnp.exp2(m - m_new)
        p = jnp.exp2(s - m_new)
        l = a * l + jnp.sum(p, axis=-1, keepdims=True)
        acc = a * acc + _dot(p.astype(BF16), vb)
        return m_new, l, acc

    for n_left in range(k_ref.shape[1] // tq):
        @pl.when(i == n_left)
        def _(n_left=n_left):
            carry = (jnp.full((2 * tq, 1), -jnp.inf, F32), jnp.zeros((2 * tq, 1), F32),
                     jnp.zeros((2 * tq, LANES), F32))
            carry = block(n_left, carry, True)
            for j in range(n_left):
                carry = block(j, carry, False)
            out = carry[2] / carry[1]
            o = out[:tq] - lam * out[tq:]
            o = o * lax.rsqrt(jnp.mean(o * o, axis=-1, keepdims=True) + RMS_EPS) * subw_ref[...]
            o_ref[0] = (o * (1.0 - lam_init)).astype(o_ref.dtype)


def _diff_attention(proj, slopes, lq1, lk1, lq2, lk2, subw, bsz, seq, lam_init):
    tq = min(TQ_ATT, seq)
    q0 = 3 * SB_HEADS * HEAD_DIM // LANES
    k0 = q0 + DIFF_HEADS
    v0 = k0 + DIFF_HEADS
    vec = lambda n: pl.BlockSpec((1, n), lambda b, h, i: (0, 0))
    return pl.pallas_call(
        functools.partial(_diff_kernel, tq=tq, lam_init=lam_init),
        grid=(bsz, DIFF_HEADS, seq // tq),
        in_specs=[pl.BlockSpec(memory_space=pltpu.SMEM),
                  vec(DIFF_SUB), vec(DIFF_SUB), vec(DIFF_SUB), vec(DIFF_SUB), vec(DIFF_VDIM),
                  pl.BlockSpec((1, tq, LANES), lambda b, h, i: (b, i, q0 + h)),
                  pl.BlockSpec((1, seq, LANES), lambda b, h, i: (b, 0, k0 + h)),
                  pl.BlockSpec((1, seq, LANES), lambda b, h, i: (b, 0, v0 + h))],
        out_specs=pl.BlockSpec((1, tq, LANES), lambda b, h, i: (b, i, h)),
        out_shape=jax.ShapeDtypeStruct((bsz, seq, DIFF_HEADS * DIFF_VDIM), BF16),
        compiler_params=_cparams("parallel", "parallel", "parallel"),
        name="diff_attn",
    )(slopes, lq1, lk1, lq2, lk2, subw, proj, proj, proj)


def _tree_sum_rows(x):
    n = x.shape[0] // 8
    x = x.reshape(n, 8, x.shape[1])
    while n > 1:
        n //= 2
        x = x[:n] + x[n:]
    return x[0]


def _tree_max_rows(x):
    n = x.shape[0] // 8
    x = x.reshape(n, 8, x.shape[1])
    while n > 1:
        n //= 2
        x = jnp.maximum(x[:n], x[n:])
    return x[0]


def _dsa_kernel(q_ref, ckv_ref, kvw_ref, qi_ref, ki_ref, wi_ref, qx_ref, kx_ref, wup_ref, o_ref,
                cn_ref, cnt_ref, skey_ref, bias_ref, s_ref, *, tq, tk, k_sel, hg):
    i = pl.program_id(1)
    t0 = i * tq

    @pl.when(i == 0)
    def _():
        c = ckv_ref[0]
        cn = c * lax.rsqrt(jnp.mean(c * c, axis=-1, keepdims=True) + RMS_EPS) * kvw_ref[...]
        cn_ref[:, :DSA_LATENT] = cn.astype(BF16)
        cn_ref[:, DSA_LATENT:] = kx_ref[...]
        cnt_ref[:DSA_LATENT, :] = cn.T.astype(BF16)
        row = lax.broadcasted_iota(I32, (BF16_ROWS, cnt_ref.shape[1]), 0)
        cnt_ref[DSA_LATENT:, :] = jnp.where(row == 0, 1.0, 0.0).astype(BF16)

    for k in range(ckv_ref.shape[1] // tk):
        @pl.when((t0 + tq - 1) // tk == k)
        def _(k=k):
            _dsa_block(q_ref, qi_ref, ki_ref, wi_ref, qx_ref, wup_ref, o_ref, cn_ref, cnt_ref, skey_ref, bias_ref,
                       s_ref, t0=t0, n_tiles=k + 1, tq=tq, tk=tk, k_sel=k_sel, hg=hg)


def _static_loop(n, body, carry):
    for j in range(n):
        carry = body(j, carry)
    return carry


def _dsa_block(q_ref, qi_ref, ki_ref, wi_ref, qx_ref, wup_ref, o_ref, cn_ref, cnt_ref, skey_ref, bias_ref, s_ref,
               *, t0, n_tiles, tq, tk, k_sel, hg):
    def tile_slice(j):
        return pl.ds(j * tk, tk)

    lane = lax.broadcasted_iota(I32, (1, LANES), 1)
    q_pos = t0 + lax.broadcasted_iota(I32, (tk, tq), 1)
    key_off = lax.broadcasted_iota(I32, (tk, tq), 0)

    w_t = (wi_ref[0] * ((IDX_HEADS ** -0.5) * (IDX_DIM ** -0.5))).T
    stacked = []
    for h in range(IDX_HEADS):
        qp = qi_ref[0, :, (h // 2) * LANES:(h // 2 + 1) * LANES]
        stacked.append(jnp.where((lane // IDX_DIM) == (h % 2), qp, jnp.zeros_like(qp)))
    q8 = jnp.concatenate(stacked, axis=0)

    def score_tile(j, _):
        d = _nt_dot(ki_ref[0, tile_slice(j), :], q8)
        score = jnp.zeros((tk, tq), F32)
        for h in range(IDX_HEADS):
            score = score + jnp.maximum(d[:, h * tq:(h + 1) * tq], 0.0) * w_t[h:h + 1, :]
        score = jnp.where(score == 0.0, 0.0, score)
        bits = pltpu.bitcast(score, I32)
        skey = jnp.where(bits < 0, bits ^ 0x7FFFFFFF, bits)
        skey_ref[tile_slice(j), :] = jnp.where(j * tk + key_off <= q_pos, skey, INT_MIN)
        return 0

    _static_loop(n_tiles, score_tile, 0)

    def count(pred):
        def tile(j, cnt):
            return cnt + _tree_sum_rows(pred(skey_ref[tile_slice(j), :]).astype(F32))
        cnt = _static_loop(n_tiles, tile, jnp.zeros((8, tq), F32))
        return jnp.sum(cnt, axis=0, keepdims=True)

    kf = float(k_sel)
    thr = jnp.where(count(lambda sk: sk >= 0) >= kf, 0, INT_MIN).astype(I32)

    def bit_step(b, thr):
        cand = thr + jnp.left_shift(jnp.int32(1), 30 - b)
        return jnp.where(count(lambda sk: sk >= cand) >= kf, cand, thr)

    thr = lax.fori_loop(0, 31, bit_step, thr)
    need = kf - count(lambda sk: sk > thr)

    r2 = lax.broadcasted_iota(I32, (tk, tk), 0)
    c2 = lax.broadcasted_iota(I32, (tk, tk), 1)
    upto = (c2 <= r2).astype(BF16)

    def tie_tile(j, seen):
        sk = skey_ref[tile_slice(j), :]
        tied = sk == thr
        prefix = _dot(upto, tied.astype(F32).astype(BF16)) + seen
        selected = ((sk > thr) | (tied & (prefix <= need))) & (j * tk + key_off <= q_pos)
        bias_ref[tile_slice(j), :] = jnp.where(selected, 0.0, MASKED)
        return prefix[tk - 1:tk, :]

    _static_loop(n_tiles, tie_tile, jnp.zeros((1, tq), F32))

    log2_scale = (DSA_LATENT ** -0.5) * LOG2_E

    def head_group(g, _):
        heads = [g * hg + u for u in range(hg)]
        offs = [pl.multiple_of(h * DSA_LATENT, DSA_LATENT) for h in heads]
        qg = jnp.concatenate(
            [jnp.concatenate([q_ref[0, :, pl.ds(off, DSA_LATENT)],
                              jnp.broadcast_to(qx_ref[pl.ds(h, 1), :], (tq, LANES)).astype(BF16)], axis=1)
             for h, off in zip(heads, offs)], axis=0)

        def logits_tile(j, m):
            d = _nt_dot(cn_ref[tile_slice(j), :], qg)
            b = bias_ref[tile_slice(j), :]
            tops = []
            for u in range(hg):
                s = d[:, u * tq:(u + 1) * tq] * log2_scale + b
                s_ref[tile_slice(j), u * tq:(u + 1) * tq] = s
                tops.append(jnp.max(_tree_max_rows(s), axis=0, keepdims=True))
            return jnp.maximum(m, jnp.concatenate(tops, axis=1))

        m = _static_loop(n_tiles, logits_tile, jnp.full((1, hg * tq), MASKED, F32))

        def value_tile(j, acc):
            p = jnp.exp2((s_ref[tile_slice(j), :] - m).astype(BF16))
            return acc + _dot(cnt_ref[:, tile_slice(j)], p)

        acc = _static_loop(n_tiles, value_tile, jnp.zeros((DSA_LATENT + BF16_ROWS, hg * tq), F32))
        out = (acc[:DSA_LATENT] / acc[DSA_LATENT:DSA_LATENT + 1]).astype(BF16)
        for u in range(0, hg, 2):
            pair = jnp.concatenate([out[:, u * tq:(u + 1) * tq], out[:, (u + 1) * tq:(u + 2) * tq]], axis=0)
            up = _dot(wup_ref[(g * hg + u) // 2], pair)
            off = pl.multiple_of((g * hg + u) * DSA_VDIM, 2 * DSA_VDIM)
            o_ref[0, :, pl.ds(off, 2 * DSA_VDIM)] = up.T.astype(o_ref.dtype)
        return 0

    lax.fori_loop(0, DSA_HEADS // hg, head_group, 0)


def _bf16_parts(x, n):
    parts = []
    for _ in range(n):
        p = x.astype(BF16).astype(F32)
        parts.append(p)
        x = x - p
    return parts


def _dsa_attention(proj, small, kvw, w_uv, bsz, seq):
    tq = min(TQ_DSA, seq)
    tk = min(TRI, seq)
    k_sel = min(TOPK_MAX, seq // 4)
    qw = DSA_HEADS * DSA_LATENT
    qiw = IDX_HEADS * IDX_DIM
    slope = _alibi_slopes(DSA_HEADS) / (DSA_LATENT ** -0.5)
    s_parts = _bf16_parts(slope, 3)
    pos = jnp.arange(seq, dtype=I32)
    p_parts = [(pos // LANES * LANES).astype(F32), (pos % LANES).astype(F32)]
    qx = jnp.stack([sp for _ in p_parts for sp in s_parts], axis=1)
    kx = jnp.stack([pp for pp in p_parts for _ in s_parts], axis=1)
    qx = jnp.pad(qx, ((0, 0), (0, LANES - qx.shape[1])))
    kx = jnp.pad(kx, ((0, 0), (0, LANES - kx.shape[1]))).astype(BF16)
    wt = jnp.swapaxes(w_uv, 1, 2).reshape(DSA_HEADS // 2, 2, DSA_VDIM, DSA_LATENT)
    zero = jnp.zeros_like(wt[:, 0])
    wup = jnp.concatenate([jnp.concatenate([wt[:, 0], zero], axis=2),
                           jnp.concatenate([zero, wt[:, 1]], axis=2)], axis=1).astype(BF16)
    return pl.pallas_call(
        functools.partial(_dsa_kernel, tq=tq, tk=tk, k_sel=k_sel, hg=HG_DSA),
        grid=(bsz, seq // tq),
        in_specs=[pl.BlockSpec((1, tq, qw), lambda b, i: (b, i, 0)),
                  pl.BlockSpec((1, seq, DSA_LATENT), lambda b, i: (b, 0, 0)),
                  pl.BlockSpec((1, DSA_LATENT), lambda b, i: (0, 0)),
                  pl.BlockSpec((1, tq, qiw), lambda b, i: (b, i, qw // qiw)),
                  pl.BlockSpec((1, seq, LANES), lambda b, i: (b, 0, (qw + qiw) // LANES)),
                  pl.BlockSpec((1, tq, LANES), lambda b, i: (b, i, 1)),
                  pl.BlockSpec((DSA_HEADS, LANES), lambda b, i: (0, 0)),
                  pl.BlockSpec((seq, LANES), lambda b, i: (0, 0)),
                  pl.BlockSpec(wup.shape, lambda b, i: (0, 0, 0))],
        out_specs=pl.BlockSpec((1, tq, DSA_HEADS * DSA_VDIM), lambda b, i: (b, i, 0)),
        out_shape=jax.ShapeDtypeStruct((bsz, seq, DSA_HEADS * DSA_VDIM), BF16),
        scratch_shapes=[pltpu.VMEM((seq, DSA_LATENT + LANES), BF16),
                        pltpu.VMEM((DSA_LATENT + BF16_ROWS, seq), BF16),
                        pltpu.VMEM((seq, tq), I32), pltpu.VMEM((seq, tq), F32),
                        pltpu.VMEM((seq, HG_DSA * tq), F32)],
        compiler_params=_cparams("parallel", "arbitrary"),
        name="dsa_attn",
    )(proj, small, kvw, proj, proj, small, qx, kx, wup)


def _mix_tail(mix, h_ref, g_ref, b_ref, rw_ref, rb_ref, hout_ref, logit_ref):
    hn = _layer_norm(DEEPNORM_ALPHA * h_ref[...] + mix, g_ref[...], b_ref[...])
    hout_ref[...] = hn
    logit_ref[...] = _dot(hn.astype(BF16), rw_ref[...]) + rb_ref[...]


def _mix_even_kernel(oa_ref, ob_ref, wa_ref, wb_ref, h_ref, g_ref, b_ref, rw_ref, rb_ref, hout_ref, logit_ref):
    mix = _dot(oa_ref[...], wa_ref[...]) + _dot(ob_ref[...], wb_ref[...])
    _mix_tail(mix, h_ref, g_ref, b_ref, rw_ref, rb_ref, hout_ref, logit_ref)


def _mix_odd_kernel(o_ref, wo_ref, h_ref, g_ref, b_ref, rw_ref, rb_ref, hout_ref, logit_ref):
    _mix_tail(_dot(o_ref[...], wo_ref[...]), h_ref, g_ref, b_ref, rw_ref, rb_ref, hout_ref, logit_ref)


def _mix_call(kernel, name, acts, weights, h, g, b, rw, rb):
    t, d = h.shape
    tm = min(TM_PROJ, t)
    row = lambda n: pl.BlockSpec((tm, n), lambda i: (i, 0))
    full = lambda a: pl.BlockSpec(a.shape, lambda i: (0, 0))
    return pl.pallas_call(
        kernel,
        grid=(t // tm,),
        in_specs=[row(a.shape[1]) for a in acts] + [full(w) for w in weights]
                 + [row(d), full(g), full(b), full(rw), full(rb)],
        out_specs=[row(d), row(LANES)],
        out_shape=[jax.ShapeDtypeStruct((t, d), F32), jax.ShapeDtypeStruct((t, LANES), F32)],
        compiler_params=_cparams("parallel"),
        name=name,
    )(*acts, *weights, h, g, b, rw, rb)


def _route_kernel(logit_ref, idx_ref, gate_ref, rank_ref, cnt_ref, *, tm):
    @pl.when(pl.program_id(0) == 0)
    def _():
        cnt_ref[...] = jnp.zeros_like(cnt_ref)

    lane = lax.broadcasted_iota(I32, (tm, LANES), 1)
    lane_f = lane.astype(F32)
    x = jnp.where(lane < N_EXPERTS, logit_ref[...], -jnp.inf)
    vals, hots = [], []
    for k in range(TOP_K):
        m = jnp.max(x, axis=-1, keepdims=True)
        first = jnp.min(jnp.where(x == m, lane_f, float(LANES)), axis=-1, keepdims=True)
        hot = lane_f == first
        x = jnp.where(hot, -jnp.inf, x)
        vals.append(m)
        hots.append(hot)
        idx_ref[:, k:k + 1] = first.astype(I32)
    exps = [jnp.exp(v - vals[0]) for v in vals]
    denom = exps[0] + exps[1] + exps[2] + exps[3]
    for k in range(TOP_K):
        gate_ref[:, k:k + 1] = exps[k] / denom

    hot_sum = (hots[0] | hots[1] | hots[2] | hots[3]).astype(F32)
    r = lax.broadcasted_iota(I32, (tm, tm), 0)
    c = lax.broadcasted_iota(I32, (tm, tm), 1)
    earlier = (c < r).astype(BF16)
    before = _dot(earlier, hot_sum.astype(BF16)) + cnt_ref[...]
    for k in range(TOP_K):
        rank_ref[:, k:k + 1] = jnp.sum(jnp.where(hots[k], before, 0.0), axis=-1, keepdims=True).astype(I32)
    cnt_ref[...] += jnp.sum(hot_sum, axis=0, keepdims=True)


def _route(logits):
    t = logits.shape[0]
    tm = min(TM_ROUTE, t)
    narrow = pl.BlockSpec((tm, TOP_K), lambda i: (i, 0))
    return pl.pallas_call(
        functools.partial(_route_kernel, tm=tm),
        grid=(t // tm,),
        in_specs=[pl.BlockSpec((tm, LANES), lambda i: (i, 0))],
        out_specs=[narrow, narrow, narrow, pl.BlockSpec((1, LANES), lambda i: (0, 0))],
        out_shape=[jax.ShapeDtypeStruct((t, TOP_K), I32), jax.ShapeDtypeStruct((t, TOP_K), F32),
                   jax.ShapeDtypeStruct((t, TOP_K), I32), jax.ShapeDtypeStruct((1, LANES), F32)],
        compiler_params=_cparams("arbitrary"),
        name="route",
    )(logits)


def _row_copy(src_ref, src_row, dst_ref, dst_row, sem):
    return pltpu.make_async_copy(src_ref.at[pl.ds(src_row, 1), :], dst_ref.at[pl.ds(dst_row, 1), :], sem)


def _dispatch_kernel(pad_start_ref, pad_count_ref, n_used_ref, pos_ref, x_ref, xs_out, zero_ref, sem, pad_sem,
                     *, td):
    @pl.when(pl.program_id(0) == 0)
    def _():
        zero_ref[...] = jnp.zeros_like(zero_ref)

        def fill_for(start_not_wait):
            def fill(present, n_rows, off):
                copy = pltpu.make_async_copy(zero_ref.at[pl.ds(0, n_rows), :], xs_out.at[pl.ds(off, n_rows), :],
                                             pad_sem)

                @pl.when(jnp.logical_and(present, start_not_wait))
                def _():
                    copy.start()

                @pl.when(jnp.logical_and(present, jnp.logical_not(start_not_wait)))
                def _():
                    copy.wait()
            return fill

        def pads(e, start_not_wait):
            fill = fill_for(start_not_wait)
            off = pad_start_ref[e]
            count = pad_count_ref[e]
            head = jnp.minimum(count, (-off) & (SUBLANES - 1))
            for r in range(SUBLANES - 1):
                fill(r < head, 1, off + r)
            off = off + head
            count = count - head
            chunk = zero_ref.shape[0] // 2
            while chunk >= SUBLANES:
                present = (count & chunk) != 0
                fill(present, chunk, pl.multiple_of(off, SUBLANES))
                off = off + jnp.where(present, chunk, 0)
                chunk //= 2
            for r in range(SUBLANES - 1):
                fill(r < (count & (SUBLANES - 1)), 1, off + r)
            return start_not_wait

        def tail(j, start_not_wait):
            tile = zero_ref.shape[0]
            fill_for(start_not_wait)(j >= n_used_ref[0], tile, pl.multiple_of(j * tile, tile))
            return start_not_wait

        n_tiles = xs_out.shape[0] // zero_ref.shape[0]
        lax.fori_loop(0, N_EXPERTS, pads, True)
        lax.fori_loop(0, n_tiles, tail, True)
        lax.fori_loop(0, N_EXPERTS, pads, False)
        lax.fori_loop(0, n_tiles, tail, False)

    def issue(i, _):
        for k in range(TOP_K):
            _row_copy(x_ref, i, xs_out, pos_ref[i * TOP_K + k], sem).start()
        return 0

    lax.fori_loop(0, td, issue, 0)

    def drain(i, _):
        for k in range(TOP_K):
            _row_copy(x_ref, i, xs_out, pos_ref[i * TOP_K + k], sem).wait()
        return 0

    lax.fori_loop(0, td, drain, 0)


def _dispatch(pad_start, pad_count, n_used, pos_flat, x, n_rows):
    t, d = x.shape
    td = min(TD_DISPATCH, t)
    grid_spec = pltpu.PrefetchScalarGridSpec(
        num_scalar_prefetch=3,
        grid=(t // td,),
        in_specs=[pl.BlockSpec((td * TOP_K,), lambda i, ps, pc, nu: (i,), memory_space=pltpu.SMEM),
                  pl.BlockSpec((td, d), lambda i, ps, pc, nu: (i, 0))],
        out_specs=pl.BlockSpec(memory_space=pl.ANY),
        scratch_shapes=[pltpu.VMEM((TM_FFN, d), x.dtype), pltpu.SemaphoreType.DMA, pltpu.SemaphoreType.DMA],
    )
    return pl.pallas_call(
        functools.partial(_dispatch_kernel, td=td),
        grid_spec=grid_spec,
        out_shape=jax.ShapeDtypeStruct((n_rows, d), x.dtype),
        compiler_params=_cparams("arbitrary"),
        name="moe_dispatch",
    )(pad_start, pad_count, n_used, pos_flat, x)


def _ffn_kernel(tile_expert_ref, n_used_ref, xs_ref, wgu_ref, bgu_ref, wd_ref, bd_ref, y_ref, wgu_bf, wd_bf):
    i = pl.program_id(0)
    d_ff = wd_ref.shape[2]
    new_expert = jnp.logical_or(i == 0, tile_expert_ref[i] != tile_expert_ref[jnp.maximum(i - 1, 0)])

    @pl.when(jnp.logical_and(new_expert, i < n_used_ref[0]))
    def _():
        wgu_bf[...] = wgu_ref[0, 0].astype(BF16)
        wd_bf[...] = wd_ref[0, 0].astype(BF16)

    @pl.when(i < n_used_ref[0])
    def _():
        x = xs_ref[...].astype(BF16)
        hgu = _dot(x, wgu_bf[...]) + bgu_ref[0, 0]
        gate = jnp.minimum(hgu[:, :d_ff], SWIGLU_LIMIT)
        up = jnp.clip(hgu[:, d_ff:], -SWIGLU_LIMIT, SWIGLU_LIMIT)
        act = gate * (1.0 / (1.0 + jnp.exp(-SWIGLU_ALPHA * gate))) * (up + 1.0)
        y_ref[...] = _dot(act.astype(BF16), wd_bf[...]) + bd_ref[0, 0]

    @pl.when(i >= n_used_ref[0])
    def _():
        y_ref[...] = jnp.zeros_like(y_ref)


def _expert_ffn(tile_expert, n_used, xs, w_gu, b_gu, w_down, b_down, layer):
    n_rows, d = xs.shape
    depth, e, _, f2 = w_gu.shape
    d_ff = w_down.shape[2]
    tm = TM_FFN
    grid_spec = pltpu.PrefetchScalarGridSpec(
        num_scalar_prefetch=2,
        grid=(n_rows // tm,),
        in_specs=[pl.BlockSpec((tm, d), lambda i, te, nu: (jnp.minimum(i, nu[0] - 1), 0)),
                  pl.BlockSpec((1, 1, d, f2), lambda i, te, nu: (layer, te[i], 0, 0)),
                  pl.BlockSpec((1, 1, 1, f2), lambda i, te, nu: (layer, te[i], 0, 0)),
                  pl.BlockSpec((1, 1, d_ff, d), lambda i, te, nu: (layer, te[i], 0, 0)),
                  pl.BlockSpec((1, 1, 1, d), lambda i, te, nu: (layer, te[i], 0, 0))],
        out_specs=pl.BlockSpec((tm, d), lambda i, te, nu: (i, 0)),
        scratch_shapes=[pltpu.VMEM((d, f2), BF16), pltpu.VMEM((d_ff, d), BF16)],
    )
    return pl.pallas_call(
        _ffn_kernel,
        grid_spec=grid_spec,
        out_shape=jax.ShapeDtypeStruct((n_rows, d), F32),
        compiler_params=_cparams("arbitrary"),
        name="expert_ffn",
    )(tile_expert, n_used, xs, w_gu, b_gu.reshape(depth, e, 1, f2), w_down, b_down.reshape(depth, e, 1, d))


def _combine_kernel(pos_ref, gate_ref, h_ref, g_ref, b_ref, y_hbm, out_ref, buf, sem, *, tc):
    def issue(i, _):
        for k in range(TOP_K):
            _row_copy(y_hbm, pos_ref[i * TOP_K + k], buf.at[k], i, sem).start()
        return 0

    lax.fori_loop(0, tc, issue, 0)

    def drain(i, _):
        for k in range(TOP_K):
            _row_copy(y_hbm, pos_ref[i * TOP_K + k], buf.at[k], i, sem).wait()
        return 0

    lax.fori_loop(0, tc, drain, 0)

    gates = gate_ref[...]
    ffn = gates[:, 0:1] * buf[0]
    for k in range(1, TOP_K):
        ffn = ffn + gates[:, k:k + 1] * buf[k]
    out_ref[...] = _layer_norm(DEEPNORM_ALPHA * h_ref[...] + ffn, g_ref[...], b_ref[...])


def _combine(pos_flat, gates, h, g, b, y):
    t, d = h.shape
    tc = min(TC_COMBINE, t)
    return pl.pallas_call(
        functools.partial(_combine_kernel, tc=tc),
        grid=(t // tc,),
        in_specs=[pl.BlockSpec((tc * TOP_K,), lambda i: (i,), memory_space=pltpu.SMEM),
                  pl.BlockSpec((tc, TOP_K), lambda i: (i, 0)),
                  pl.BlockSpec((tc, d), lambda i: (i, 0)),
                  pl.BlockSpec((1, d), lambda i: (0, 0)),
                  pl.BlockSpec((1, d), lambda i: (0, 0)),
                  pl.BlockSpec(memory_space=pl.ANY)],
        out_specs=pl.BlockSpec((tc, d), lambda i: (i, 0)),
        out_shape=jax.ShapeDtypeStruct((t, d), F32),
        scratch_shapes=[pltpu.VMEM((TOP_K, tc, d), F32), pltpu.SemaphoreType.DMA],
        compiler_params=_cparams("arbitrary"),
        name="moe_combine",
    )(pos_flat, gates, h, g, b, y)


def _moe(h, logits, w_gu, b_gu, w_down, b_down, g, b, layer):
    t = h.shape[0]
    idx, gates, rank, counts = _route(logits)
    counts = counts[0, :N_EXPERTS].astype(I32)
    tiles = (counts + TM_FFN - 1) // TM_FFN
    tile_end = jnp.cumsum(tiles)
    start = (tile_end - tiles) * TM_FFN
    n_tiles = (t * TOP_K) // TM_FFN + N_EXPERTS
    tile_ids = jnp.arange(n_tiles, dtype=I32)
    tile_expert = jnp.minimum(jnp.sum((tile_end[None, :] <= tile_ids[:, None]).astype(I32), axis=1), N_EXPERTS - 1)
    pos = (start[idx] + rank).reshape(-1).astype(I32)
    n_used = tile_end[-1:].astype(I32)
    xs = _dispatch(start + counts, tiles * TM_FFN - counts, n_used, pos, h, n_tiles * TM_FFN)
    y = _expert_ffn(tile_expert, n_used, xs, w_gu, b_gu, w_down, b_down, layer)
    return _combine(pos, gates, h, g, b, y)


def _router_params(router_w, router_b):
    rw = jnp.pad(router_w, ((0, 0), (0, LANES - N_EXPERTS))).astype(BF16)
    rb = jnp.pad(router_b, (0, LANES - N_EXPERTS)).reshape(1, LANES)
    return rw, rb


def _alibi_slopes(n):
    return 2.0 ** (-8.0 * jnp.arange(1, n + 1, dtype=F32) / n)


def _even_layer(h, bsz, seq, w_in, w_out, lq1, lk1, lq2, lk2, subw, layer, g, b, rw, rb):
    proj = _project(h, [w_in.astype(BF16)], [BF16])[0].reshape(bsz, seq, -1)
    o_a = _sb_attention(proj, bsz, seq)
    lam_init = 0.8 - 0.6 * math.exp(-0.3 * layer)
    row = lambda v: v.reshape(1, -1)
    o_b = _diff_attention(proj, _alibi_slopes(DIFF_HEADS), row(lq1), row(lk1), row(lq2), row(lk2), row(subw),
                          bsz, seq, lam_init)
    t = bsz * seq
    sbw = SB_HEADS * HEAD_DIM
    wo = w_out.astype(BF16)
    return _mix_call(_mix_even_kernel, "mix_even", [o_a.reshape(t, -1), o_b.reshape(t, -1)],
                     [wo[:sbw], wo[sbw:]], h, row(g), row(b), rw, rb)


def _odd_layer(h, bsz, seq, w_in, kvw, w_uv, w_out, g, b, rw, rb):
    qw = DSA_HEADS * DSA_LATENT
    c0, c1 = qw, qw + DSA_LATENT
    i0, i1 = c1, c1 + IDX_HEADS * IDX_DIM
    k1 = i1 + IDX_DIM
    w_main = jnp.concatenate([w_in[:, :qw], w_in[:, i0:i1], w_in[:, i1:k1], w_in[:, i1:k1]], axis=1).astype(BF16)
    w_small = jnp.pad(jnp.concatenate([w_in[:, c0:c1], w_in[:, k1:]], axis=1),
                      ((0, 0), (0, LANES - IDX_HEADS))).astype(BF16)
    proj, small = [p.reshape(bsz, seq, -1) for p in _project(h, [w_main, w_small], [BF16, F32])]
    o = _dsa_attention(proj, small, kvw.reshape(1, -1), w_uv, bsz, seq)
    row = lambda v: v.reshape(1, -1)
    return _mix_call(_mix_odd_kernel, "mix_odd", [o.reshape(bsz * seq, -1)], [w_out.astype(BF16)],
                     h, row(g), row(b), rw, rb)


def kernel(x, ev_w_in, ev_w_out, ev_lambda_q1, ev_lambda_k1, ev_lambda_q2, ev_lambda_k2, ev_subln_w, od_w_in, od_kv_norm_w, od_w_uv, od_w_out, ln_mix_g, ln_mix_b, router_w, router_b, exp_w_gu, exp_b_gu, exp_w_down, exp_b_down, ln_ffn_g, ln_ffn_b):
    bsz, seq, d = x.shape
    h = x.reshape(bsz * seq, d)
    for layer in range(ln_mix_g.shape[0]):
        j = layer // 2
        rw, rb = _router_params(router_w[layer], router_b[layer])
        if layer % 2 == 0:
            h, logits = _even_layer(h, bsz, seq, ev_w_in[j], ev_w_out[j], ev_lambda_q1[j], ev_lambda_k1[j],
                                    ev_lambda_q2[j], ev_lambda_k2[j], ev_subln_w[j], layer,
                                    ln_mix_g[layer], ln_mix_b[layer], rw, rb)
        else:
            h, logits = _odd_layer(h, bsz, seq, od_w_in[j], od_kv_norm_w[j], od_w_uv[j], od_w_out[j],
                                   ln_mix_g[layer], ln_mix_b[layer], rw, rb)
        h = _moe(h, logits, exp_w_gu, exp_b_gu, exp_w_down, exp_b_down,
                 ln_ffn_g[layer].reshape(1, -1), ln_ffn_b[layer].reshape(1, -1), layer)
    return h.reshape(bsz, seq, d)
```

```python
import functools
import math

import jax
import jax.numpy as jnp
from jax import lax
from jax.experimental import pallas as pl
from jax.experimental.pallas import tpu as pltpu

F32, BF16, I32 = jnp.float32, jnp.bfloat16, jnp.int32

HEAD_DIM = 64
SB_HEADS = 8
DIFF_HEADS = 4
DIFF_SUB = 64
DIFF_VDIM = 128
DSA_HEADS = 16
DSA_LATENT = 128
DSA_VDIM = 64
IDX_HEADS = 8
IDX_DIM = 64
TOPK_MAX = 256
N_EXPERTS = 32
TOP_K = 4
SWIGLU_LIMIT = 7.0
SWIGLU_ALPHA = 1.702
LN_EPS = 1e-5
RMS_EPS = 1e-5
DEPTH = 2
DEEPNORM_ALPHA = (2 * DEPTH) ** 0.25

LANES = 128
SUBLANES = 8
BF16_ROWS = 16
VMEM_LIMIT = 56 * 1024 * 1024
INT_MIN = -(2 ** 31)
MASKED = -1e30
DEAD_LOG2 = -152.0
LOG2_E = math.log2(math.e)

TM_PROJ = 512
TQ_ATT = 256
TQ_DSA = 128
TRI = 512
HG_DSA = 16
TM_ROUTE = 512
TM_FFN = 512
TD_DISPATCH = 512
TC_COMBINE = 256


def _cparams(*sem):
    return pltpu.CompilerParams(dimension_semantics=sem, vmem_limit_bytes=VMEM_LIMIT)


def _nt_dot(a, b):
    return lax.dot_general(a, b, (((1,), (1,)), ((), ())), preferred_element_type=F32)


def _dot(a, b):
    return jnp.dot(a, b, preferred_element_type=F32)


def _layer_norm(y, g, b):
    mu = jnp.mean(y, axis=-1, keepdims=True)
    d = y - mu
    var = jnp.mean(d * d, axis=-1, keepdims=True)
    return d * lax.rsqrt(var + LN_EPS) * g + b


def _col_chunk(n):
    for c in (512, 384, 256, 128):
        if n % c == 0:
            return c
    raise ValueError(n)


def _proj_kernel(x_ref, *refs):
    n_out = len(refs) // 2
    x = x_ref[...].astype(BF16)
    for w_ref, o_ref in zip(refs[:n_out], refs[n_out:]):
        n = o_ref.shape[-1]
        c = _col_chunk(n)
        for j in range(0, n, c):
            o_ref[:, j:j + c] = _dot(x, w_ref[:, j:j + c]).astype(o_ref.dtype)


def _project(x, weights, out_dtypes):
    t, k = x.shape
    tm = min(TM_PROJ, t)
    return pl.pallas_call(
        _proj_kernel,
        grid=(t // tm,),
        in_specs=[pl.BlockSpec((tm, k), lambda i: (i, 0))]
                 + [pl.BlockSpec(w.shape, lambda i: (0, 0)) for w in weights],
        out_specs=[pl.BlockSpec((tm, w.shape[1]), lambda i: (i, 0)) for w in weights],
        out_shape=[jax.ShapeDtypeStruct((t, w.shape[1]), dt) for w, dt in zip(weights, out_dtypes)],
        compiler_params=_cparams("parallel"),
        name="in_proj",
    )(x, *weights)


def _stack_halves(x, lane):
    zero = jnp.zeros_like(x)
    return jnp.concatenate([jnp.where(lane < HEAD_DIM, x, zero), jnp.where(lane >= HEAD_DIM, x, zero)], axis=0)


def _sb_kernel(q_ref, k_ref, v_ref, o_ref, *, tq):
    i = pl.program_id(2)
    scale = HEAD_DIM ** -0.5
    lane = lax.broadcasted_iota(I32, (1, LANES), 1)
    row = lax.broadcasted_iota(I32, (2 * tq, tq), 0)
    row = jnp.where(row >= tq, row - tq, row)
    col = lax.broadcasted_iota(I32, (2 * tq, tq), 1)
    strict = col < row
    r1 = lax.broadcasted_iota(I32, (tq, tq), 0)
    c1 = lax.broadcasted_iota(I32, (tq, tq), 1)
    after = (r1 > c1).astype(BF16)
    q2 = _stack_halves(q_ref[0], lane)

    def block(j, carry, masked):
        run, acc = carry
        start = pl.multiple_of(j * tq, tq)
        kb = k_ref[0, pl.ds(start, tq), :]
        vb = v_ref[0, pl.ds(start, tq), :]
        z2 = _nt_dot(q2, kb) * (scale * LOG2_E)
        neg = -z2
        keep2 = jnp.minimum(neg, 0.0) - jnp.log2(1.0 + jnp.exp2(jnp.minimum(z2, neg)))
        lk = jnp.where(strict, keep2, 0.0) if masked else keep2
        hi = lk.astype(BF16)
        lo = (lk - hi.astype(F32)).astype(BF16)
        suffix = _dot(hi, after) + _dot(lo, after)
        w = jnp.exp2(z2 + keep2 + suffix + run)
        if masked:
            w = jnp.where(strict, w, 0.0)
        wb = w.astype(BF16)
        zero = jnp.zeros_like(vb)
        acc = (acc + _dot(wb[:tq], jnp.where(lane < HEAD_DIM, vb, zero))
               + _dot(wb[tq:], jnp.where(lane >= HEAD_DIM, vb, zero)))
        run = run + suffix[:, 0:1] + lk[:, 0:1]
        return run, acc

    run, acc = block(i, (jnp.zeros((2 * tq, 1), F32), jnp.zeros((tq, LANES), F32)), True)

    def alive(c):
        return jnp.logical_and(c[0] >= 0, c[1] > DEAD_LOG2)

    def step(c):
        j, _, run, acc = c
        run, acc = block(j, (run, acc), False)
        return j - 1, jnp.max(run), run, acc

    _, _, _, acc = lax.while_loop(alive, step, (i - 1, jnp.max(run), run, acc))
    o_ref[0] = acc.astype(o_ref.dtype)


def _sb_attention(proj, bsz, seq):
    tq = min(TQ_ATT, seq)
    pairs = SB_HEADS * HEAD_DIM // LANES
    return pl.pallas_call(
        functools.partial(_sb_kernel, tq=tq),
        grid=(bsz, pairs, seq // tq),
        in_specs=[pl.BlockSpec((1, tq, LANES), lambda b, p, i: (b, i, p)),
                  pl.BlockSpec((1, seq, LANES), lambda b, p, i: (b, 0, pairs + p)),
                  pl.BlockSpec((1, seq, LANES), lambda b, p, i: (b, 0, 2 * pairs + p))],
        out_specs=pl.BlockSpec((1, tq, LANES), lambda b, p, i: (b, i, p)),
        out_shape=jax.ShapeDtypeStruct((bsz, seq, SB_HEADS * HEAD_DIM), BF16),
        compiler_params=_cparams("parallel", "parallel", "parallel"),
        name="stickbreak_attn",
    )(proj, proj, proj)


def _diff_kernel(slope_ref, lq1_ref, lk1_ref, lq2_ref, lk2_ref, subw_ref, q_ref, k_ref, v_ref, o_ref,
                 *, tq, lam_init):
    h = pl.program_id(1)
    i = pl.program_id(2)
    scale = DIFF_SUB ** -0.5
    slope = slope_ref[h]
    lam = (jnp.exp(jnp.sum(lq1_ref[...] * lk1_ref[...], keepdims=True))
           - jnp.exp(jnp.sum(lq2_ref[...] * lk2_ref[...], keepdims=True)) + lam_init)
    lane = lax.broadcasted_iota(I32, (1, LANES), 1)
    row = lax.broadcasted_iota(I32, (2 * tq, tq), 0)
    row = jnp.where(row >= tq, row - tq, row)
    col = lax.broadcasted_iota(I32, (2 * tq, tq), 1)
    causal = col <= row
    key_off = lax.broadcasted_iota(I32, (1, tq), 1)
    q2 = _stack_halves(q_ref[0], lane)

    def block(j, carry, masked):
        m, l, acc = carry
        start = j * tq
        kb = k_ref[0, pl.ds(start, tq), :]
        vb = v_ref[0, pl.ds(start, tq), :]
        s = _nt_dot(q2, kb) * (scale * LOG2_E) + (slope * LOG2_E) * (start + key_off).astype(F32)
        if masked:
            s = jnp.where(causal, s, -jnp.inf)
        m_new = jnp.maximum(m, jnp.max(s, axis=-1, keepdims=True))
        a = jnp.exp2(m - m_new)
        p = jnp.exp2(s - m_new)
        l = a * l + jnp.sum(p, axis=-1, keepdims=True)
        acc = a * acc + _dot(p.astype(BF16), vb)
        return m_new, l, acc

    for n_left in range(k_ref.shape[1] // tq):
        @pl.when(i == n_left)
        def _(n_left=n_left):
            carry = (jnp.full((2 * tq, 1), -jnp.inf, F32), jnp.zeros((2 * tq, 1), F32),
                     jnp.zeros((2 * tq, LANES), F32))
            carry = block(n_left, carry, True)
            for j in range(n_left):
                carry = block(j, carry, False)
            out = carry[2] / carry[1]
            o = out[:tq] - lam * out[tq:]
            o = o * lax.rsqrt(jnp.mean(o * o, axis=-1, keepdims=True) + RMS_EPS) * subw_ref[...]
            o_ref[0] = (o * (1.0 - lam_init)).astype(o_ref.dtype)


def _diff_attention(proj, slopes, lq1, lk1, lq2, lk2, subw, bsz, seq, lam_init):
    tq = min(TQ_ATT, seq)
    q0 = 3 * SB_HEADS * HEAD_DIM // LANES
    k0 = q0 + DIFF_HEADS
    v0 = k0 + DIFF_HEADS
    vec = lambda n: pl.BlockSpec((1, n), lambda b, h, i: (0, 0))
    return pl.pallas_call(
        functools.partial(_diff_kernel, tq=tq, lam_init=lam_init),
        grid=(bsz, DIFF_HEADS, seq // tq),
        in_specs=[pl.BlockSpec(memory_space=pltpu.SMEM),
                  vec(DIFF_SUB), vec(DIFF_SUB), vec(DIFF_SUB), vec(DIFF_SUB), vec(DIFF_VDIM),
                  pl.BlockSpec((1, tq, LANES), lambda b, h, i: (b, i, q0 + h)),
                  pl.BlockSpec((1, seq, LANES), lambda b, h, i: (b, 0, k0 + h)),
                  pl.BlockSpec((1, seq, LANES), lambda b, h, i: (b, 0, v0 + h))],
        out_specs=pl.BlockSpec((1, tq, LANES), lambda b, h, i: (b, i, h)),
        out_shape=jax.ShapeDtypeStruct((bsz, seq, DIFF_HEADS * DIFF_VDIM), BF16),
        compiler_params=_cparams("parallel", "parallel", "parallel"),
        name="diff_attn",
    )(slopes, lq1, lk1, lq2, lk2, subw, proj, proj, proj)


def _tree_sum_rows(x):
    n = x.shape[0] // 8
    x = x.reshape(n, 8, x.shape[1])
    while n > 1:
        n //= 2
        x = x[:n] + x[n:]
    return x[0]


def _tree_max_rows(x):
    n = x.shape[0] // 8
    x = x.reshape(n, 8, x.shape[1])
    while n > 1:
        n //= 2
        x = jnp.maximum(x[:n], x[n:])
    return x[0]


def _dsa_kernel(q_ref, ckv_ref, kvw_ref, qi_ref, ki_ref, wi_ref, qx_ref, kx_ref, wup_ref, o_ref,
                cn_ref, cnt_ref, skey_ref, bias_ref, s_ref, *, tq, tk, k_sel, hg):
    i = pl.program_id(1)
    t0 = i * tq

    @pl.when(i == 0)
    def _():
        c = ckv_ref[0]
        cn = c * lax.rsqrt(jnp.mean(c * c, axis=-1, keepdims=True) + RMS_EPS) * kvw_ref[...]
        cn_ref[:, :DSA_LATENT] = cn.astype(BF16)
        cn_ref[:, DSA_LATENT:] = kx_ref[...]
        cnt_ref[:DSA_LATENT, :] = cn.T.astype(BF16)
        row = lax.broadcasted_iota(I32, (BF16_ROWS, cnt_ref.shape[1]), 0)
        cnt_ref[DSA_LATENT:, :] = jnp.where(row == 0, 1.0, 0.0).astype(BF16)

    for k in range(ckv_ref.shape[1] // tk):
        @pl.when((t0 + tq - 1) // tk == k)
        def _(k=k):
            _dsa_block(q_ref, qi_ref, ki_ref, wi_ref, qx_ref, wup_ref, o_ref, cn_ref, cnt_ref, skey_ref, bias_ref,
                       s_ref, t0=t0, n_tiles=k + 1, tq=tq, tk=tk, k_sel=k_sel, hg=hg)


def _static_loop(n, body, carry):
    for j in range(n):
        carry = body(j, carry)
    return carry


def _dsa_block(q_ref, qi_ref, ki_ref, wi_ref, qx_ref, wup_ref, o_ref, cn_ref, cnt_ref, skey_ref, bias_ref, s_ref,
               *, t0, n_tiles, tq, tk, k_sel, hg):
    def tile_slice(j):
        return pl.ds(j * tk, tk)

    lane = lax.broadcasted_iota(I32, (1, LANES), 1)
    q_pos = t0 + lax.broadcasted_iota(I32, (tk, tq), 1)
    key_off = lax.broadcasted_iota(I32, (tk, tq), 0)

    w_t = (wi_ref[0] * ((IDX_HEADS ** -0.5) * (IDX_DIM ** -0.5))).T
    stacked = []
    for h in range(IDX_HEADS):
        qp = qi_ref[0, :, (h // 2) * LANES:(h // 2 + 1) * LANES]
        stacked.append(jnp.where((lane // IDX_DIM) == (h % 2), qp, jnp.zeros_like(qp)))
    q8 = jnp.concatenate(stacked, axis=0)

    def score_tile(j, _):
        d = _nt_dot(ki_ref[0, tile_slice(j), :], q8)
        score = jnp.zeros((tk, tq), F32)
        for h in range(IDX_HEADS):
            score = score + jnp.maximum(d[:, h * tq:(h + 1) * tq], 0.0) * w_t[h:h + 1, :]
        score = jnp.where(score == 0.0, 0.0, score)
        bits = pltpu.bitcast(score, I32)
        skey = jnp.where(bits < 0, bits ^ 0x7FFFFFFF, bits)
        skey_ref[tile_slice(j), :] = jnp.where(j * tk + key_off <= q_pos, skey, INT_MIN)
        return 0

    _static_loop(n_tiles, score_tile, 0)

    def count(pred):
        def tile(j, cnt):
            return cnt + _tree_sum_rows(pred(skey_ref[tile_slice(j), :]).astype(F32))
        cnt = _static_loop(n_tiles, tile, jnp.zeros((8, tq), F32))
        return jnp.sum(cnt, axis=0, keepdims=True)

    kf = float(k_sel)
    thr = jnp.where(count(lambda sk: sk >= 0) >= kf, 0, INT_MIN).astype(I32)

    def bit_step(b, thr):
        cand = thr + jnp.left_shift(jnp.int32(1), 30 - b)
        return jnp.where(count(lambda sk: sk >= cand) >= kf, cand, thr)

    thr = lax.fori_loop(0, 31, bit_step, thr)
    need = kf - count(lambda sk: sk > thr)

    r2 = lax.broadcasted_iota(I32, (tk, tk), 0)
    c2 = lax.broadcasted_iota(I32, (tk, tk), 1)
    upto = (c2 <= r2).astype(BF16)

    def tie_tile(j, seen):
        sk = skey_ref[tile_slice(j), :]
        tied = sk == thr
        prefix = _dot(upto, tied.astype(F32).astype(BF16)) + seen
        selected = ((sk > thr) | (tied & (prefix <= need))) & (j * tk + key_off <= q_pos)
        bias_ref[tile_slice(j), :] = jnp.where(selected, 0.0, MASKED)
        return prefix[tk - 1:tk, :]

    _static_loop(n_tiles, tie_tile, jnp.zeros((1, tq), F32))

    log2_scale = (DSA_LATENT ** -0.5) * LOG2_E

    def head_group(g, _):
        heads = [g * hg + u for u in range(hg)]
        offs = [pl.multiple_of(h * DSA_LATENT, DSA_LATENT) for h in heads]
        qg = jnp.concatenate(
            [jnp.concatenate([q_ref[0, :, pl.ds(off, DSA_LATENT)],
                              jnp.broadcast_to(qx_ref[pl.ds(h, 1), :], (tq, LANES)).astype(BF16)], axis=1)
             for h, off in zip(heads, offs)], axis=0)

        def logits_tile(j, m):
            d = _nt_dot(cn_ref[tile_slice(j), :], qg)
            b = bias_ref[tile_slice(j), :]
            tops = []
            for u in range(hg):
                s = d[:, u * tq:(u + 1) * tq] * log2_scale + b
                s_ref[tile_slice(j), u * tq:(u + 1) * tq] = s
                tops.append(jnp.max(_tree_max_rows(s), axis=0, keepdims=True))
            return jnp.maximum(m, jnp.concatenate(tops, axis=1))

        m = _static_loop(n_tiles, logits_tile, jnp.full((1, hg * tq), MASKED, F32))

        def value_tile(j, acc):
            p = jnp.exp2((s_ref[tile_slice(j), :] - m).astype(BF16))
            return acc + _dot(cnt_ref[:, tile_slice(j)], p)

        acc = _static_loop(n_tiles, value_tile, jnp.zeros((DSA_LATENT + BF16_ROWS, hg * tq), F32))
        out = (acc[:DSA_LATENT] / acc[DSA_LATENT:DSA_LATENT + 1]).astype(BF16)
        for u in range(0, hg, 2):
            pair = jnp.concatenate([out[:, u * tq:(u + 1) * tq], out[:, (u + 1) * tq:(u + 2) * tq]], axis=0)
            up = _dot(wup_ref[(g * hg + u) // 2], pair)
            off = pl.multiple_of((g * hg + u) * DSA_VDIM, 2 * DSA_VDIM)
            o_ref[0, :, pl.ds(off, 2 * DSA_VDIM)] = up.T.astype(o_ref.dtype)
        return 0

    lax.fori_loop(0, DSA_HEADS // hg, head_group, 0)


def _bf16_parts(x, n):
    parts = []
    for _ in range(n):
        p = x.astype(BF16).astype(F32)
        parts.append(p)
        x = x - p
    return parts


def _dsa_attention(proj, small, kvw, w_uv, bsz, seq):
    tq = min(TQ_DSA, seq)
    tk = min(TRI, seq)
    k_sel = min(TOPK_MAX, seq // 4)
    qw = DSA_HEADS * DSA_LATENT
    qiw = IDX_HEADS * IDX_DIM
    slope = _alibi_slopes(DSA_HEADS) / (DSA_LATENT ** -0.5)
    s_parts = _bf16_parts(slope, 3)
    pos = jnp.arange(seq, dtype=I32)
    p_parts = [(pos // LANES * LANES).astype(F32), (pos % LANES).astype(F32)]
    qx = jnp.stack([sp for _ in p_parts for sp in s_parts], axis=1)
    kx = jnp.stack([pp for pp in p_parts for _ in s_parts], axis=1)
    qx = jnp.pad(qx, ((0, 0), (0, LANES - qx.shape[1])))
    kx = jnp.pad(kx, ((0, 0), (0, LANES - kx.shape[1]))).astype(BF16)
    wt = jnp.swapaxes(w_uv, 1, 2).reshape(DSA_HEADS // 2, 2, DSA_VDIM, DSA_LATENT)
    zero = jnp.zeros_like(wt[:, 0])
    wup = jnp.concatenate([jnp.concatenate([wt[:, 0], zero], axis=2),
                           jnp.concatenate([zero, wt[:, 1]], axis=2)], axis=1).astype(BF16)
    return pl.pallas_call(
        functools.partial(_dsa_kernel, tq=tq, tk=tk, k_sel=k_sel, hg=HG_DSA),
        grid=(bsz, seq // tq),
        in_specs=[pl.BlockSpec((1, tq, qw), lambda b, i: (b, i, 0)),
                  pl.BlockSpec((1, seq, DSA_LATENT), lambda b, i: (b, 0, 0)),
                  pl.BlockSpec((1, DSA_LATENT), lambda b, i: (0, 0)),
                  pl.BlockSpec((1, tq, qiw), lambda b, i: (b, i, qw // qiw)),
                  pl.BlockSpec((1, seq, LANES), lambda b, i: (b, 0, (qw + qiw) // LANES)),
                  pl.BlockSpec((1, tq, LANES), lambda b, i: (b, i, 1)),
                  pl.BlockSpec((DSA_HEADS, LANES), lambda b, i: (0, 0)),
                  pl.BlockSpec((seq, LANES), lambda b, i: (0, 0)),
                  pl.BlockSpec(wup.shape, lambda b, i: (0, 0, 0))],
        out_specs=pl.BlockSpec((1, tq, DSA_HEADS * DSA_VDIM), lambda b, i: (b, i, 0)),
        out_shape=jax.ShapeDtypeStruct((bsz, seq, DSA_HEADS * DSA_VDIM), BF16),
        scratch_shapes=[pltpu.VMEM((seq, DSA_LATENT + LANES), BF16),
                        pltpu.VMEM((DSA_LATENT + BF16_ROWS, seq), BF16),
                        pltpu.VMEM((seq, tq), I32), pltpu.VMEM((seq, tq), F32),
                        pltpu.VMEM((seq, HG_DSA * tq), F32)],
        compiler_params=_cparams("parallel", "arbitrary"),
        name="dsa_attn",
    )(proj, small, kvw, proj, proj, small, qx, kx, wup)


def _mix_tail(mix, h_ref, g_ref, b_ref, rw_ref, rb_ref, hout_ref, logit_ref):
    hn = _layer_norm(DEEPNORM_ALPHA * h_ref[...] + mix, g_ref[...], b_ref[...])
    hout_ref[...] = hn
    logit_ref[...] = _dot(hn.astype(BF16), rw_ref[...]) + rb_ref[...]


def _mix_even_kernel(oa_ref, ob_ref, wa_ref, wb_ref, h_ref, g_ref, b_ref, rw_ref, rb_ref, hout_ref, logit_ref):
    mix = _dot(oa_ref[...], wa_ref[...]) + _dot(ob_ref[...], wb_ref[...])
    _mix_tail(mix, h_ref, g_ref, b_ref, rw_ref, rb_ref, hout_ref, logit_ref)


def _mix_odd_kernel(o_ref, wo_ref, h_ref, g_ref, b_ref, rw_ref, rb_ref, hout_ref, logit_ref):
    _mix_tail(_dot(o_ref[...], wo_ref[...]), h_ref, g_ref, b_ref, rw_ref, rb_ref, hout_ref, logit_ref)


def _mix_call(kernel, name, acts, weights, h, g, b, rw, rb):
    t, d = h.shape
    tm = min(TM_PROJ, t)
    row = lambda n: pl.BlockSpec((tm, n), lambda i: (i, 0))
    full = lambda a: pl.BlockSpec(a.shape, lambda i: (0, 0))
    return pl.pallas_call(
        kernel,
        grid=(t // tm,),
        in_specs=[row(a.shape[1]) for a in acts] + [full(w) for w in weights]
                 + [row(d), full(g), full(b), full(rw), full(rb)],
        out_specs=[row(d), row(LANES)],
        out_shape=[jax.ShapeDtypeStruct((t, d), F32), jax.ShapeDtypeStruct((t, LANES), F32)],
        compiler_params=_cparams("parallel"),
        name=name,
    )(*acts, *weights, h, g, b, rw, rb)


def _route_kernel(logit_ref, idx_ref, gate_ref, rank_ref, cnt_ref, *, tm):
    @pl.when(pl.program_id(0) == 0)
    def _():
        cnt_ref[...] = jnp.zeros_like(cnt_ref)

    lane = lax.broadcasted_iota(I32, (tm, LANES), 1)
    lane_f = lane.astype(F32)
    x = jnp.where(lane < N_EXPERTS, logit_ref[...], -jnp.inf)
    vals, hots = [], []
    for k in range(TOP_K):
        m = jnp.max(x, axis=-1, keepdims=True)
        first = jnp.min(jnp.where(x == m, lane_f, float(LANES)), axis=-1, keepdims=True)
        hot = lane_f == first
        x = jnp.where(hot, -jnp.inf, x)
        vals.append(m)
        hots.append(hot)
        idx_ref[:, k:k + 1] = first.astype(I32)
    exps = [jnp.exp(v - vals[0]) for v in vals]
    denom = exps[0] + exps[1] + exps[2] + exps[3]
    for k in range(TOP_K):
        gate_ref[:, k:k + 1] = exps[k] / denom

    hot_sum = (hots[0] | hots[1] | hots[2] | hots[3]).astype(F32)
    r = lax.broadcasted_iota(I32, (tm, tm), 0)
    c = lax.broadcasted_iota(I32, (tm, tm), 1)
    earlier = (c < r).astype(BF16)
    before = _dot(earlier, hot_sum.astype(BF16)) + cnt_ref[...]
    for k in range(TOP_K):
        rank_ref[:, k:k + 1] = jnp.sum(jnp.where(hots[k], before, 0.0), axis=-1, keepdims=True).astype(I32)
    cnt_ref[...] += jnp.sum(hot_sum, axis=0, keepdims=True)


def _route(logits):
    t = logits.shape[0]
    tm = min(TM_ROUTE, t)
    narrow = pl.BlockSpec((tm, TOP_K), lambda i: (i, 0))
    return pl.pallas_call(
        functools.partial(_route_kernel, tm=tm),
        grid=(t // tm,),
        in_specs=[pl.BlockSpec((tm, LANES), lambda i: (i, 0))],
        out_specs=[narrow, narrow, narrow, pl.BlockSpec((1, LANES), lambda i: (0, 0))],
        out_shape=[jax.ShapeDtypeStruct((t, TOP_K), I32), jax.ShapeDtypeStruct((t, TOP_K), F32),
                   jax.ShapeDtypeStruct((t, TOP_K), I32), jax.ShapeDtypeStruct((1, LANES), F32)],
        compiler_params=_cparams("arbitrary"),
        name="route",
    )(logits)


def _row_copy(src_ref, src_row, dst_ref, dst_row, sem):
    return pltpu.make_async_copy(src_ref.at[pl.ds(src_row, 1), :], dst_ref.at[pl.ds(dst_row, 1), :], sem)


def _dispatch_kernel(pad_start_ref, pad_count_ref, n_used_ref, pos_ref, x_ref, xs_out, zero_ref, sem, pad_sem,
                     *, td):
    @pl.when(pl.program_id(0) == 0)
    def _():
        zero_ref[...] = jnp.zeros_like(zero_ref)

        def fill_for(start_not_wait):
            def fill(present, n_rows, off):
                copy = pltpu.make_async_copy(zero_ref.at[pl.ds(0, n_rows), :], xs_out.at[pl.ds(off, n_rows), :],
                                             pad_sem)

                @pl.when(jnp.logical_and(present, start_not_wait))
                def _():
                    copy.start()

                @pl.when(jnp.logical_and(present, jnp.logical_not(start_not_wait)))
                def _():
                    copy.wait()
            return fill

        def pads(e, start_not_wait):
            fill = fill_for(start_not_wait)
            off = pad_start_ref[e]
            count = pad_count_ref[e]
            head = jnp.minimum(count, (-off) & (SUBLANES - 1))
            for r in range(SUBLANES - 1):
                fill(r < head, 1, off + r)
            off = off + head
            count = count - head
            chunk = zero_ref.shape[0] // 2
            while chunk >= SUBLANES:
                present = (count & chunk) != 0
                fill(present, chunk, pl.multiple_of(off, SUBLANES))
                off = off + jnp.where(present, chunk, 0)
                chunk //= 2
            for r in range(SUBLANES - 1):
                fill(r < (count & (SUBLANES - 1)), 1, off + r)
            return start_not_wait

        def tail(j, start_not_wait):
            tile = zero_ref.shape[0]
            fill_for(start_not_wait)(j >= n_used_ref[0], tile, pl.multiple_of(j * tile, tile))
            return start_not_wait

        n_tiles = xs_out.shape[0] // zero_ref.shape[0]
        lax.fori_loop(0, N_EXPERTS, pads, True)
        lax.fori_loop(0, n_tiles, tail, True)
        lax.fori_loop(0, N_EXPERTS, pads, False)
        lax.fori_loop(0, n_tiles, tail, False)

    def issue(i, _):
        for k in range(TOP_K):
            _row_copy(x_ref, i, xs_out, pos_ref[i * TOP_K + k], sem).start()
        return 0

    lax.fori_loop(0, td, issue, 0)

    def drain(i, _):
        for k in range(TOP_K):
            _row_copy(x_ref, i, xs_out, pos_ref[i * TOP_K + k], sem).wait()
        return 0

    lax.fori_loop(0, td, drain, 0)


def _dispatch(pad_start, pad_count, n_used, pos_flat, x, n_rows):
    t, d = x.shape
    td = min(TD_DISPATCH, t)
    grid_spec = pltpu.PrefetchScalarGridSpec(
        num_scalar_prefetch=3,
        grid=(t // td,),
        in_specs=[pl.BlockSpec((td * TOP_K,), lambda i, ps, pc, nu: (i,), memory_space=pltpu.SMEM),
                  pl.BlockSpec((td, d), lambda i, ps, pc, nu: (i, 0))],
        out_specs=pl.BlockSpec(memory_space=pl.ANY),
        scratch_shapes=[pltpu.VMEM((TM_FFN, d), x.dtype), pltpu.SemaphoreType.DMA, pltpu.SemaphoreType.DMA],
    )
    return pl.pallas_call(
        functools.partial(_dispatch_kernel, td=td),
        grid_spec=grid_spec,
        out_shape=jax.ShapeDtypeStruct((n_rows, d), x.dtype),
        compiler_params=_cparams("arbitrary"),
        name="moe_dispatch",
    )(pad_start, pad_count, n_used, pos_flat, x)


def _ffn_kernel(tile_expert_ref, n_used_ref, xs_ref, wgu_ref, bgu_ref, wd_ref, bd_ref, y_ref, wgu_bf, wd_bf):
    i = pl.program_id(0)
    d_ff = wd_ref.shape[2]
    new_expert = jnp.logical_or(i == 0, tile_expert_ref[i] != tile_expert_ref[jnp.maximum(i - 1, 0)])

    @pl.when(jnp.logical_and(new_expert, i < n_used_ref[0]))
    def _():
        wgu_bf[...] = wgu_ref[0, 0].astype(BF16)
        wd_bf[...] = wd_ref[0, 0].astype(BF16)

    @pl.when(i < n_used_ref[0])
    def _():
        x = xs_ref[...].astype(BF16)
        hgu = _dot(x, wgu_bf[...]) + bgu_ref[0, 0]
        gate = jnp.minimum(hgu[:, :d_ff], SWIGLU_LIMIT)
        up = jnp.clip(hgu[:, d_ff:], -SWIGLU_LIMIT, SWIGLU_LIMIT)
        act = gate * (1.0 / (1.0 + jnp.exp(-SWIGLU_ALPHA * gate))) * (up + 1.0)
        y_ref[...] = _dot(act.astype(BF16), wd_bf[...]) + bd_ref[0, 0]

    @pl.when(i >= n_used_ref[0])
    def _():
        y_ref[...] = jnp.zeros_like(y_ref)


def _expert_ffn(tile_expert, n_used, xs, w_gu, b_gu, w_down, b_down, layer):
    n_rows, d = xs.shape
    depth, e, _, f2 = w_gu.shape
    d_ff = w_down.shape[2]
    tm = TM_FFN
    grid_spec = pltpu.PrefetchScalarGridSpec(
        num_scalar_prefetch=2,
        grid=(n_rows // tm,),
        in_specs=[pl.BlockSpec((tm, d), lambda i, te, nu: (jnp.minimum(i, nu[0] - 1), 0)),
                  pl.BlockSpec((1, 1, d, f2), lambda i, te, nu: (layer, te[i], 0, 0)),
                  pl.BlockSpec((1, 1, 1, f2), lambda i, te, nu: (layer, te[i], 0, 0)),
                  pl.BlockSpec((1, 1, d_ff, d), lambda i, te, nu: (layer, te[i], 0, 0)),
                  pl.BlockSpec((1, 1, 1, d), lambda i, te, nu: (layer, te[i], 0, 0))],
        out_specs=pl.BlockSpec((tm, d), lambda i, te, nu: (i, 0)),
        scratch_shapes=[pltpu.VMEM((d, f2), BF16), pltpu.VMEM((d_ff, d), BF16)],
    )
    return pl.pallas_call(
        _ffn_kernel,
        grid_spec=grid_spec,
        out_shape=jax.ShapeDtypeStruct((n_rows, d), F32),
        compiler_params=_cparams("arbitrary"),
        name="expert_ffn",
    )(tile_expert, n_used, xs, w_gu, b_gu.reshape(depth, e, 1, f2), w_down, b_down.reshape(depth, e, 1, d))


def _combine_kernel(pos_ref, gate_ref, h_ref, g_ref, b_ref, y_hbm, out_ref, buf, sem, *, tc):
    def issue(i, _):
        for k in range(TOP_K):
            _row_copy(y_hbm, pos_ref[i * TOP_K + k], buf.at[k], i, sem).start()
        return 0

    lax.fori_loop(0, tc, issue, 0)

    def drain(i, _):
        for k in range(TOP_K):
            _row_copy(y_hbm, pos_ref[i * TOP_K + k], buf.at[k], i, sem).wait()
        return 0

    lax.fori_loop(0, tc, drain, 0)

    gates = gate_ref[...]
    ffn = gates[:, 0:1] * buf[0]
    for k in range(1, TOP_K):
        ffn = ffn + gates[:, k:k + 1] * buf[k]
    out_ref[...] = _layer_norm(DEEPNORM_ALPHA * h_ref[...] + ffn, g_ref[...], b_ref[...])


def _combine(pos_flat, gates, h, g, b, y):
    t, d = h.shape
    tc = min(TC_COMBINE, t)
    return pl.pallas_call(
        functools.partial(_combine_kernel, tc=tc),
        grid=(t // tc,),
        in_specs=[pl.BlockSpec((tc * TOP_K,), lambda i: (i,), memory_space=pltpu.SMEM),
                  pl.BlockSpec((tc, TOP_K), lambda i: (i, 0)),
                  pl.BlockSpec((tc, d), lambda i: (i, 0)),
                  pl.BlockSpec((1, d), lambda i: (0, 0)),
                  pl.BlockSpec((1, d), lambda i: (0, 0)),
                  pl.BlockSpec(memory_space=pl.ANY)],
        out_specs=pl.BlockSpec((tc, d), lambda i: (i, 0)),
        out_shape=jax.ShapeDtypeStruct((t, d), F32),
        scratch_shapes=[pltpu.VMEM((TOP_K, tc, d), F32), pltpu.SemaphoreType.DMA],
        compiler_params=_cparams("arbitrary"),
        name="moe_combine",
    )(pos_flat, gates, h, g, b, y)


def _moe(h, logits, w_gu, b_gu, w_down, b_down, g, b, layer):
    t = h.shape[0]
    idx, gates, rank, counts = _route(logits)
    counts = counts[0, :N_EXPERTS].astype(I32)
    tiles = (counts + TM_FFN - 1) // TM_FFN
    tile_end = jnp.cumsum(tiles)
    start = (tile_end - tiles) * TM_FFN
    n_tiles = (t * TOP_K) // TM_FFN + N_EXPERTS
    tile_ids = jnp.arange(n_tiles, dtype=I32)
    tile_expert = jnp.minimum(jnp.sum((tile_end[None, :] <= tile_ids[:, None]).astype(I32), axis=1), N_EXPERTS - 1)
    pos = (start[idx] + rank).reshape(-1).astype(I32)
    n_used = tile_end[-1:].astype(I32)
    xs = _dispatch(start + counts, tiles * TM_FFN - counts, n_used, pos, h, n_tiles * TM_FFN)
    y = _expert_ffn(tile_expert, n_used, xs, w_gu, b_gu, w_down, b_down, layer)
    return _combine(pos, gates, h, g, b, y)


def _router_params(router_w, router_b):
    rw = jnp.pad(router_w, ((0, 0), (0, LANES - N_EXPERTS))).astype(BF16)
    rb = jnp.pad(router_b, (0, LANES - N_EXPERTS)).reshape(1, LANES)
    return rw, rb


def _alibi_slopes(n):
    return 2.0 ** (-8.0 * jnp.arange(1, n + 1, dtype=F32) / n)


def _even_layer(h, bsz, seq, w_in, w_out, lq1, lk1, lq2, lk2, subw, layer, g, b, rw, rb):
    proj = _project(h, [w_in.astype(BF16)], [BF16])[0].reshape(bsz, seq, -1)
    o_a = _sb_attention(proj, bsz, seq)
    lam_init = 0.8 - 0.6 * math.exp(-0.3 * layer)
    row = lambda v: v.reshape(1, -1)
    o_b = _diff_attention(proj, _alibi_slopes(DIFF_HEADS), row(lq1), row(lk1), row(lq2), row(lk2), row(subw),
                          bsz, seq, lam_init)
    t = bsz * seq
    sbw = SB_HEADS * HEAD_DIM
    wo = w_out.astype(BF16)
    return _mix_call(_mix_even_kernel, "mix_even", [o_a.reshape(t, -1), o_b.reshape(t, -1)],
                     [wo[:sbw], wo[sbw:]], h, row(g), row(b), rw, rb)


def _odd_layer(h, bsz, seq, w_in, kvw, w_uv, w_out, g, b, rw, rb):
    qw = DSA_HEADS * DSA_LATENT
    c0, c1 = qw, qw + DSA_LATENT
    i0, i1 = c1, c1 + IDX_HEADS * IDX_DIM
    k1 = i1 + IDX_DIM
    w_main = jnp.concatenate([w_in[:, :qw], w_in[:, i0:i1], w_in[:, i1:k1], w_in[:, i1:k1]], axis=1).astype(BF16)
    w_small = jnp.pad(jnp.concatenate([w_in[:, c0:c1], w_in[:, k1:]], axis=1),
                      ((0, 0), (0, LANES - IDX_HEADS))).astype(BF16)
    proj, small = [p.reshape(bsz, seq, -1) for p in _project(h, [w_main, w_small], [BF16, F32])]
    o = _dsa_attention(proj, small, kvw.reshape(1, -1), w_uv, bsz, seq)
    row = lambda v: v.reshape(1, -1)
    return _mix_call(_mix_odd_kernel, "mix_odd", [o.reshape(bsz * seq, -1)], [w_out.astype(BF16)],
                     h, row(g), row(b), rw, rb)


def kernel(x, ev_w_in, ev_w_out, ev_lambda_q1, ev_lambda_k1, ev_lambda_q2, ev_lambda_k2, ev_subln_w, od_w_in, od_kv_norm_w, od_w_uv, od_w_out, ln_mix_g, ln_mix_b, router_w, router_b, exp_w_gu, exp_b_gu, exp_w_down, exp_b_down, ln_ffn_g, ln_ffn_b):
    bsz, seq, d = x.shape
    h = x.reshape(bsz * seq, d)
    for layer in range(ln_mix_g.shape[0]):
        j = layer // 2
        rw, rb = _router_params(router_w[layer], router_b[layer])
        if layer % 2 == 0:
            h, logits = _even_layer(h, bsz, seq, ev_w_in[j], ev_w_out[j], ev_lambda_q1[j], ev_lambda_k1[j],
                                    ev_lambda_q2[j], ev_lambda_k2[j], ev_subln_w[j], layer,
                                    ln_mix_g[layer], ln_mix_b[layer], rw, rb)
        else:
            h, logits = _odd_layer(h, bsz, seq, od_w_in[j], od_kv_norm_w[j], od_w_uv[j], od_w_out[j],
                                   ln_mix_g[layer], ln_mix_b[layer], rw, rb)
        h = _moe(h, logits, exp_w_gu, exp_b_gu, exp_w_down, exp_b_down,
                 ln_ffn_g[layer].reshape(1, -1), ln_ffn_b[layer].reshape(1, -1), layer)
    return h.reshape(bsz, seq, d)
```

```python
import functools
import math

import jax
import jax.numpy as jnp
from jax import lax
from jax.experimental import pallas as pl
from jax.experimental.pallas import tpu as pltpu

F32, BF16, I32 = jnp.float32, jnp.bfloat16, jnp.int32

HEAD_DIM = 64
SB_HEADS = 8
DIFF_HEADS = 4
DIFF_SUB = 64
DIFF_VDIM = 128
DSA_HEADS = 16
DSA_LATENT = 128
DSA_VDIM = 64
IDX_HEADS = 8
IDX_DIM = 64
TOPK_MAX = 256
N_EXPERTS = 32
TOP_K = 4
SWIGLU_LIMIT = 7.0
SWIGLU_ALPHA = 1.702
LN_EPS = 1e-5
RMS_EPS = 1e-5
DEPTH = 2
DEEPNORM_ALPHA = (2 * DEPTH) ** 0.25

LANES = 128
SUBLANES = 8
BF16_ROWS = 16
VMEM_LIMIT = 56 * 1024 * 1024
INT_MIN = -(2 ** 31)
MASKED = -1e30
DEAD_LOG2 = -152.0
LOG2_E = math.log2(math.e)

TM_PROJ = 512
TQ_ATT = 256
TQ_DSA = 128
TRI = 512
HG_DSA = 16
TM_ROUTE = 512
TM_FFN = 512
TD_DISPATCH = 512
TC_COMBINE = 256


def _cparams(*sem):
    return pltpu.CompilerParams(dimension_semantics=sem, vmem_limit_bytes=VMEM_LIMIT)


def _nt_dot(a, b):
    return lax.dot_general(a, b, (((1,), (1,)), ((), ())), preferred_element_type=F32)


def _dot(a, b):
    return jnp.dot(a, b, preferred_element_type=F32)


def _layer_norm(y, g, b):
    mu = jnp.mean(y, axis=-1, keepdims=True)
    d = y - mu
    var = jnp.mean(d * d, axis=-1, keepdims=True)
    return d * lax.rsqrt(var + LN_EPS) * g + b


def _col_chunk(n):
    for c in (512, 384, 256, 128):
        if n % c == 0:
            return c
    raise ValueError(n)


def _proj_kernel(x_ref, *refs):
    n_out = len(refs) // 2
    x = x_ref[...].astype(BF16)
    for w_ref, o_ref in zip(refs[:n_out], refs[n_out:]):
        n = o_ref.shape[-1]
        c = _col_chunk(n)
        for j in range(0, n, c):
            o_ref[:, j:j + c] = _dot(x, w_ref[:, j:j + c]).astype(o_ref.dtype)


def _project(x, weights, out_dtypes):
    t, k = x.shape
    tm = min(TM_PROJ, t)
    return pl.pallas_call(
        _proj_kernel,
        grid=(t // tm,),
        in_specs=[pl.BlockSpec((tm, k), lambda i: (i, 0))]
                 + [pl.BlockSpec(w.shape, lambda i: (0, 0)) for w in weights],
        out_specs=[pl.BlockSpec((tm, w.shape[1]), lambda i: (i, 0)) for w in weights],
        out_shape=[jax.ShapeDtypeStruct((t, w.shape[1]), dt) for w, dt in zip(weights, out_dtypes)],
        compiler_params=_cparams("parallel"),
        name="in_proj",
    )(x, *weights)


def _stack_halves(x, lane):
    zero = jnp.zeros_like(x)
    return jnp.concatenate([jnp.where(lane < HEAD_DIM, x, zero), jnp.where(lane >= HEAD_DIM, x, zero)], axis=0)


def _sb_kernel(q_ref, k_ref, v_ref, o_ref, *, tq):
    i = pl.program_id(2)
    scale = HEAD_DIM ** -0.5
    lane = lax.broadcasted_iota(I32, (1, LANES), 1)
    row = lax.broadcasted_iota(I32, (2 * tq, tq), 0)
    row = jnp.where(row >= tq, row - tq, row)
    col = lax.broadcasted_iota(I32, (2 * tq, tq), 1)
    strict = col < row
    r1 = lax.broadcasted_iota(I32, (tq, tq), 0)
    c1 = lax.broadcasted_iota(I32, (tq, tq), 1)
    after = (r1 > c1).astype(BF16)
    q2 = _stack_halves(q_ref[0], lane)

    def block(j, carry, masked):
        run, acc = carry
        start = pl.multiple_of(j * tq, tq)
        kb = k_ref[0, pl.ds(start, tq), :]
        vb = v_ref[0, pl.ds(start, tq), :]
        z2 = _nt_dot(q2, kb) * (scale * LOG2_E)
        neg = -z2
        keep2 = jnp.minimum(neg, 0.0) - jnp.log2(1.0 + jnp.exp2(jnp.minimum(z2, neg)))
        lk = jnp.where(strict, keep2, 0.0) if masked else keep2
        hi = lk.astype(BF16)
        lo = (lk - hi.astype(F32)).astype(BF16)
        suffix = _dot(hi, after) + _dot(lo, after)
        w = jnp.exp2(z2 + keep2 + suffix + run)
        if masked:
            w = jnp.where(strict, w, 0.0)
        wb = w.astype(BF16)
        zero = jnp.zeros_like(vb)
        acc = (acc + _dot(wb[:tq], jnp.where(lane < HEAD_DIM, vb, zero))
               + _dot(wb[tq:], jnp.where(lane >= HEAD_DIM, vb, zero)))
        run = run + suffix[:, 0:1] + lk[:, 0:1]
        return run, acc

    def alive(c):
        return jnp.logical_and(c[0] >= 0, c[1] > DEAD_LOG2)

    def step(c):
        j, _, run, acc = c
        run, acc = block(j, (run, acc), False)
        return j - 1, jnp.max(run), run, acc

    start = (jnp.zeros((2 * tq, 1), F32), jnp.zeros((tq, LANES), F32))

    @pl.when(i == 0)
    def _():
        o_ref[0] = block(0, start, True)[1].astype(o_ref.dtype)

    @pl.when(i > 0)
    def _():
        run, acc = block(i - 1, block(i, start, True), False)
        _, _, _, acc = lax.while_loop(alive, step, (i - 2, jnp.max(run), run, acc))
        o_ref[0] = acc.astype(o_ref.dtype)


def _sb_attention(proj, bsz, seq):
    tq = min(TQ_ATT, seq)
    pairs = SB_HEADS * HEAD_DIM // LANES
    return pl.pallas_call(
        functools.partial(_sb_kernel, tq=tq),
        grid=(bsz, pairs, seq // tq),
        in_specs=[pl.BlockSpec((1, tq, LANES), lambda b, p, i: (b, i, p)),
                  pl.BlockSpec((1, seq, LANES), lambda b, p, i: (b, 0, pairs + p)),
                  pl.BlockSpec((1, seq, LANES), lambda b, p, i: (b, 0, 2 * pairs + p))],
        out_specs=pl.BlockSpec((1, tq, LANES), lambda b, p, i: (b, i, p)),
        out_shape=jax.ShapeDtypeStruct((bsz, seq, SB_HEADS * HEAD_DIM), BF16),
        compiler_params=_cparams("parallel", "parallel", "parallel"),
        name="stickbreak_attn",
    )(proj, proj, proj)


def _diff_kernel(slope_ref, lq1_ref, lk1_ref, lq2_ref, lk2_ref, subw_ref, q_ref, k_ref, v_ref, o_ref,
                 *, tq, lam_init):
    h = pl.program_id(1)
    i = pl.program_id(2)
    scale = DIFF_SUB ** -0.5
    slope = slope_ref[h]
    lam = (jnp.exp(jnp.sum(lq1_ref[...] * lk1_ref[...], keepdims=True))
           - jnp.exp(jnp.sum(lq2_ref[...] * lk2_ref[...], keepdims=True)) + lam_init)
    lane = lax.broadcasted_iota(I32, (1, LANES), 1)
    row = lax.broadcasted_iota(I32, (2 * tq, tq), 0)
    row = jnp.where(row >= tq, row - tq, row)
    col = lax.broadcasted_iota(I32, (2 * tq, tq), 1)
    causal = col <= row
    key_off = lax.broadcasted_iota(I32, (1, tq), 1)
    q2 = _stack_halves(q_ref[0], lane)

    def block(j, carry, masked):
        m, l, acc = carry
        start = j * tq
        kb = k_ref[0, pl.ds(start, tq), :]
        vb = v_ref[0, pl.ds(start, tq), :]
        s = _nt_dot(q2, kb) * (scale * LOG2_E) + (slope * LOG2_E) * (start + key_off).astype(F32)
        if masked:
            s = jnp.where(causal, s, -jnp.inf)
        m_new = jnp.maximum(m, jnp.max(s, axis=-1, keepdims=True))
        a = jnp.exp2(m - m_new)
        p = jnp.exp2(s - m_new)
        l = a * l + jnp.sum(p, axis=-1, keepdims=True)
        acc = a * acc + _dot(p.astype(BF16), vb)
        return m_new, l, acc

    for n_left in range(k_ref.shape[1] // tq):
        @pl.when(i == n_left)
        def _(n_left=n_left):
            carry = (jnp.full((2 * tq, 1), -jnp.inf, F32), jnp.zeros((2 * tq, 1), F32),
                     jnp.zeros((2 * tq, LANES), F32))
            carry = block(n_left, carry, True)
            for j in range(n_left):
                carry = block(j, carry, False)
            out = carry[2] / carry[1]
            o = out[:tq] - lam * out[tq:]
            o = o * lax.rsqrt(jnp.mean(o * o, axis=-1, keepdims=True) + RMS_EPS) * subw_ref[...]
            o_ref[0] = (o * (1.0 - lam_init)).astype(o_ref.dtype)


def _diff_attention(proj, slopes, lq1, lk1, lq2, lk2, subw, bsz, seq, lam_init):
    tq = min(TQ_ATT, seq)
    q0 = 3 * SB_HEADS * HEAD_DIM // LANES
    k0 = q0 + DIFF_HEADS
    v0 = k0 + DIFF_HEADS
    vec = lambda n: pl.BlockSpec((1, n), lambda b, h, i: (0, 0))
    return pl.pallas_call(
        functools.partial(_diff_kernel, tq=tq, lam_init=lam_init),
        grid=(bsz, DIFF_HEADS, seq // tq),
        in_specs=[pl.BlockSpec(memory_space=pltpu.SMEM),
                  vec(DIFF_SUB), vec(DIFF_SUB), vec(DIFF_SUB), vec(DIFF_SUB), vec(DIFF_VDIM),
                  pl.BlockSpec((1, tq, LANES), lambda b, h, i: (b, i, q0 + h)),
                  pl.BlockSpec((1, seq, LANES), lambda b, h, i: (b, 0, k0 + h)),
                  pl.BlockSpec((1, seq, LANES), lambda b, h, i: (b, 0, v0 + h))],
        out_specs=pl.BlockSpec((1, tq, LANES), lambda b, h, i: (b, i, h)),
        out_shape=jax.ShapeDtypeStruct((bsz, seq, DIFF_HEADS * DIFF_VDIM), BF16),
        compiler_params=_cparams("parallel", "parallel", "parallel"),
        name="diff_attn",
    )(slopes, lq1, lk1, lq2, lk2, subw, proj, proj, proj)


def _tree_sum_rows(x):
    n = x.shape[0] // 8
    x = x.reshape(n, 8, x.shape[1])
    while n > 1:
        n //= 2
        x = x[:n] + x[n:]
    return x[0]


def _tree_max_rows(x):
    n = x.shape[0] // 8
    x = x.reshape(n, 8, x.shape[1])
    while n > 1:
        n //= 2
        x = jnp.maximum(x[:n], x[n:])
    return x[0]


def _dsa_kernel(q_ref, ckv_ref, kvw_ref, qi_ref, ki_ref, wi_ref, qx_ref, kx_ref, wup_ref, o_ref,
                cn_ref, cnt_ref, skey_ref, bias_ref, s_ref, *, tq, tk, k_sel, hg):
    i = pl.program_id(1)
    t0 = i * tq

    @pl.when(i == 0)
    def _():
        c = ckv_ref[0]
        cn = c * lax.rsqrt(jnp.mean(c * c, axis=-1, keepdims=True) + RMS_EPS) * kvw_ref[...]
        cn_ref[:, :DSA_LATENT] = cn.astype(BF16)
        cn_ref[:, DSA_LATENT:] = kx_ref[...]
        cnt_ref[:DSA_LATENT, :] = cn.T.astype(BF16)
        row = lax.broadcasted_iota(I32, (BF16_ROWS, cnt_ref.shape[1]), 0)
        cnt_ref[DSA_LATENT:, :] = jnp.where(row == 0, 1.0, 0.0).astype(BF16)

    for k in range(ckv_ref.shape[1] // tk):
        @pl.when((t0 + tq - 1) // tk == k)
        def _(k=k):
            _dsa_block(q_ref, qi_ref, ki_ref, wi_ref, qx_ref, wup_ref, o_ref, cn_ref, cnt_ref, skey_ref, bias_ref,
                       s_ref, t0=t0, n_tiles=k + 1, tq=tq, tk=tk, k_sel=k_sel, hg=hg)


def _static_loop(n, body, carry):
    for j in range(n):
        carry = body(j, carry)
    return carry


def _dsa_block(q_ref, qi_ref, ki_ref, wi_ref, qx_ref, wup_ref, o_ref, cn_ref, cnt_ref, skey_ref, bias_ref, s_ref,
               *, t0, n_tiles, tq, tk, k_sel, hg):
    def tile_slice(j):
        return pl.ds(j * tk, tk)

    lane = lax.broadcasted_iota(I32, (1, LANES), 1)
    q_pos = t0 + lax.broadcasted_iota(I32, (tk, tq), 1)
    key_off = lax.broadcasted_iota(I32, (tk, tq), 0)

    w_t = (wi_ref[0] * ((IDX_HEADS ** -0.5) * (IDX_DIM ** -0.5))).T
    stacked = []
    for h in range(IDX_HEADS):
        qp = qi_ref[0, :, (h // 2) * LANES:(h // 2 + 1) * LANES]
        stacked.append(jnp.where((lane // IDX_DIM) == (h % 2), qp, jnp.zeros_like(qp)))
    q8 = jnp.concatenate(stacked, axis=0)

    def score_tile(j, _):
        d = _nt_dot(ki_ref[0, tile_slice(j), :], q8)
        score = jnp.zeros((tk, tq), F32)
        for h in range(IDX_HEADS):
            score = score + jnp.maximum(d[:, h * tq:(h + 1) * tq], 0.0) * w_t[h:h + 1, :]
        score = jnp.where(score == 0.0, 0.0, score)
        bits = pltpu.bitcast(score, I32)
        skey = jnp.where(bits < 0, bits ^ 0x7FFFFFFF, bits)
        skey_ref[tile_slice(j), :] = jnp.where(j * tk + key_off <= q_pos, skey, INT_MIN)
        return 0

    _static_loop(n_tiles, score_tile, 0)

    def count(pred):
        def tile(j, cnt):
            return cnt + _tree_sum_rows(pred(skey_ref[tile_slice(j), :]).astype(F32))
        cnt = _static_loop(n_tiles, tile, jnp.zeros((8, tq), F32))
        return jnp.sum(cnt, axis=0, keepdims=True)

    kf = float(k_sel)
    thr = jnp.where(count(lambda sk: sk >= 0) >= kf, 0, INT_MIN).astype(I32)

    def bit_step(b, thr):
        cand = thr + jnp.left_shift(jnp.int32(1), 30 - b)
        return jnp.where(count(lambda sk: sk >= cand) >= kf, cand, thr)

    thr = lax.fori_loop(0, 31, bit_step, thr)
    need = kf - count(lambda sk: sk > thr)

    r2 = lax.broadcasted_iota(I32, (tk, tk), 0)
    c2 = lax.broadcasted_iota(I32, (tk, tk), 1)
    upto = (c2 <= r2).astype(BF16)

    def tie_tile(j, seen):
        sk = skey_ref[tile_slice(j), :]
        tied = sk == thr
        prefix = _dot(upto, tied.astype(F32).astype(BF16)) + seen
        selected = ((sk > thr) | (tied & (prefix <= need))) & (j * tk + key_off <= q_pos)
        bias_ref[tile_slice(j), :] = jnp.where(selected, 0.0, MASKED)
        return prefix[tk - 1:tk, :]

    _static_loop(n_tiles, tie_tile, jnp.zeros((1, tq), F32))

    log2_scale = (DSA_LATENT ** -0.5) * LOG2_E

    def head_group(g, _):
        heads = [g * hg + u for u in range(hg)]
        offs = [pl.multiple_of(h * DSA_LATENT, DSA_LATENT) for h in heads]
        qg = jnp.concatenate(
            [jnp.concatenate([q_ref[0, :, pl.ds(off, DSA_LATENT)],
                              jnp.broadcast_to(qx_ref[pl.ds(h, 1), :], (tq, LANES)).astype(BF16)], axis=1)
             for h, off in zip(heads, offs)], axis=0)

        def logits_tile(j, m):
            d = _nt_dot(cn_ref[tile_slice(j), :], qg)
            b = bias_ref[tile_slice(j), :]
            tops = []
            for u in range(hg):
                s = d[:, u * tq:(u + 1) * tq] * log2_scale + b
                s_ref[tile_slice(j), u * tq:(u + 1) * tq] = s
                tops.append(jnp.max(_tree_max_rows(s), axis=0, keepdims=True))
            return jnp.maximum(m, jnp.concatenate(tops, axis=1))

        m = _static_loop(n_tiles, logits_tile, jnp.full((1, hg * tq), MASKED, F32))

        def value_tile(j, acc):
            p = jnp.exp2((s_ref[tile_slice(j), :] - m).astype(BF16))
            return acc + _dot(cnt_ref[:, tile_slice(j)], p)

        acc = _static_loop(n_tiles, value_tile, jnp.zeros((DSA_LATENT + BF16_ROWS, hg * tq), F32))
        out = (acc[:DSA_LATENT] / acc[DSA_LATENT:DSA_LATENT + 1]).astype(BF16)
        for u in range(0, hg, 2):
            pair = jnp.concatenate([out[:, u * tq:(u + 1) * tq], out[:, (u + 1) * tq:(u + 2) * tq]], axis=0)
            up = _dot(wup_ref[(g * hg + u) // 2], pair)
            off = pl.multiple_of((g * hg + u) * DSA_VDIM, 2 * DSA_VDIM)
            o_ref[0, :, pl.ds(off, 2 * DSA_VDIM)] = up.T.astype(o_ref.dtype)
        return 0

    lax.fori_loop(0, DSA_HEADS // hg, head_group, 0)


def _bf16_parts(x, n):
    parts = []
    for _ in range(n):
        p = x.astype(BF16).astype(F32)
        parts.append(p)
        x = x - p
    return parts


def _dsa_attention(proj, small, kvw, w_uv, bsz, seq):
    tq = min(TQ_DSA, seq)
    tk = min(TRI, seq)
    k_sel = min(TOPK_MAX, seq // 4)
    qw = DSA_HEADS * DSA_LATENT
    qiw = IDX_HEADS * IDX_DIM
    slope = _alibi_slopes(DSA_HEADS) / (DSA_LATENT ** -0.5)
    s_parts = _bf16_parts(slope, 3)
    pos = jnp.arange(seq, dtype=I32)
    p_parts = [(pos // LANES * LANES).astype(F32), (pos % LANES).astype(F32)]
    qx = jnp.stack([sp for _ in p_parts for sp in s_parts], axis=1)
    kx = jnp.stack([pp for pp in p_parts for _ in s_parts], axis=1)
    qx = jnp.pad(qx, ((0, 0), (0, LANES - qx.shape[1])))
    kx = jnp.pad(kx, ((0, 0), (0, LANES - kx.shape[1]))).astype(BF16)
    wt = jnp.swapaxes(w_uv, 1, 2).reshape(DSA_HEADS // 2, 2, DSA_VDIM, DSA_LATENT)
    zero = jnp.zeros_like(wt[:, 0])
    wup = jnp.concatenate([jnp.concatenate([wt[:, 0], zero], axis=2),
                           jnp.concatenate([zero, wt[:, 1]], axis=2)], axis=1).astype(BF16)
    return pl.pallas_call(
        functools.partial(_dsa_kernel, tq=tq, tk=tk, k_sel=k_sel, hg=HG_DSA),
        grid=(bsz, seq // tq),
        in_specs=[pl.BlockSpec((1, tq, qw), lambda b, i: (b, i, 0)),
                  pl.BlockSpec((1, seq, DSA_LATENT), lambda b, i: (b, 0, 0)),
                  pl.BlockSpec((1, DSA_LATENT), lambda b, i: (0, 0)),
                  pl.BlockSpec((1, tq, qiw), lambda b, i: (b, i, qw // qiw)),
                  pl.BlockSpec((1, seq, LANES), lambda b, i: (b, 0, (qw + qiw) // LANES)),
                  pl.BlockSpec((1, tq, LANES), lambda b, i: (b, i, 1)),
                  pl.BlockSpec((DSA_HEADS, LANES), lambda b, i: (0, 0)),
                  pl.BlockSpec((seq, LANES), lambda b, i: (0, 0)),
                  pl.BlockSpec(wup.shape, lambda b, i: (0, 0, 0))],
        out_specs=pl.BlockSpec((1, tq, DSA_HEADS * DSA_VDIM), lambda b, i: (b, i, 0)),
        out_shape=jax.ShapeDtypeStruct((bsz, seq, DSA_HEADS * DSA_VDIM), BF16),
        scratch_shapes=[pltpu.VMEM((seq, DSA_LATENT + LANES), BF16),
                        pltpu.VMEM((DSA_LATENT + BF16_ROWS, seq), BF16),
                        pltpu.VMEM((seq, tq), I32), pltpu.VMEM((seq, tq), F32),
                        pltpu.VMEM((seq, HG_DSA * tq), F32)],
        compiler_params=_cparams("parallel", "arbitrary"),
        name="dsa_attn",
    )(proj, small, kvw, proj, proj, small, qx, kx, wup)


def _mix_tail(mix, h_ref, g_ref, b_ref, rw_ref, rb_ref, hout_ref, logit_ref):
    hn = _layer_norm(DEEPNORM_ALPHA * h_ref[...] + mix, g_ref[...], b_ref[...])
    hout_ref[...] = hn
    logit_ref[...] = _dot(hn.astype(BF16), rw_ref[...]) + rb_ref[...]


def _mix_even_kernel(oa_ref, ob_ref, wa_ref, wb_ref, h_ref, g_ref, b_ref, rw_ref, rb_ref, hout_ref, logit_ref):
    mix = _dot(oa_ref[...], wa_ref[...]) + _dot(ob_ref[...], wb_ref[...])
    _mix_tail(mix, h_ref, g_ref, b_ref, rw_ref, rb_ref, hout_ref, logit_ref)


def _mix_odd_kernel(o_ref, wo_ref, h_ref, g_ref, b_ref, rw_ref, rb_ref, hout_ref, logit_ref):
    _mix_tail(_dot(o_ref[...], wo_ref[...]), h_ref, g_ref, b_ref, rw_ref, rb_ref, hout_ref, logit_ref)


def _mix_call(kernel, name, acts, weights, h, g, b, rw, rb):
    t, d = h.shape
    tm = min(TM_PROJ, t)
    row = lambda n: pl.BlockSpec((tm, n), lambda i: (i, 0))
    full = lambda a: pl.BlockSpec(a.shape, lambda i: (0, 0))
    return pl.pallas_call(
        kernel,
        grid=(t // tm,),
        in_specs=[row(a.shape[1]) for a in acts] + [full(w) for w in weights]
                 + [row(d), full(g), full(b), full(rw), full(rb)],
        out_specs=[row(d), row(LANES)],
        out_shape=[jax.ShapeDtypeStruct((t, d), F32), jax.ShapeDtypeStruct((t, LANES), F32)],
        compiler_params=_cparams("parallel"),
        name=name,
    )(*acts, *weights, h, g, b, rw, rb)


def _route_kernel(logit_ref, idx_ref, gate_ref, rank_ref, cnt_ref, *, tm):
    @pl.when(pl.program_id(0) == 0)
    def _():
        cnt_ref[...] = jnp.zeros_like(cnt_ref)

    lane = lax.broadcasted_iota(I32, (tm, LANES), 1)
    lane_f = lane.astype(F32)
    x = jnp.where(lane < N_EXPERTS, logit_ref[...], -jnp.inf)
    vals, hots = [], []
    for k in range(TOP_K):
        m = jnp.max(x, axis=-1, keepdims=True)
        first = jnp.min(jnp.where(x == m, lane_f, float(LANES)), axis=-1, keepdims=True)
        hot = lane_f == first
        x = jnp.where(hot, -jnp.inf, x)
        vals.append(m)
        hots.append(hot)
        idx_ref[:, k:k + 1] = first.astype(I32)
    exps = [jnp.exp(v - vals[0]) for v in vals]
    denom = exps[0] + exps[1] + exps[2] + exps[3]
    for k in range(TOP_K):
        gate_ref[:, k:k + 1] = exps[k] / denom

    hot_sum = (hots[0] | hots[1] | hots[2] | hots[3]).astype(F32)
    r = lax.broadcasted_iota(I32, (tm, tm), 0)
    c = lax.broadcasted_iota(I32, (tm, tm), 1)
    earlier = (c < r).astype(BF16)
    before = _dot(earlier, hot_sum.astype(BF16)) + cnt_ref[...]
    for k in range(TOP_K):
        rank_ref[:, k:k + 1] = jnp.sum(jnp.where(hots[k], before, 0.0), axis=-1, keepdims=True).astype(I32)
    cnt_ref[...] += jnp.sum(hot_sum, axis=0, keepdims=True)


def _route(logits):
    t = logits.shape[0]
    tm = min(TM_ROUTE, t)
    narrow = pl.BlockSpec((tm, TOP_K), lambda i: (i, 0))
    return pl.pallas_call(
        functools.partial(_route_kernel, tm=tm),
        grid=(t // tm,),
        in_specs=[pl.BlockSpec((tm, LANES), lambda i: (i, 0))],
        out_specs=[narrow, narrow, narrow, pl.BlockSpec((1, LANES), lambda i: (0, 0))],
        out_shape=[jax.ShapeDtypeStruct((t, TOP_K), I32), jax.ShapeDtypeStruct((t, TOP_K), F32),
                   jax.ShapeDtypeStruct((t, TOP_K), I32), jax.ShapeDtypeStruct((1, LANES), F32)],
        compiler_params=_cparams("arbitrary"),
        name="route",
    )(logits)


def _row_copy(src_ref, src_row, dst_ref, dst_row, sem):
    return pltpu.make_async_copy(src_ref.at[pl.ds(src_row, 1), :], dst_ref.at[pl.ds(dst_row, 1), :], sem)


def _dispatch_kernel(pad_start_ref, pad_count_ref, n_used_ref, pos_ref, x_ref, xs_out, zero_ref, sem, pad_sem,
                     *, td):
    @pl.when(pl.program_id(0) == 0)
    def _():
        zero_ref[...] = jnp.zeros_like(zero_ref)

        def fill_for(start_not_wait):
            def fill(present, n_rows, off):
                copy = pltpu.make_async_copy(zero_ref.at[pl.ds(0, n_rows), :], xs_out.at[pl.ds(off, n_rows), :],
                                             pad_sem)

                @pl.when(jnp.logical_and(present, start_not_wait))
                def _():
                    copy.start()

                @pl.when(jnp.logical_and(present, jnp.logical_not(start_not_wait)))
                def _():
                    copy.wait()
            return fill

        def pads(e, start_not_wait):
            fill = fill_for(start_not_wait)
            off = pad_start_ref[e]
            count = pad_count_ref[e]
            head = jnp.minimum(count, (-off) & (SUBLANES - 1))
            for r in range(SUBLANES - 1):
                fill(r < head, 1, off + r)
            off = off + head
            count = count - head
            chunk = zero_ref.shape[0] // 2
            while chunk >= SUBLANES:
                present = (count & chunk) != 0
                fill(present, chunk, pl.multiple_of(off, SUBLANES))
                off = off + jnp.where(present, chunk, 0)
                chunk //= 2
            for r in range(SUBLANES - 1):
                fill(r < (count & (SUBLANES - 1)), 1, off + r)
            return start_not_wait

        def tail(j, start_not_wait):
            tile = zero_ref.shape[0]
            fill_for(start_not_wait)(j >= n_used_ref[0], tile, pl.multiple_of(j * tile, tile))
            return start_not_wait

        n_tiles = xs_out.shape[0] // zero_ref.shape[0]
        lax.fori_loop(0, N_EXPERTS, pads, True)
        lax.fori_loop(0, n_tiles, tail, True)
        lax.fori_loop(0, N_EXPERTS, pads, False)
        lax.fori_loop(0, n_tiles, tail, False)

    def issue(i, _):
        for k in range(TOP_K):
            _row_copy(x_ref, i, xs_out, pos_ref[i * TOP_K + k], sem).start(priority=k % 2)
        return 0

    lax.fori_loop(0, td, issue, 0)

    def drain(i, _):
        for k in range(TOP_K):
            _row_copy(x_ref, i, xs_out, pos_ref[i * TOP_K + k], sem).wait()
        return 0

    lax.fori_loop(0, td, drain, 0)


def _dispatch(pad_start, pad_count, n_used, pos_flat, x, n_rows):
    t, d = x.shape
    td = min(TD_DISPATCH, t)
    grid_spec = pltpu.PrefetchScalarGridSpec(
        num_scalar_prefetch=3,
        grid=(t // td,),
        in_specs=[pl.BlockSpec((td * TOP_K,), lambda i, ps, pc, nu: (i,), memory_space=pltpu.SMEM),
                  pl.BlockSpec((td, d), lambda i, ps, pc, nu: (i, 0))],
        out_specs=pl.BlockSpec(memory_space=pl.ANY),
        scratch_shapes=[pltpu.VMEM((TM_FFN, d), x.dtype), pltpu.SemaphoreType.DMA, pltpu.SemaphoreType.DMA],
    )
    return pl.pallas_call(
        functools.partial(_dispatch_kernel, td=td),
        grid_spec=grid_spec,
        out_shape=jax.ShapeDtypeStruct((n_rows, d), x.dtype),
        compiler_params=_cparams("arbitrary"),
        name="moe_dispatch",
    )(pad_start, pad_count, n_used, pos_flat, x)


def _ffn_kernel(tile_expert_ref, n_used_ref, xs_ref, wgu_ref, bgu_ref, wd_ref, bd_ref, y_ref, wgu_bf, wd_bf):
    i = pl.program_id(0)
    d_ff = wd_ref.shape[2]
    new_expert = jnp.logical_or(i == 0, tile_expert_ref[i] != tile_expert_ref[jnp.maximum(i - 1, 0)])

    @pl.when(jnp.logical_and(new_expert, i < n_used_ref[0]))
    def _():
        wgu_bf[...] = wgu_ref[0, 0].astype(BF16)
        wd_bf[...] = wd_ref[0, 0].astype(BF16)

    @pl.when(i < n_used_ref[0])
    def _():
        x = xs_ref[...].astype(BF16)
        hgu = _dot(x, wgu_bf[...]) + bgu_ref[0, 0]
        gate = jnp.minimum(hgu[:, :d_ff], SWIGLU_LIMIT)
        up = jnp.clip(hgu[:, d_ff:], -SWIGLU_LIMIT, SWIGLU_LIMIT)
        act = gate * (1.0 / (1.0 + jnp.exp(-SWIGLU_ALPHA * gate))) * (up + 1.0)
        y_ref[...] = _dot(act.astype(BF16), wd_bf[...]) + bd_ref[0, 0]

    @pl.when(i >= n_used_ref[0])
    def _():
        y_ref[...] = jnp.zeros_like(y_ref)


def _expert_ffn(tile_expert, n_used, xs, w_gu, b_gu, w_down, b_down, layer):
    n_rows, d = xs.shape
    depth, e, _, f2 = w_gu.shape
    d_ff = w_down.shape[2]
    tm = TM_FFN
    grid_spec = pltpu.PrefetchScalarGridSpec(
        num_scalar_prefetch=2,
        grid=(n_rows // tm,),
        in_specs=[pl.BlockSpec((tm, d), lambda i, te, nu: (jnp.minimum(i, nu[0] - 1), 0)),
                  pl.BlockSpec((1, 1, d, f2), lambda i, te, nu: (layer, te[i], 0, 0)),
                  pl.BlockSpec((1, 1, 1, f2), lambda i, te, nu: (layer, te[i], 0, 0)),
                  pl.BlockSpec((1, 1, d_ff, d), lambda i, te, nu: (layer, te[i], 0, 0)),
                  pl.BlockSpec((1, 1, 1, d), lambda i, te, nu: (layer, te[i], 0, 0))],
        out_specs=pl.BlockSpec((tm, d), lambda i, te, nu: (i, 0)),
        scratch_shapes=[pltpu.VMEM((d, f2), BF16), pltpu.VMEM((d_ff, d), BF16)],
    )
    return pl.pallas_call(
        _ffn_kernel,
        grid_spec=grid_spec,
        out_shape=jax.ShapeDtypeStruct((n_rows, d), F32),
        compiler_params=_cparams("arbitrary"),
        name="expert_ffn",
    )(tile_expert, n_used, xs, w_gu, b_gu.reshape(depth, e, 1, f2), w_down, b_down.reshape(depth, e, 1, d))


def _combine_kernel(pos_ref, gate_ref, h_ref, g_ref, b_ref, y_hbm, out_ref, buf, sem, *, tc):
    def issue(i, _):
        for k in range(TOP_K):
            _row_copy(y_hbm, pos_ref[i * TOP_K + k], buf.at[k], i, sem).start(priority=k % 2)
        return 0

    lax.fori_loop(0, tc, issue, 0)

    def drain(i, _):
        for k in range(TOP_K):
            _row_copy(y_hbm, pos_ref[i * TOP_K + k], buf.at[k], i, sem).wait()
        return 0

    lax.fori_loop(0, tc, drain, 0)

    gates = gate_ref[...]
    ffn = gates[:, 0:1] * buf[0]
    for k in range(1, TOP_K):
        ffn = ffn + gates[:, k:k + 1] * buf[k]
    out_ref[...] = _layer_norm(DEEPNORM_ALPHA * h_ref[...] + ffn, g_ref[...], b_ref[...])


def _combine(pos_flat, gates, h, g, b, y):
    t, d = h.shape
    tc = min(TC_COMBINE, t)
    return pl.pallas_call(
        functools.partial(_combine_kernel, tc=tc),
        grid=(t // tc,),
        in_specs=[pl.BlockSpec((tc * TOP_K,), lambda i: (i,), memory_space=pltpu.SMEM),
                  pl.BlockSpec((tc, TOP_K), lambda i: (i, 0)),
                  pl.BlockSpec((tc, d), lambda i: (i, 0)),
                  pl.BlockSpec((1, d), lambda i: (0, 0)),
                  pl.BlockSpec((1, d), lambda i: (0, 0)),
                  pl.BlockSpec(memory_space=pl.ANY)],
        out_specs=pl.BlockSpec((tc, d), lambda i: (i, 0)),
        out_shape=jax.ShapeDtypeStruct((t, d), F32),
        scratch_shapes=[pltpu.VMEM((TOP_K, tc, d), F32), pltpu.SemaphoreType.DMA],
        compiler_params=_cparams("arbitrary"),
        name="moe_combine",
    )(pos_flat, gates, h, g, b, y)


def _moe(h, logits, w_gu, b_gu, w_down, b_down, g, b, layer):
    t = h.shape[0]
    idx, gates, rank, counts = _route(logits)
    counts = counts[0, :N_EXPERTS].astype(I32)
    tiles = (counts + TM_FFN - 1) // TM_FFN
    tile_end = jnp.cumsum(tiles)
    start = (tile_end - tiles) * TM_FFN
    n_tiles = (t * TOP_K) // TM_FFN + N_EXPERTS
    tile_ids = jnp.arange(n_tiles, dtype=I32)
    tile_expert = jnp.minimum(jnp.sum((tile_end[None, :] <= tile_ids[:, None]).astype(I32), axis=1), N_EXPERTS - 1)
    pos = (start[idx] + rank).reshape(-1).astype(I32)
    n_used = tile_end[-1:].astype(I32)
    xs = _dispatch(start + counts, tiles * TM_FFN - counts, n_used, pos, h, n_tiles * TM_FFN)
    y = _expert_ffn(tile_expert, n_used, xs, w_gu, b_gu, w_down, b_down, layer)
    return _combine(pos, gates, h, g, b, y)


def _router_params(router_w, router_b):
    rw = jnp.pad(router_w, ((0, 0), (0, LANES - N_EXPERTS))).astype(BF16)
    rb = jnp.pad(router_b, (0, LANES - N_EXPERTS)).reshape(1, LANES)
    return rw, rb


def _alibi_slopes(n):
    return 2.0 ** (-8.0 * jnp.arange(1, n + 1, dtype=F32) / n)


def _even_layer(h, bsz, seq, w_in, w_out, lq1, lk1, lq2, lk2, subw, layer, g, b, rw, rb):
    proj = _project(h, [w_in.astype(BF16)], [BF16])[0].reshape(bsz, seq, -1)
    o_a = _sb_attention(proj, bsz, seq)
    lam_init = 0.8 - 0.6 * math.exp(-0.3 * layer)
    row = lambda v: v.reshape(1, -1)
    o_b = _diff_attention(proj, _alibi_slopes(DIFF_HEADS), row(lq1), row(lk1), row(lq2), row(lk2), row(subw),
                          bsz, seq, lam_init)
    t = bsz * seq
    sbw = SB_HEADS * HEAD_DIM
    wo = w_out.astype(BF16)
    return _mix_call(_mix_even_kernel, "mix_even", [o_a.reshape(t, -1), o_b.reshape(t, -1)],
                     [wo[:sbw], wo[sbw:]], h, row(g), row(b), rw, rb)


def _odd_layer(h, bsz, seq, w_in, kvw, w_uv, w_out, g, b, rw, rb):
    qw = DSA_HEADS * DSA_LATENT
    c0, c1 = qw, qw + DSA_LATENT
    i0, i1 = c1, c1 + IDX_HEADS * IDX_DIM
    k1 = i1 + IDX_DIM
    w_main = jnp.concatenate([w_in[:, :qw], w_in[:, i0:i1], w_in[:, i1:k1], w_in[:, i1:k1]], axis=1).astype(BF16)
    w_small = jnp.pad(jnp.concatenate([w_in[:, c0:c1], w_in[:, k1:]], axis=1),
                      ((0, 0), (0, LANES - IDX_HEADS))).astype(BF16)
    proj, small = [p.reshape(bsz, seq, -1) for p in _project(h, [w_main, w_small], [BF16, F32])]
    o = _dsa_attention(proj, small, kvw.reshape(1, -1), w_uv, bsz, seq)
    row = lambda v: v.reshape(1, -1)
    return _mix_call(_mix_odd_kernel, "mix_odd", [o.reshape(bsz * seq, -1)], [w_out.astype(BF16)],
                     h, row(g), row(b), rw, rb)


def kernel(x, ev_w_in, ev_w_out, ev_lambda_q1, ev_lambda_k1, ev_lambda_q2, ev_lambda_k2, ev_subln_w, od_w_in, od_kv_norm_w, od_w_uv, od_w_out, ln_mix_g, ln_mix_b, router_w, router_b, exp_w_gu, exp_b_gu, exp_w_down, exp_b_down, ln_ffn_g, ln_ffn_b):
    bsz, seq, d = x.shape
    h = x.reshape(bsz * seq, d)
    for layer in range(ln_mix_g.shape[0]):
        j = layer // 2
        rw, rb = _router_params(router_w[layer], router_b[layer])
        if layer % 2 == 0:
            h, logits = _even_layer(h, bsz, seq, ev_w_in[j], ev_w_out[j], ev_lambda_q1[j], ev_lambda_k1[j],
                                    ev_lambda_q2[j], ev_lambda_k2[j], ev_subln_w[j], layer,
                                    ln_mix_g[layer], ln_mix_b[layer], rw, rb)
        else:
            h, logits = _odd_layer(h, bsz, seq, od_w_in[j], od_kv_norm_w[j], od_w_uv[j], od_w_out[j],
                                   ln_mix_g[layer], ln_mix_b[layer], rw, rb)
        h = _moe(h, logits, exp_w_gu, exp_b_gu, exp_w_down, exp_b_down,
                 ln_ffn_g[layer].reshape(1, -1), ln_ffn_b[layer].reshape(1, -1), layer)
    return h.reshape(bsz, seq, d)
```

```python
import functools
import math

import jax
import jax.numpy as jnp
from jax import lax
from jax.experimental import pallas as pl
from jax.experimental.pallas import tpu as pltpu

F32, BF16, I32 = jnp.float32, jnp.bfloat16, jnp.int32

HEAD_DIM = 64
SB_HEADS = 8
DIFF_HEADS = 4
DIFF_SUB = 64
DIFF_VDIM = 128
DSA_HEADS = 16
DSA_LATENT = 128
DSA_VDIM = 64
IDX_HEADS = 8
IDX_DIM = 64
TOPK_MAX = 256
N_EXPERTS = 32
TOP_K = 4
SWIGLU_LIMIT = 7.0
SWIGLU_ALPHA = 1.702
LN_EPS = 1e-5
RMS_EPS = 1e-5
DEPTH = 2
DEEPNORM_ALPHA = (2 * DEPTH) ** 0.25

LANES = 128
SUBLANES = 8
BF16_ROWS = 16
VMEM_LIMIT = 56 * 1024 * 1024
INT_MIN = -(2 ** 31)
MASKED = -1e30
DEAD_LOG2 = -152.0
LOG2_E = math.log2(math.e)

TM_PROJ = 512
TQ_ATT = 256
TQ_DSA = 128
TRI = 512
HG_DSA = 16
TM_FFN = 512
TD_DISPATCH = 512
TC_COMBINE = 256


def _cparams(*sem):
    return pltpu.CompilerParams(dimension_semantics=sem, vmem_limit_bytes=VMEM_LIMIT)


def _nt_dot(a, b):
    return lax.dot_general(a, b, (((1,), (1,)), ((), ())), preferred_element_type=F32)


def _dot(a, b):
    return jnp.dot(a, b, preferred_element_type=F32)


def _layer_norm(y, g, b):
    mu = jnp.mean(y, axis=-1, keepdims=True)
    d = y - mu
    var = jnp.mean(d * d, axis=-1, keepdims=True)
    return d * lax.rsqrt(var + LN_EPS) * g + b


def _col_chunk(n):
    for c in (512, 384, 256, 128):
        if n % c == 0:
            return c
    raise ValueError(n)


def _proj_kernel(x_ref, *refs):
    n_out = len(refs) // 2
    x = x_ref[...].astype(BF16)
    for w_ref, o_ref in zip(refs[:n_out], refs[n_out:]):
        n = o_ref.shape[-1]
        c = _col_chunk(n)
        for j in range(0, n, c):
            o_ref[:, j:j + c] = _dot(x, w_ref[:, j:j + c]).astype(o_ref.dtype)


def _project(x, weights, out_dtypes):
    t, k = x.shape
    tm = min(TM_PROJ, t)
    return pl.pallas_call(
        _proj_kernel,
        grid=(t // tm,),
        in_specs=[pl.BlockSpec((tm, k), lambda i: (i, 0))]
                 + [pl.BlockSpec(w.shape, lambda i: (0, 0)) for w in weights],
        out_specs=[pl.BlockSpec((tm, w.shape[1]), lambda i: (i, 0)) for w in weights],
        out_shape=[jax.ShapeDtypeStruct((t, w.shape[1]), dt) for w, dt in zip(weights, out_dtypes)],
        compiler_params=_cparams("parallel"),
        name="in_proj",
    )(x, *weights)


def _stack_halves(x, lane):
    zero = jnp.zeros_like(x)
    return jnp.concatenate([jnp.where(lane < HEAD_DIM, x, zero), jnp.where(lane >= HEAD_DIM, x, zero)], axis=0)


def _sb_kernel(q_ref, k_ref, v_ref, o_ref, *, tq):
    i = pl.program_id(2)
    scale = HEAD_DIM ** -0.5
    lane = lax.broadcasted_iota(I32, (1, LANES), 1)
    row = lax.broadcasted_iota(I32, (2 * tq, tq), 0)
    row = jnp.where(row >= tq, row - tq, row)
    col = lax.broadcasted_iota(I32, (2 * tq, tq), 1)
    strict = col < row
    r1 = lax.broadcasted_iota(I32, (tq, tq), 0)
    c1 = lax.broadcasted_iota(I32, (tq, tq), 1)
    after = (r1 > c1).astype(BF16)
    q2 = _stack_halves(q_ref[0], lane)

    def block(j, carry, masked):
        run, acc = carry
        start = pl.multiple_of(j * tq, tq)
        kb = k_ref[0, pl.ds(start, tq), :]
        vb = v_ref[0, pl.ds(start, tq), :]
        z2 = _nt_dot(q2, kb) * (scale * LOG2_E)
        neg = -z2
        keep2 = jnp.minimum(neg, 0.0) - jnp.log2(1.0 + jnp.exp2(jnp.minimum(z2, neg)))
        lk = jnp.where(strict, keep2, 0.0) if masked else keep2
        hi = lk.astype(BF16)
        lo = (lk - hi.astype(F32)).astype(BF16)
        suffix = _dot(hi, after) + _dot(lo, after)
        w = jnp.exp2(z2 + keep2 + suffix + run)
        if masked:
            w = jnp.where(strict, w, 0.0)
        wb = w.astype(BF16)
        zero = jnp.zeros_like(vb)
        acc = (acc + _dot(wb[:tq], jnp.where(lane < HEAD_DIM, vb, zero))
               + _dot(wb[tq:], jnp.where(lane >= HEAD_DIM, vb, zero)))
        run = run + suffix[:, 0:1] + lk[:, 0:1]
        return run, acc

    def alive(c):
        return jnp.logical_and(c[0] >= 0, c[1] > DEAD_LOG2)

    def step(c):
        j, _, run, acc = c
        run, acc = block(j, (run, acc), False)
        return j - 1, jnp.max(run), run, acc

    start = (jnp.zeros((2 * tq, 1), F32), jnp.zeros((tq, LANES), F32))

    @pl.when(i == 0)
    def _():
        o_ref[0] = block(0, start, True)[1].astype(o_ref.dtype)

    @pl.when(i > 0)
    def _():
        run, acc = block(i - 1, block(i, start, True), False)
        _, _, _, acc = lax.while_loop(alive, step, (i - 2, jnp.max(run), run, acc))
        o_ref[0] = acc.astype(o_ref.dtype)


def _sb_attention(proj, bsz, seq):
    tq = min(TQ_ATT, seq)
    pairs = SB_HEADS * HEAD_DIM // LANES
    return pl.pallas_call(
        functools.partial(_sb_kernel, tq=tq),
        grid=(bsz, pairs, seq // tq),
        in_specs=[pl.BlockSpec((1, tq, LANES), lambda b, p, i: (b, i, p)),
                  pl.BlockSpec((1, seq, LANES), lambda b, p, i: (b, 0, pairs + p)),
                  pl.BlockSpec((1, seq, LANES), lambda b, p, i: (b, 0, 2 * pairs + p))],
        out_specs=pl.BlockSpec((1, tq, LANES), lambda b, p, i: (b, i, p)),
        out_shape=jax.ShapeDtypeStruct((bsz, seq, SB_HEADS * HEAD_DIM), BF16),
        compiler_params=_cparams("parallel", "parallel", "parallel"),
        name="stickbreak_attn",
    )(proj, proj, proj)


def _diff_kernel(slope_ref, lq1_ref, lk1_ref, lq2_ref, lk2_ref, subw_ref, q_ref, k_ref, v_ref, o_ref,
                 *, tq, lam_init):
    h = pl.program_id(1)
    i = pl.program_id(2)
    scale = DIFF_SUB ** -0.5
    slope = slope_ref[h]
    lam = (jnp.exp(jnp.sum(lq1_ref[...] * lk1_ref[...], keepdims=True))
           - jnp.exp(jnp.sum(lq2_ref[...] * lk2_ref[...], keepdims=True)) + lam_init)
    lane = lax.broadcasted_iota(I32, (1, LANES), 1)
    row = lax.broadcasted_iota(I32, (2 * tq, tq), 0)
    row = jnp.where(row >= tq, row - tq, row)
    col = lax.broadcasted_iota(I32, (2 * tq, tq), 1)
    causal = col <= row
    key_off = lax.broadcasted_iota(I32, (1, tq), 1)
    q2 = _stack_halves(q_ref[0], lane)

    def block(j, carry, masked):
        m, l, acc = carry
        start = j * tq
        kb = k_ref[0, pl.ds(start, tq), :]
        vb = v_ref[0, pl.ds(start, tq), :]
        s = _nt_dot(q2, kb) * (scale * LOG2_E) + (slope * LOG2_E) * (start + key_off).astype(F32)
        if masked:
            s = jnp.where(causal, s, -jnp.inf)
        m_new = jnp.maximum(m, jnp.max(s, axis=-1, keepdims=True))
        a = jnp.exp2(m - m_new)
        p = jnp.exp2(s - m_new)
        l = a * l + jnp.sum(p, axis=-1, keepdims=True)
        acc = a * acc + _dot(p.astype(BF16), vb)
        return m_new, l, acc

    for n_left in range(k_ref.shape[1] // tq):
        @pl.when(i == n_left)
        def _(n_left=n_left):
            carry = (jnp.full((2 * tq, 1), -jnp.inf, F32), jnp.zeros((2 * tq, 1), F32),
                     jnp.zeros((2 * tq, LANES), F32))
            carry = block(n_left, carry, True)
            for j in range(n_left):
                carry = block(j, carry, False)
            out = carry[2] / carry[1]
            o = out[:tq] - lam * out[tq:]
            o = o * lax.rsqrt(jnp.mean(o * o, axis=-1, keepdims=True) + RMS_EPS) * subw_ref[...]
            o_ref[0] = (o * (1.0 - lam_init)).astype(o_ref.dtype)


def _diff_attention(proj, slopes, lq1, lk1, lq2, lk2, subw, bsz, seq, lam_init):
    tq = min(TQ_ATT, seq)
    q0 = 3 * SB_HEADS * HEAD_DIM // LANES
    k0 = q0 + DIFF_HEADS
    v0 = k0 + DIFF_HEADS
    vec = lambda n: pl.BlockSpec((1, n), lambda b, h, i: (0, 0))
    return pl.pallas_call(
        functools.partial(_diff_kernel, tq=tq, lam_init=lam_init),
        grid=(bsz, DIFF_HEADS, seq // tq),
        in_specs=[pl.BlockSpec(memory_space=pltpu.SMEM),
                  vec(DIFF_SUB), vec(DIFF_SUB), vec(DIFF_SUB), vec(DIFF_SUB), vec(DIFF_VDIM),
                  pl.BlockSpec((1, tq, LANES), lambda b, h, i: (b, i, q0 + h)),
                  pl.BlockSpec((1, seq, LANES), lambda b, h, i: (b, 0, k0 + h)),
                  pl.BlockSpec((1, seq, LANES), lambda b, h, i: (b, 0, v0 + h))],
        out_specs=pl.BlockSpec((1, tq, LANES), lambda b, h, i: (b, i, h)),
        out_shape=jax.ShapeDtypeStruct((bsz, seq, DIFF_HEADS * DIFF_VDIM), BF16),
        compiler_params=_cparams("parallel", "parallel", "parallel"),
        name="diff_attn",
    )(slopes, lq1, lk1, lq2, lk2, subw, proj, proj, proj)


def _tree_sum_rows(x):
    n = x.shape[0] // 8
    x = x.reshape(n, 8, x.shape[1])
    while n > 1:
        n //= 2
        x = x[:n] + x[n:]
    return x[0]


def _tree_max_rows(x):
    n = x.shape[0] // 8
    x = x.reshape(n, 8, x.shape[1])
    while n > 1:
        n //= 2
        x = jnp.maximum(x[:n], x[n:])
    return x[0]


def _dsa_kernel(q_ref, ckv_ref, kvw_ref, qi_ref, ki_ref, wi_ref, qx_ref, kx_ref, wup_ref, o_ref,
                cn_ref, cnt_ref, skey_ref, bias_ref, s_ref, *, tq, tk, k_sel, hg):
    i = pl.program_id(1)
    t0 = i * tq

    @pl.when(i == 0)
    def _():
        c = ckv_ref[0]
        cn = c * lax.rsqrt(jnp.mean(c * c, axis=-1, keepdims=True) + RMS_EPS) * kvw_ref[...]
        cn_ref[:, :DSA_LATENT] = cn.astype(BF16)
        cn_ref[:, DSA_LATENT:] = kx_ref[...]
        cnt_ref[:DSA_LATENT, :] = cn.T.astype(BF16)
        row = lax.broadcasted_iota(I32, (BF16_ROWS, cnt_ref.shape[1]), 0)
        cnt_ref[DSA_LATENT:, :] = jnp.where(row == 0, 1.0, 0.0).astype(BF16)

    for k in range(ckv_ref.shape[1] // tk):
        @pl.when((t0 + tq - 1) // tk == k)
        def _(k=k):
            _dsa_block(q_ref, qi_ref, ki_ref, wi_ref, qx_ref, wup_ref, o_ref, cn_ref, cnt_ref, skey_ref, bias_ref,
                       s_ref, t0=t0, n_tiles=k + 1, tq=tq, tk=tk, k_sel=k_sel, hg=hg)


def _static_loop(n, body, carry):
    for j in range(n):
        carry = body(j, carry)
    return carry


def _dsa_block(q_ref, qi_ref, ki_ref, wi_ref, qx_ref, wup_ref, o_ref, cn_ref, cnt_ref, skey_ref, bias_ref, s_ref,
               *, t0, n_tiles, tq, tk, k_sel, hg):
    def tile_slice(j):
        return pl.ds(j * tk, tk)

    lane = lax.broadcasted_iota(I32, (1, LANES), 1)
    q_pos = t0 + lax.broadcasted_iota(I32, (tk, tq), 1)
    key_off = lax.broadcasted_iota(I32, (tk, tq), 0)

    w_t = (wi_ref[0] * ((IDX_HEADS ** -0.5) * (IDX_DIM ** -0.5))).T
    stacked = []
    for h in range(IDX_HEADS):
        qp = qi_ref[0, :, (h // 2) * LANES:(h // 2 + 1) * LANES]
        stacked.append(jnp.where((lane // IDX_DIM) == (h % 2), qp, jnp.zeros_like(qp)))
    q8 = jnp.concatenate(stacked, axis=0)

    def score_tile(j, _):
        d = _nt_dot(ki_ref[0, tile_slice(j), :], q8)
        score = jnp.zeros((tk, tq), F32)
        for h in range(IDX_HEADS):
            score = score + jnp.maximum(d[:, h * tq:(h + 1) * tq], 0.0) * w_t[h:h + 1, :]
        score = jnp.where(score == 0.0, 0.0, score)
        bits = pltpu.bitcast(score, I32)
        skey = jnp.where(bits < 0, bits ^ 0x7FFFFFFF, bits)
        skey_ref[tile_slice(j), :] = jnp.where(j * tk + key_off <= q_pos, skey, INT_MIN)
        return 0

    _static_loop(n_tiles, score_tile, 0)

    def count(pred):
        def tile(j, cnt):
            return cnt + _tree_sum_rows(pred(skey_ref[tile_slice(j), :]).astype(F32))
        cnt = _static_loop(n_tiles, tile, jnp.zeros((8, tq), F32))
        return jnp.sum(cnt, axis=0, keepdims=True)

    kf = float(k_sel)
    thr = jnp.where(count(lambda sk: sk >= 0) >= kf, 0, INT_MIN).astype(I32)

    def bit_step(b, thr):
        cand = thr + jnp.left_shift(jnp.int32(1), 30 - b)
        return jnp.where(count(lambda sk: sk >= cand) >= kf, cand, thr)

    thr = lax.fori_loop(0, 31, bit_step, thr)
    need = kf - count(lambda sk: sk > thr)

    r2 = lax.broadcasted_iota(I32, (tk, tk), 0)
    c2 = lax.broadcasted_iota(I32, (tk, tk), 1)
    upto = (c2 <= r2).astype(BF16)

    def tie_tile(j, seen):
        sk = skey_ref[tile_slice(j), :]
        tied = sk == thr
        prefix = _dot(upto, tied.astype(F32).astype(BF16)) + seen
        selected = ((sk > thr) | (tied & (prefix <= need))) & (j * tk + key_off <= q_pos)
        bias_ref[tile_slice(j), :] = jnp.where(selected, 0.0, MASKED)
        return prefix[tk - 1:tk, :]

    _static_loop(n_tiles, tie_tile, jnp.zeros((1, tq), F32))

    log2_scale = (DSA_LATENT ** -0.5) * LOG2_E

    def head_group(g, _):
        heads = [g * hg + u for u in range(hg)]
        offs = [pl.multiple_of(h * DSA_LATENT, DSA_LATENT) for h in heads]
        qg = jnp.concatenate(
            [jnp.concatenate([q_ref[0, :, pl.ds(off, DSA_LATENT)],
                              jnp.broadcast_to(qx_ref[pl.ds(h, 1), :], (tq, LANES)).astype(BF16)], axis=1)
             for h, off in zip(heads, offs)], axis=0)

        def logits_tile(j, m):
            d = _nt_dot(cn_ref[tile_slice(j), :], qg)
            b = bias_ref[tile_slice(j), :]
            tops = []
            for u in range(hg):
                s = d[:, u * tq:(u + 1) * tq] * log2_scale + b
                s_ref[tile_slice(j), u * tq:(u + 1) * tq] = s
                tops.append(jnp.max(_tree_max_rows(s), axis=0, keepdims=True))
            return jnp.maximum(m, jnp.concatenate(tops, axis=1))

        m = _static_loop(n_tiles, logits_tile, jnp.full((1, hg * tq), MASKED, F32))

        def value_tile(j, acc):
            p = jnp.exp2((s_ref[tile_slice(j), :] - m).astype(BF16))
            return acc + _dot(cnt_ref[:, tile_slice(j)], p)

        acc = _static_loop(n_tiles, value_tile, jnp.zeros((DSA_LATENT + BF16_ROWS, hg * tq), F32))
        out = (acc[:DSA_LATENT] / acc[DSA_LATENT:DSA_LATENT + 1]).astype(BF16)
        for u in range(0, hg, 2):
            pair = jnp.concatenate([out[:, u * tq:(u + 1) * tq], out[:, (u + 1) * tq:(u + 2) * tq]], axis=0)
            up = _dot(wup_ref[(g * hg + u) // 2], pair)
            off = pl.multiple_of((g * hg + u) * DSA_VDIM, 2 * DSA_VDIM)
            o_ref[0, :, pl.ds(off, 2 * DSA_VDIM)] = up.T.astype(o_ref.dtype)
        return 0

    lax.fori_loop(0, DSA_HEADS // hg, head_group, 0)


def _bf16_parts(x, n):
    parts = []
    for _ in range(n):
        p = x.astype(BF16).astype(F32)
        parts.append(p)
        x = x - p
    return parts


def _dsa_attention(proj, small, kvw, w_uv, bsz, seq):
    tq = min(TQ_DSA, seq)
    tk = min(TRI, seq)
    k_sel = min(TOPK_MAX, seq // 4)
    qw = DSA_HEADS * DSA_LATENT
    qiw = IDX_HEADS * IDX_DIM
    slope = _alibi_slopes(DSA_HEADS) / (DSA_LATENT ** -0.5)
    s_parts = _bf16_parts(slope, 3)
    pos = jnp.arange(seq, dtype=I32)
    p_parts = [(pos // LANES * LANES).astype(F32), (pos % LANES).astype(F32)]
    qx = jnp.stack([sp for _ in p_parts for sp in s_parts], axis=1)
    kx = jnp.stack([pp for pp in p_parts for _ in s_parts], axis=1)
    qx = jnp.pad(qx, ((0, 0), (0, LANES - qx.shape[1])))
    kx = jnp.pad(kx, ((0, 0), (0, LANES - kx.shape[1]))).astype(BF16)
    wt = jnp.swapaxes(w_uv, 1, 2).reshape(DSA_HEADS // 2, 2, DSA_VDIM, DSA_LATENT)
    zero = jnp.zeros_like(wt[:, 0])
    wup = jnp.concatenate([jnp.concatenate([wt[:, 0], zero], axis=2),
                           jnp.concatenate([zero, wt[:, 1]], axis=2)], axis=1).astype(BF16)
    return pl.pallas_call(
        functools.partial(_dsa_kernel, tq=tq, tk=tk, k_sel=k_sel, hg=HG_DSA),
        grid=(bsz, seq // tq),
        in_specs=[pl.BlockSpec((1, tq, qw), lambda b, i: (b, i, 0)),
                  pl.BlockSpec((1, seq, DSA_LATENT), lambda b, i: (b, 0, 0)),
                  pl.BlockSpec((1, DSA_LATENT), lambda b, i: (0, 0)),
                  pl.BlockSpec((1, tq, qiw), lambda b, i: (b, i, qw // qiw)),
                  pl.BlockSpec((1, seq, LANES), lambda b, i: (b, 0, (qw + qiw) // LANES)),
                  pl.BlockSpec((1, tq, LANES), lambda b, i: (b, i, 1)),
                  pl.BlockSpec((DSA_HEADS, LANES), lambda b, i: (0, 0)),
                  pl.BlockSpec((seq, LANES), lambda b, i: (0, 0)),
                  pl.BlockSpec(wup.shape, lambda b, i: (0, 0, 0))],
        out_specs=pl.BlockSpec((1, tq, DSA_HEADS * DSA_VDIM), lambda b, i: (b, i, 0)),
        out_shape=jax.ShapeDtypeStruct((bsz, seq, DSA_HEADS * DSA_VDIM), BF16),
        scratch_shapes=[pltpu.VMEM((seq, DSA_LATENT + LANES), BF16),
                        pltpu.VMEM((DSA_LATENT + BF16_ROWS, seq), BF16),
                        pltpu.VMEM((seq, tq), I32), pltpu.VMEM((seq, tq), F32),
                        pltpu.VMEM((seq, HG_DSA * tq), F32)],
        compiler_params=_cparams("parallel", "arbitrary"),
        name="dsa_attn",
    )(proj, small, kvw, proj, proj, small, qx, kx, wup)


def _mix_tail(mix, h_ref, g_ref, b_ref, rw_ref, rb_ref, hout_ref, logit_ref):
    hn = _layer_norm(DEEPNORM_ALPHA * h_ref[...] + mix, g_ref[...], b_ref[...])
    hout_ref[...] = hn
    logit_ref[...] = _dot(hn.astype(BF16), rw_ref[...]) + rb_ref[...]


def _mix_even_kernel(oa_ref, ob_ref, wa_ref, wb_ref, h_ref, g_ref, b_ref, rw_ref, rb_ref, hout_ref, logit_ref):
    mix = _dot(oa_ref[...], wa_ref[...]) + _dot(ob_ref[...], wb_ref[...])
    _mix_tail(mix, h_ref, g_ref, b_ref, rw_ref, rb_ref, hout_ref, logit_ref)


def _mix_odd_kernel(o_ref, wo_ref, h_ref, g_ref, b_ref, rw_ref, rb_ref, hout_ref, logit_ref):
    _mix_tail(_dot(o_ref[...], wo_ref[...]), h_ref, g_ref, b_ref, rw_ref, rb_ref, hout_ref, logit_ref)


def _mix_call(kernel, name, acts, weights, h, g, b, rw, rb):
    t, d = h.shape
    tm = min(TM_PROJ, t)
    row = lambda n: pl.BlockSpec((tm, n), lambda i: (i, 0))
    full = lambda a: pl.BlockSpec(a.shape, lambda i: (0, 0))
    return pl.pallas_call(
        kernel,
        grid=(t // tm,),
        in_specs=[row(a.shape[1]) for a in acts] + [full(w) for w in weights]
                 + [row(d), full(g), full(b), full(rw), full(rb)],
        out_specs=[row(d), row(LANES)],
        out_shape=[jax.ShapeDtypeStruct((t, d), F32), jax.ShapeDtypeStruct((t, LANES), F32)],
        compiler_params=_cparams("parallel"),
        name=name,
    )(*acts, *weights, h, g, b, rw, rb)


def _route_kernel(logit_ref, idx_ref, gate_ref, rank_ref, cnt_ref, *, tm):
    lane = lax.broadcasted_iota(I32, (tm, LANES), 1)
    lane_f = lane.astype(F32)
    x = jnp.where(lane < N_EXPERTS, logit_ref[...], -jnp.inf)
    vals, hots = [], []
    for k in range(TOP_K):
        m = jnp.max(x, axis=-1, keepdims=True)
        first = jnp.min(jnp.where(x == m, lane_f, float(LANES)), axis=-1, keepdims=True)
        hot = lane_f == first
        x = jnp.where(hot, -jnp.inf, x)
        vals.append(m)
        hots.append(hot)
        idx_ref[:, k:k + 1] = first.astype(I32)
    exps = [jnp.exp(v - vals[0]) for v in vals]
    denom = exps[0] + exps[1] + exps[2] + exps[3]
    for k in range(TOP_K):
        gate_ref[:, k:k + 1] = exps[k] / denom

    hot_sum = (hots[0] | hots[1] | hots[2] | hots[3]).astype(F32)
    r = lax.broadcasted_iota(I32, (tm, tm), 0)
    c = lax.broadcasted_iota(I32, (tm, tm), 1)
    earlier = (c < r).astype(BF16)
    before = _dot(earlier, hot_sum.astype(BF16))
    for k in range(TOP_K):
        rank_ref[:, k:k + 1] = jnp.sum(jnp.where(hots[k], before, 0.0), axis=-1, keepdims=True).astype(I32)
    cnt_ref[0] = jnp.sum(hot_sum, axis=0, keepdims=True)


def _route(logits):
    t = logits.shape[0]
    tm = min(TD_DISPATCH, t)
    narrow = pl.BlockSpec((tm, TOP_K), lambda i: (i, 0))
    return pl.pallas_call(
        functools.partial(_route_kernel, tm=tm),
        grid=(t // tm,),
        in_specs=[pl.BlockSpec((tm, LANES), lambda i: (i, 0))],
        out_specs=[narrow, narrow, narrow, pl.BlockSpec((1, 1, LANES), lambda i: (i, 0, 0))],
        out_shape=[jax.ShapeDtypeStruct((t, TOP_K), I32), jax.ShapeDtypeStruct((t, TOP_K), F32),
                   jax.ShapeDtypeStruct((t, TOP_K), I32), jax.ShapeDtypeStruct((t // tm, 1, LANES), F32)],
        compiler_params=_cparams("parallel"),
        name="route",
    )(logits)


def _row_copy(src_ref, src_row, dst_ref, dst_row, sem):
    return pltpu.make_async_copy(src_ref.at[pl.ds(src_row, 1), :], dst_ref.at[pl.ds(dst_row, 1), :], sem)


def _chunked_copy(src_ref, src_off, dst_ref, dst_off, count, largest, sem, start_not_wait):
    chunk = largest
    while chunk >= SUBLANES:
        present = (count & chunk) != 0
        copy = pltpu.make_async_copy(src_ref.at[pl.ds(pl.multiple_of(src_off, SUBLANES), chunk), :],
                                     dst_ref.at[pl.ds(pl.multiple_of(dst_off, SUBLANES), chunk), :], sem)

        @pl.when(jnp.logical_and(present, start_not_wait))
        def _():
            copy.start()

        @pl.when(jnp.logical_and(present, jnp.logical_not(start_not_wait)))
        def _():
            copy.wait()

        step = jnp.where(present, chunk, 0)
        src_off = src_off + step
        dst_off = dst_off + step
        chunk //= 2


def _dispatch_kernel(seg_rows_ref, seg_src_ref, seg_dst_ref, pad_start_ref, pad_count_ref, n_used_ref,
                     lp_ref, x_ref, xs_out, staged_ref, zero_ref, sem, pad_sem, *, td):
    i = pl.program_id(0)
    tile = zero_ref.shape[0]

    @pl.when(i == 0)
    def _():
        zero_ref[...] = jnp.zeros_like(zero_ref)

        def pads(e, start_not_wait):
            _chunked_copy(zero_ref, 0, xs_out, pad_start_ref[e], pad_count_ref[e], tile // 2, pad_sem,
                          start_not_wait)
            return start_not_wait

        def tail(j, start_not_wait):
            present = j >= n_used_ref[0]
            copy = pltpu.make_async_copy(zero_ref, xs_out.at[pl.ds(pl.multiple_of(j * tile, tile), tile), :], pad_sem)

            @pl.when(jnp.logical_and(present, start_not_wait))
            def _():
                copy.start()

            @pl.when(jnp.logical_and(present, jnp.logical_not(start_not_wait)))
            def _():
                copy.wait()
            return start_not_wait

        n_tiles = xs_out.shape[0] // tile
        lax.fori_loop(0, N_EXPERTS, pads, True)
        lax.fori_loop(0, n_tiles, tail, True)
        lax.fori_loop(0, N_EXPERTS, pads, False)
        lax.fori_loop(0, n_tiles, tail, False)

    rows = staged_ref.shape[0]
    r_iota = lax.broadcasted_iota(I32, (rows, td), 0)
    hit = r_iota == lp_ref[0, 0:1, :]
    for k in range(1, TOP_K):
        hit = hit | (r_iota == lp_ref[0, k:k + 1, :])
    staged_ref[...] = _dot(jnp.where(hit, 1.0, 0.0).astype(BF16), x_ref[...].astype(BF16))

    def runs(e, start_not_wait):
        s = i * N_EXPERTS + e
        _chunked_copy(staged_ref, seg_src_ref[s], xs_out, seg_dst_ref[s], seg_rows_ref[s], td, sem, start_not_wait)
        return start_not_wait

    lax.fori_loop(0, N_EXPERTS, runs, True)
    lax.fori_loop(0, N_EXPERTS, runs, False)


def _dispatch(seg_rows, seg_src, seg_dst, pad_start, pad_count, n_used, lp, x, n_rows):
    t, d = x.shape
    td = min(TD_DISPATCH, t)
    staged_rows = td * TOP_K + N_EXPERTS * SUBLANES
    grid_spec = pltpu.PrefetchScalarGridSpec(
        num_scalar_prefetch=6,
        grid=(t // td,),
        in_specs=[pl.BlockSpec((1, TOP_K, td), lambda i, *_: (i, 0, 0)),
                  pl.BlockSpec((td, d), lambda i, *_: (i, 0))],
        out_specs=pl.BlockSpec(memory_space=pl.ANY),
        scratch_shapes=[pltpu.VMEM((staged_rows, d), F32), pltpu.VMEM((TM_FFN, d), F32),
                        pltpu.SemaphoreType.DMA, pltpu.SemaphoreType.DMA],
    )
    return pl.pallas_call(
        functools.partial(_dispatch_kernel, td=td),
        grid_spec=grid_spec,
        out_shape=jax.ShapeDtypeStruct((n_rows, d), F32),
        compiler_params=_cparams("arbitrary"),
        name="moe_dispatch",
    )(seg_rows, seg_src, seg_dst, pad_start, pad_count, n_used, lp, x)


def _ffn_kernel(tile_expert_ref, n_used_ref, xs_ref, wgu_ref, bgu_ref, wd_ref, bd_ref, y_ref, wgu_bf, wd_bf):
    i = pl.program_id(0)
    d_ff = wd_ref.shape[2]
    new_expert = jnp.logical_or(i == 0, tile_expert_ref[i] != tile_expert_ref[jnp.maximum(i - 1, 0)])

    @pl.when(jnp.logical_and(new_expert, i < n_used_ref[0]))
    def _():
        wgu_bf[...] = wgu_ref[0, 0].astype(BF16)
        wd_bf[...] = wd_ref[0, 0].astype(BF16)

    @pl.when(i < n_used_ref[0])
    def _():
        x = xs_ref[...].astype(BF16)
        hgu = _dot(x, wgu_bf[...]) + bgu_ref[0, 0]
        gate = jnp.minimum(hgu[:, :d_ff], SWIGLU_LIMIT)
        up = jnp.clip(hgu[:, d_ff:], -SWIGLU_LIMIT, SWIGLU_LIMIT)
        act = gate * (1.0 / (1.0 + jnp.exp(-SWIGLU_ALPHA * gate))) * (up + 1.0)
        y_ref[...] = _dot(act.astype(BF16), wd_bf[...]) + bd_ref[0, 0]

    @pl.when(i >= n_used_ref[0])
    def _():
        y_ref[...] = jnp.zeros_like(y_ref)


def _expert_ffn(tile_expert, n_used, xs, w_gu, b_gu, w_down, b_down, layer):
    n_rows, d = xs.shape
    depth, e, _, f2 = w_gu.shape
    d_ff = w_down.shape[2]
    tm = TM_FFN
    grid_spec = pltpu.PrefetchScalarGridSpec(
        num_scalar_prefetch=2,
        grid=(n_rows // tm,),
        in_specs=[pl.BlockSpec((tm, d), lambda i, te, nu: (jnp.minimum(i, nu[0] - 1), 0)),
                  pl.BlockSpec((1, 1, d, f2), lambda i, te, nu: (layer, te[i], 0, 0)),
                  pl.BlockSpec((1, 1, 1, f2), lambda i, te, nu: (layer, te[i], 0, 0)),
                  pl.BlockSpec((1, 1, d_ff, d), lambda i, te, nu: (layer, te[i], 0, 0)),
                  pl.BlockSpec((1, 1, 1, d), lambda i, te, nu: (layer, te[i], 0, 0))],
        out_specs=pl.BlockSpec((tm, d), lambda i, te, nu: (i, 0)),
        scratch_shapes=[pltpu.VMEM((d, f2), BF16), pltpu.VMEM((d_ff, d), BF16)],
    )
    return pl.pallas_call(
        _ffn_kernel,
        grid_spec=grid_spec,
        out_shape=jax.ShapeDtypeStruct((n_rows, d), F32),
        compiler_params=_cparams("arbitrary"),
        name="expert_ffn",
    )(tile_expert, n_used, xs, w_gu, b_gu.reshape(depth, e, 1, f2), w_down, b_down.reshape(depth, e, 1, d))


def _combine_kernel(pos_ref, gate_ref, h_ref, g_ref, b_ref, y_hbm, out_ref, buf, sem, *, tc):
    def issue(i, _):
        for k in range(TOP_K):
            _row_copy(y_hbm, pos_ref[i * TOP_K + k], buf.at[k], i, sem).start()
        return 0

    lax.fori_loop(0, tc, issue, 0)

    def drain(i, _):
        for k in range(TOP_K):
            _row_copy(y_hbm, pos_ref[i * TOP_K + k], buf.at[k], i, sem).wait()
        return 0

    lax.fori_loop(0, tc, drain, 0)

    gates = gate_ref[...]
    ffn = gates[:, 0:1] * buf[0]
    for k in range(1, TOP_K):
        ffn = ffn + gates[:, k:k + 1] * buf[k]
    out_ref[...] = _layer_norm(DEEPNORM_ALPHA * h_ref[...] + ffn, g_ref[...], b_ref[...])


def _combine(pos_flat, gates, h, g, b, y):
    t, d = h.shape
    tc = min(TC_COMBINE, t)
    return pl.pallas_call(
        functools.partial(_combine_kernel, tc=tc),
        grid=(t // tc,),
        in_specs=[pl.BlockSpec((tc * TOP_K,), lambda i: (i,), memory_space=pltpu.SMEM),
                  pl.BlockSpec((tc, TOP_K), lambda i: (i, 0)),
                  pl.BlockSpec((tc, d), lambda i: (i, 0)),
                  pl.BlockSpec((1, d), lambda i: (0, 0)),
                  pl.BlockSpec((1, d), lambda i: (0, 0)),
                  pl.BlockSpec(memory_space=pl.ANY)],
        out_specs=pl.BlockSpec((tc, d), lambda i: (i, 0)),
        out_shape=jax.ShapeDtypeStruct((t, d), F32),
        scratch_shapes=[pltpu.VMEM((TOP_K, tc, d), F32), pltpu.SemaphoreType.DMA],
        compiler_params=_cparams("arbitrary"),
        name="moe_combine",
    )(pos_flat, gates, h, g, b, y)


def _moe(h, logits, w_gu, b_gu, w_down, b_down, g, b, layer):
    t = h.shape[0]
    td = min(TD_DISPATCH, t)
    n_tok_tiles = t // td
    idx, gates, rank, counts = _route(logits)
    counts = counts[:, 0, :N_EXPERTS].astype(I32)
    seg = (counts + SUBLANES - 1) // SUBLANES * SUBLANES
    rows = jnp.sum(seg, axis=0)
    tiles = (rows + TM_FFN - 1) // TM_FFN
    tile_end = jnp.cumsum(tiles)
    start = (tile_end - tiles) * TM_FFN
    seg_dst = start[None, :] + jnp.cumsum(seg, axis=0) - seg
    seg_src = jnp.cumsum(seg, axis=1) - seg
    n_tiles = -(-(t * TOP_K + n_tok_tiles * N_EXPERTS * (SUBLANES - 1)) // TM_FFN) + N_EXPERTS
    tile_ids = jnp.arange(n_tiles, dtype=I32)
    tile_expert = jnp.minimum(jnp.sum((tile_end[None, :] <= tile_ids[:, None]).astype(I32), axis=1), N_EXPERTS - 1)
    run = (jnp.arange(t, dtype=I32) // td)[:, None] * N_EXPERTS + idx
    pos = (jnp.take(seg_dst.reshape(-1), run) + rank).reshape(-1).astype(I32)
    lp = jnp.take(seg_src.reshape(-1), run) + rank
    lp = jnp.swapaxes(lp.reshape(n_tok_tiles, td, TOP_K), 1, 2).astype(I32)
    n_used = tile_end[-1:].astype(I32)
    flat = lambda a: a.reshape(-1).astype(I32)
    xs = _dispatch(flat(seg), flat(seg_src), flat(seg_dst), start + rows, tiles * TM_FFN - rows, n_used, lp, h,
                   n_tiles * TM_FFN)
    y = _expert_ffn(tile_expert, n_used, xs, w_gu, b_gu, w_down, b_down, layer)
    return _combine(pos, gates, h, g, b, y)


def _router_params(router_w, router_b):
    rw = jnp.pad(router_w, ((0, 0), (0, LANES - N_EXPERTS))).astype(BF16)
    rb = jnp.pad(router_b, (0, LANES - N_EXPERTS)).reshape(1, LANES)
    return rw, rb


def _alibi_slopes(n):
    return 2.0 ** (-8.0 * jnp.arange(1, n + 1, dtype=F32) / n)


def _even_layer(h, bsz, seq, w_in, w_out, lq1, lk1, lq2, lk2, subw, layer, g, b, rw, rb):
    proj = _project(h, [w_in.astype(BF16)], [BF16])[0].reshape(bsz, seq, -1)
    o_a = _sb_attention(proj, bsz, seq)
    lam_init = 0.8 - 0.6 * math.exp(-0.3 * layer)
    row = lambda v: v.reshape(1, -1)
    o_b = _diff_attention(proj, _alibi_slopes(DIFF_HEADS), row(lq1), row(lk1), row(lq2), row(lk2), row(subw),
                          bsz, seq, lam_init)
    t = bsz * seq
    sbw = SB_HEADS * HEAD_DIM
    wo = w_out.astype(BF16)
    return _mix_call(_mix_even_kernel, "mix_even", [o_a.reshape(t, -1), o_b.reshape(t, -1)],
                     [wo[:sbw], wo[sbw:]], h, row(g), row(b), rw, rb)


def _odd_layer(h, bsz, seq, w_in, kvw, w_uv, w_out, g, b, rw, rb):
    qw = DSA_HEADS * DSA_LATENT
    c0, c1 = qw, qw + DSA_LATENT
    i0, i1 = c1, c1 + IDX_HEADS * IDX_DIM
    k1 = i1 + IDX_DIM
    w_main = jnp.concatenate([w_in[:, :qw], w_in[:, i0:i1], w_in[:, i1:k1], w_in[:, i1:k1]], axis=1).astype(BF16)
    w_small = jnp.pad(jnp.concatenate([w_in[:, c0:c1], w_in[:, k1:]], axis=1),
                      ((0, 0), (0, LANES - IDX_HEADS))).astype(BF16)
    proj, small = [p.reshape(bsz, seq, -1) for p in _project(h, [w_main, w_small], [BF16, F32])]
    o = _dsa_attention(proj, small, kvw.reshape(1, -1), w_uv, bsz, seq)
    row = lambda v: v.reshape(1, -1)
    return _mix_call(_mix_odd_kernel, "mix_odd", [o.reshape(bsz * seq, -1)], [w_out.astype(BF16)],
                     h, row(g), row(b), rw, rb)


def kernel(x, ev_w_in, ev_w_out, ev_lambda_q1, ev_lambda_k1, ev_lambda_q2, ev_lambda_k2, ev_subln_w, od_w_in, od_kv_norm_w, od_w_uv, od_w_out, ln_mix_g, ln_mix_b, router_w, router_b, exp_w_gu, exp_b_gu, exp_w_down, exp_b_down, ln_ffn_g, ln_ffn_b):
    bsz, seq, d = x.shape
    h = x.reshape(bsz * seq, d)
    for layer in range(ln_mix_g.shape[0]):
        j = layer // 2
        rw, rb = _router_params(router_w[layer], router_b[layer])
        if layer % 2 == 0:
            h, logits = _even_layer(h, bsz, seq, ev_w_in[j], ev_w_out[j], ev_lambda_q1[j], ev_lambda_k1[j],
                                    ev_lambda_q2[j], ev_lambda_k2[j], ev_subln_w[j], layer,
                                    ln_mix_g[layer], ln_mix_b[layer], rw, rb)
        else:
            h, logits = _odd_layer(h, bsz, seq, od_w_in[j], od_kv_norm_w[j], od_w_uv[j], od_w_out[j],
                                   ln_mix_g[layer], ln_mix_b[layer], rw, rb)
        h = _moe(h, logits, exp_w_gu, exp_b_gu, exp_w_down, exp_b_down,
                 ln_ffn_g[layer].reshape(1, -1), ln_ffn_b[layer].reshape(1, -1), layer)
    return h.reshape(bsz, seq, d)
```

```python
import functools
import math

import jax
import jax.numpy as jnp
from jax import lax
from jax.experimental import pallas as pl
from jax.experimental.pallas import tpu as pltpu

F32, BF16, I32 = jnp.float32, jnp.bfloat16, jnp.int32

HEAD_DIM = 64
SB_HEADS = 8
DIFF_HEADS = 4
DIFF_SUB = 64
DIFF_VDIM = 128
DSA_HEADS = 16
DSA_LATENT = 128
DSA_VDIM = 64
IDX_HEADS = 8
IDX_DIM = 64
TOPK_MAX = 256
N_EXPERTS = 32
TOP_K = 4
SWIGLU_LIMIT = 7.0
SWIGLU_ALPHA = 1.702
LN_EPS = 1e-5
RMS_EPS = 1e-5
DEPTH = 2
DEEPNORM_ALPHA = (2 * DEPTH) ** 0.25

LANES = 128
SUBLANES = 8
BF16_ROWS = 16
VMEM_LIMIT = 56 * 1024 * 1024
INT_MIN = -(2 ** 31)
MASKED = -1e30
DEAD_LOG2 = -152.0
LOG2_E = math.log2(math.e)

TM_PROJ = 512
TQ_ATT = 256
TQ_DSA = 128
TRI = 512
HG_DSA = 16
TM_FFN = 512
TD_DISPATCH = 512
TC_COMBINE = 256


def _cparams(*sem):
    return pltpu.CompilerParams(dimension_semantics=sem, vmem_limit_bytes=VMEM_LIMIT)


def _nt_dot(a, b):
    return lax.dot_general(a, b, (((1,), (1,)), ((), ())), preferred_element_type=F32)


def _dot(a, b):
    return jnp.dot(a, b, preferred_element_type=F32)


def _layer_norm(y, g, b):
    mu = jnp.mean(y, axis=-1, keepdims=True)
    d = y - mu
    var = jnp.mean(d * d, axis=-1, keepdims=True)
    return d * lax.rsqrt(var + LN_EPS) * g + b


def _col_chunk(n):
    for c in (512, 384, 256, 128):
        if n % c == 0:
            return c
    raise ValueError(n)


def _proj_kernel(x_ref, *refs):
    n_out = len(refs) // 2
    x = x_ref[...].astype(BF16)
    for w_ref, o_ref in zip(refs[:n_out], refs[n_out:]):
        n = o_ref.shape[-1]
        c = _col_chunk(n)
        for j in range(0, n, c):
            o_ref[:, j:j + c] = _dot(x, w_ref[:, j:j + c]).astype(o_ref.dtype)


def _project(x, weights, out_dtypes):
    t, k = x.shape
    tm = min(TM_PROJ, t)
    return pl.pallas_call(
        _proj_kernel,
        grid=(t // tm,),
        in_specs=[pl.BlockSpec((tm, k), lambda i: (i, 0))]
                 + [pl.BlockSpec(w.shape, lambda i: (0, 0)) for w in weights],
        out_specs=[pl.BlockSpec((tm, w.shape[1]), lambda i: (i, 0)) for w in weights],
        out_shape=[jax.ShapeDtypeStruct((t, w.shape[1]), dt) for w, dt in zip(weights, out_dtypes)],
        compiler_params=_cparams("parallel"),
        name="in_proj",
    )(x, *weights)


def _stack_halves(x, lane):
    zero = jnp.zeros_like(x)
    return jnp.concatenate([jnp.where(lane < HEAD_DIM, x, zero), jnp.where(lane >= HEAD_DIM, x, zero)], axis=0)


def _sb_kernel(q_ref, k_ref, v_ref, o_ref, *, tq):
    i = pl.program_id(2)
    scale = HEAD_DIM ** -0.5
    lane = lax.broadcasted_iota(I32, (1, LANES), 1)
    row = lax.broadcasted_iota(I32, (2 * tq, tq), 0)
    row = jnp.where(row >= tq, row - tq, row)
    col = lax.broadcasted_iota(I32, (2 * tq, tq), 1)
    strict = col < row
    r1 = lax.broadcasted_iota(I32, (tq, tq), 0)
    c1 = lax.broadcasted_iota(I32, (tq, tq), 1)
    after = (r1 > c1).astype(BF16)
    q2 = _stack_halves(q_ref[0], lane)

    def block(j, carry, masked):
        run, acc = carry
        start = pl.multiple_of(j * tq, tq)
        kb = k_ref[0, pl.ds(start, tq), :]
        vb = v_ref[0, pl.ds(start, tq), :]
        z2 = _nt_dot(q2, kb) * (scale * LOG2_E)
        neg = -z2
        keep2 = jnp.minimum(neg, 0.0) - jnp.log2(1.0 + jnp.exp2(jnp.minimum(z2, neg)))
        lk = jnp.where(strict, keep2, 0.0) if masked else keep2
        hi = lk.astype(BF16)
        lo = (lk - hi.astype(F32)).astype(BF16)
        suffix = _dot(hi, after) + _dot(lo, after)
        w = jnp.exp2(z2 + keep2 + suffix + run)
        if masked:
            w = jnp.where(strict, w, 0.0)
        wb = w.astype(BF16)
        zero = jnp.zeros_like(vb)
        acc = (acc + _dot(wb[:tq], jnp.where(lane < HEAD_DIM, vb, zero))
               + _dot(wb[tq:], jnp.where(lane >= HEAD_DIM, vb, zero)))
        run = run + suffix[:, 0:1] + lk[:, 0:1]
        return run, acc

    def alive(c):
        return jnp.logical_and(c[0] >= 0, c[1] > DEAD_LOG2)

    def step(c):
        j, _, run, acc = c
        run, acc = block(j, (run, acc), False)
        return j - 1, jnp.max(run), run, acc

    start = (jnp.zeros((2 * tq, 1), F32), jnp.zeros((tq, LANES), F32))

    @pl.when(i == 0)
    def _():
        o_ref[0] = block(0, start, True)[1].astype(o_ref.dtype)

    @pl.when(i > 0)
    def _():
        run, acc = block(i - 1, block(i, start, True), False)
        _, _, _, acc = lax.while_loop(alive, step, (i - 2, jnp.max(run), run, acc))
        o_ref[0] = acc.astype(o_ref.dtype)


def _sb_attention(proj, bsz, seq):
    tq = min(TQ_ATT, seq)
    pairs = SB_HEADS * HEAD_DIM // LANES
    return pl.pallas_call(
        functools.partial(_sb_kernel, tq=tq),
        grid=(bsz, pairs, seq // tq),
        in_specs=[pl.BlockSpec((1, tq, LANES), lambda b, p, i: (b, i, p)),
                  pl.BlockSpec((1, seq, LANES), lambda b, p, i: (b, 0, pairs + p)),
                  pl.BlockSpec((1, seq, LANES), lambda b, p, i: (b, 0, 2 * pairs + p))],
        out_specs=pl.BlockSpec((1, tq, LANES), lambda b, p, i: (b, i, p)),
        out_shape=jax.ShapeDtypeStruct((bsz, seq, SB_HEADS * HEAD_DIM), BF16),
        compiler_params=_cparams("parallel", "parallel", "parallel"),
        name="stickbreak_attn",
    )(proj, proj, proj)


def _diff_kernel(slope_ref, lq1_ref, lk1_ref, lq2_ref, lk2_ref, subw_ref, q_ref, k_ref, v_ref, o_ref,
                 *, tq, lam_init):
    h = pl.program_id(1)
    i = pl.program_id(2)
    scale = DIFF_SUB ** -0.5
    slope = slope_ref[h]
    lam = (jnp.exp(jnp.sum(lq1_ref[...] * lk1_ref[...], keepdims=True))
           - jnp.exp(jnp.sum(lq2_ref[...] * lk2_ref[...], keepdims=True)) + lam_init)
    lane = lax.broadcasted_iota(I32, (1, LANES), 1)
    row = lax.broadcasted_iota(I32, (2 * tq, tq), 0)
    row = jnp.where(row >= tq, row - tq, row)
    col = lax.broadcasted_iota(I32, (2 * tq, tq), 1)
    causal = col <= row
    key_off = lax.broadcasted_iota(I32, (1, tq), 1)
    q2 = _stack_halves(q_ref[0], lane)

    def block(j, carry, masked):
        m, l, acc = carry
        start = j * tq
        kb = k_ref[0, pl.ds(start, tq), :]
        vb = v_ref[0, pl.ds(start, tq), :]
        s = _nt_dot(q2, kb) * (scale * LOG2_E) + (slope * LOG2_E) * (start + key_off).astype(F32)
        if masked:
            s = jnp.where(causal, s, -jnp.inf)
        m_new = jnp.maximum(m, jnp.max(s, axis=-1, keepdims=True))
        a = jnp.exp2(m - m_new)
        p = jnp.exp2(s - m_new)
        l = a * l + jnp.sum(p, axis=-1, keepdims=True)
        acc = a * acc + _dot(p.astype(BF16), vb)
        return m_new, l, acc

    for n_left in range(k_ref.shape[1] // tq):
        @pl.when(i == n_left)
        def _(n_left=n_left):
            carry = (jnp.full((2 * tq, 1), -jnp.inf, F32), jnp.zeros((2 * tq, 1), F32),
                     jnp.zeros((2 * tq, LANES), F32))
            carry = block(n_left, carry, True)
            for j in range(n_left):
                carry = block(j, carry, False)
            out = carry[2] / carry[1]
            o = out[:tq] - lam * out[tq:]
            o = o * lax.rsqrt(jnp.mean(o * o, axis=-1, keepdims=True) + RMS_EPS) * subw_ref[...]
            o_ref[0] = (o * (1.0 - lam_init)).astype(o_ref.dtype)


def _diff_attention(proj, slopes, lq1, lk1, lq2, lk2, subw, bsz, seq, lam_init):
    tq = min(TQ_ATT, seq)
    q0 = 3 * SB_HEADS * HEAD_DIM // LANES
    k0 = q0 + DIFF_HEADS
    v0 = k0 + DIFF_HEADS
    vec = lambda n: pl.BlockSpec((1, n), lambda b, h, i: (0, 0))
    return pl.pallas_call(
        functools.partial(_diff_kernel, tq=tq, lam_init=lam_init),
        grid=(bsz, DIFF_HEADS, seq // tq),
        in_specs=[pl.BlockSpec(memory_space=pltpu.SMEM),
                  vec(DIFF_SUB), vec(DIFF_SUB), vec(DIFF_SUB), vec(DIFF_SUB), vec(DIFF_VDIM),
                  pl.BlockSpec((1, tq, LANES), lambda b, h, i: (b, i, q0 + h)),
                  pl.BlockSpec((1, seq, LANES), lambda b, h, i: (b, 0, k0 + h)),
                  pl.BlockSpec((1, seq, LANES), lambda b, h, i: (b, 0, v0 + h))],
        out_specs=pl.BlockSpec((1, tq, LANES), lambda b, h, i: (b, i, h)),
        out_shape=jax.ShapeDtypeStruct((bsz, seq, DIFF_HEADS * DIFF_VDIM), BF16),
        compiler_params=_cparams("parallel", "parallel", "parallel"),
        name="diff_attn",
    )(slopes, lq1, lk1, lq2, lk2, subw, proj, proj, proj)


def _tree_sum_rows(x):
    n = x.shape[0] // 8
    x = x.reshape(n, 8, x.shape[1])
    while n > 1:
        n //= 2
        x = x[:n] + x[n:]
    return x[0]


def _tree_max_rows(x):
    n = x.shape[0] // 8
    x = x.reshape(n, 8, x.shape[1])
    while n > 1:
        n //= 2
        x = jnp.maximum(x[:n], x[n:])
    return x[0]


def _dsa_kernel(q_ref, ckv_ref, kvw_ref, qi_ref, ki_ref, wi_ref, qx_ref, kx_ref, wup_ref, o_ref,
                cn_ref, cnt_ref, skey_ref, bias_ref, s_ref, *, tq, tk, k_sel, hg):
    i = pl.program_id(1)
    t0 = i * tq

    @pl.when(i == 0)
    def _():
        c = ckv_ref[0]
        cn = c * lax.rsqrt(jnp.mean(c * c, axis=-1, keepdims=True) + RMS_EPS) * kvw_ref[...]
        cn_ref[:, :DSA_LATENT] = cn.astype(BF16)
        cn_ref[:, DSA_LATENT:] = kx_ref[...]
        cnt_ref[:DSA_LATENT, :] = cn.T.astype(BF16)
        row = lax.broadcasted_iota(I32, (BF16_ROWS, cnt_ref.shape[1]), 0)
        cnt_ref[DSA_LATENT:, :] = jnp.where(row == 0, 1.0, 0.0).astype(BF16)

    for k in range(ckv_ref.shape[1] // tk):
        @pl.when((t0 + tq - 1) // tk == k)
        def _(k=k):
            _dsa_block(q_ref, qi_ref, ki_ref, wi_ref, qx_ref, wup_ref, o_ref, cn_ref, cnt_ref, skey_ref, bias_ref,
                       s_ref, t0=t0, n_tiles=k + 1, tq=tq, tk=tk, k_sel=k_sel, hg=hg)


def _static_loop(n, body, carry):
    for j in range(n):
        carry = body(j, carry)
    return carry


def _dsa_block(q_ref, qi_ref, ki_ref, wi_ref, qx_ref, wup_ref, o_ref, cn_ref, cnt_ref, skey_ref, bias_ref, s_ref,
               *, t0, n_tiles, tq, tk, k_sel, hg):
    def tile_slice(j):
        return pl.ds(j * tk, tk)

    lane = lax.broadcasted_iota(I32, (1, LANES), 1)
    q_pos = t0 + lax.broadcasted_iota(I32, (tk, tq), 1)
    key_off = lax.broadcasted_iota(I32, (tk, tq), 0)

    w_t = (wi_ref[0] * ((IDX_HEADS ** -0.5) * (IDX_DIM ** -0.5))).T
    stacked = []
    for h in range(IDX_HEADS):
        qp = qi_ref[0, :, (h // 2) * LANES:(h // 2 + 1) * LANES]
        stacked.append(jnp.where((lane // IDX_DIM) == (h % 2), qp, jnp.zeros_like(qp)))
    q8 = jnp.concatenate(stacked, axis=0)

    def score_tile(j, _):
        d = _nt_dot(ki_ref[0, tile_slice(j), :], q8)
        score = jnp.zeros((tk, tq), F32)
        for h in range(IDX_HEADS):
            score = score + jnp.maximum(d[:, h * tq:(h + 1) * tq], 0.0) * w_t[h:h + 1, :]
        score = jnp.where(score == 0.0, 0.0, score)
        bits = pltpu.bitcast(score, I32)
        skey = jnp.where(bits < 0, bits ^ 0x7FFFFFFF, bits)
        skey_ref[tile_slice(j), :] = jnp.where(j * tk + key_off <= q_pos, skey, INT_MIN)
        return 0

    _static_loop(n_tiles, score_tile, 0)

    def count(pred):
        def tile(j, cnt):
            return cnt + _tree_sum_rows(pred(skey_ref[tile_slice(j), :]).astype(F32))
        cnt = _static_loop(n_tiles, tile, jnp.zeros((8, tq), F32))
        return jnp.sum(cnt, axis=0, keepdims=True)

    kf = float(k_sel)
    thr = jnp.where(count(lambda sk: sk >= 0) >= kf, 0, INT_MIN).astype(I32)

    def bit_step(b, thr):
        cand = thr + jnp.left_shift(jnp.int32(1), 30 - b)
        return jnp.where(count(lambda sk: sk >= cand) >= kf, cand, thr)

    thr = lax.fori_loop(0, 31, bit_step, thr)
    need = kf - count(lambda sk: sk > thr)

    r2 = lax.broadcasted_iota(I32, (tk, tk), 0)
    c2 = lax.broadcasted_iota(I32, (tk, tk), 1)
    upto = (c2 <= r2).astype(BF16)

    def tie_tile(j, seen):
        sk = skey_ref[tile_slice(j), :]
        tied = sk == thr
        prefix = _dot(upto, tied.astype(F32).astype(BF16)) + seen
        selected = ((sk > thr) | (tied & (prefix <= need))) & (j * tk + key_off <= q_pos)
        bias_ref[tile_slice(j), :] = jnp.where(selected, 0.0, MASKED)
        return prefix[tk - 1:tk, :]

    _static_loop(n_tiles, tie_tile, jnp.zeros((1, tq), F32))

    log2_scale = (DSA_LATENT ** -0.5) * LOG2_E

    def head_group(g, _):
        heads = [g * hg + u for u in range(hg)]
        offs = [pl.multiple_of(h * DSA_LATENT, DSA_LATENT) for h in heads]
        qg = jnp.concatenate(
            [jnp.concatenate([q_ref[0, :, pl.ds(off, DSA_LATENT)],
                              jnp.broadcast_to(qx_ref[pl.ds(h, 1), :], (tq, LANES)).astype(BF16)], axis=1)
             for h, off in zip(heads, offs)], axis=0)

        def logits_tile(j, m):
            d = _nt_dot(cn_ref[tile_slice(j), :], qg)
            b = bias_ref[tile_slice(j), :]
            tops = []
            for u in range(hg):
                s = d[:, u * tq:(u + 1) * tq] * log2_scale + b
                s_ref[tile_slice(j), u * tq:(u + 1) * tq] = s
                tops.append(jnp.max(_tree_max_rows(s), axis=0, keepdims=True))
            return jnp.maximum(m, jnp.concatenate(tops, axis=1))

        m = _static_loop(n_tiles, logits_tile, jnp.full((1, hg * tq), MASKED, F32))

        def value_tile(j, acc):
            p = jnp.exp2((s_ref[tile_slice(j), :] - m).astype(BF16))
            return acc + _dot(cnt_ref[:, tile_slice(j)], p)

        acc = _static_loop(n_tiles, value_tile, jnp.zeros((DSA_LATENT + BF16_ROWS, hg * tq), F32))
        out = (acc[:DSA_LATENT] / acc[DSA_LATENT:DSA_LATENT + 1]).astype(BF16)
        for u in range(0, hg, 2):
            pair = jnp.concatenate([out[:, u * tq:(u + 1) * tq], out[:, (u + 1) * tq:(u + 2) * tq]], axis=0)
            up = _dot(wup_ref[(g * hg + u) // 2], pair)
            off = pl.multiple_of((g * hg + u) * DSA_VDIM, 2 * DSA_VDIM)
            o_ref[0, :, pl.ds(off, 2 * DSA_VDIM)] = up.T.astype(o_ref.dtype)
        return 0

    lax.fori_loop(0, DSA_HEADS // hg, head_group, 0)


def _bf16_parts(x, n):
    parts = []
    for _ in range(n):
        p = x.astype(BF16).astype(F32)
        parts.append(p)
        x = x - p
    return parts


def _dsa_attention(proj, small, kvw, w_uv, bsz, seq):
    tq = min(TQ_DSA, seq)
    tk = min(TRI, seq)
    k_sel = min(TOPK_MAX, seq // 4)
    qw = DSA_HEADS * DSA_LATENT
    qiw = IDX_HEADS * IDX_DIM
    slope = _alibi_slopes(DSA_HEADS) / (DSA_LATENT ** -0.5)
    s_parts = _bf16_parts(slope, 3)
    pos = jnp.arange(seq, dtype=I32)
    p_parts = [(pos // LANES * LANES).astype(F32), (pos % LANES).astype(F32)]
    qx = jnp.stack([sp for _ in p_parts for sp in s_parts], axis=1)
    kx = jnp.stack([pp for pp in p_parts for _ in s_parts], axis=1)
    qx = jnp.pad(qx, ((0, 0), (0, LANES - qx.shape[1])))
    kx = jnp.pad(kx, ((0, 0), (0, LANES - kx.shape[1]))).astype(BF16)
    wt = jnp.swapaxes(w_uv, 1, 2).reshape(DSA_HEADS // 2, 2, DSA_VDIM, DSA_LATENT)
    zero = jnp.zeros_like(wt[:, 0])
    wup = jnp.concatenate([jnp.concatenate([wt[:, 0], zero], axis=2),
                           jnp.concatenate([zero, wt[:, 1]], axis=2)], axis=1).astype(BF16)
    return pl.pallas_call(
        functools.partial(_dsa_kernel, tq=tq, tk=tk, k_sel=k_sel, hg=HG_DSA),
        grid=(bsz, seq // tq),
        in_specs=[pl.BlockSpec((1, tq, qw), lambda b, i: (b, i, 0)),
                  pl.BlockSpec((1, seq, DSA_LATENT), lambda b, i: (b, 0, 0)),
                  pl.BlockSpec((1, DSA_LATENT), lambda b, i: (0, 0)),
                  pl.BlockSpec((1, tq, qiw), lambda b, i: (b, i, qw // qiw)),
                  pl.BlockSpec((1, seq, LANES), lambda b, i: (b, 0, (qw + qiw) // LANES)),
                  pl.BlockSpec((1, tq, LANES), lambda b, i: (b, i, 1)),
                  pl.BlockSpec((DSA_HEADS, LANES), lambda b, i: (0, 0)),
                  pl.BlockSpec((seq, LANES), lambda b, i: (0, 0)),
                  pl.BlockSpec(wup.shape, lambda b, i: (0, 0, 0))],
        out_specs=pl.BlockSpec((1, tq, DSA_HEADS * DSA_VDIM), lambda b, i: (b, i, 0)),
        out_shape=jax.ShapeDtypeStruct((bsz, seq, DSA_HEADS * DSA_VDIM), BF16),
        scratch_shapes=[pltpu.VMEM((seq, DSA_LATENT + LANES), BF16),
                        pltpu.VMEM((DSA_LATENT + BF16_ROWS, seq), BF16),
                        pltpu.VMEM((seq, tq), I32), pltpu.VMEM((seq, tq), F32),
                        pltpu.VMEM((seq, HG_DSA * tq), F32)],
        compiler_params=_cparams("parallel", "arbitrary"),
        name="dsa_attn",
    )(proj, small, kvw, proj, proj, small, qx, kx, wup)


def _mix_tail(mix, h_ref, g_ref, b_ref, rw_ref, rb_ref, hout_ref, logit_ref):
    hn = _layer_norm(DEEPNORM_ALPHA * h_ref[...] + mix, g_ref[...], b_ref[...])
    hout_ref[...] = hn
    logit_ref[...] = _dot(hn.astype(BF16), rw_ref[...]) + rb_ref[...]


def _mix_even_kernel(oa_ref, ob_ref, wa_ref, wb_ref, h_ref, g_ref, b_ref, rw_ref, rb_ref, hout_ref, logit_ref):
    mix = _dot(oa_ref[...], wa_ref[...]) + _dot(ob_ref[...], wb_ref[...])
    _mix_tail(mix, h_ref, g_ref, b_ref, rw_ref, rb_ref, hout_ref, logit_ref)


def _mix_odd_kernel(o_ref, wo_ref, h_ref, g_ref, b_ref, rw_ref, rb_ref, hout_ref, logit_ref):
    _mix_tail(_dot(o_ref[...], wo_ref[...]), h_ref, g_ref, b_ref, rw_ref, rb_ref, hout_ref, logit_ref)


def _mix_call(kernel, name, acts, weights, h, g, b, rw, rb):
    t, d = h.shape
    tm = min(TM_PROJ, t)
    row = lambda n: pl.BlockSpec((tm, n), lambda i: (i, 0))
    full = lambda a: pl.BlockSpec(a.shape, lambda i: (0, 0))
    return pl.pallas_call(
        kernel,
        grid=(t // tm,),
        in_specs=[row(a.shape[1]) for a in acts] + [full(w) for w in weights]
                 + [row(d), full(g), full(b), full(rw), full(rb)],
        out_specs=[row(d), row(LANES)],
        out_shape=[jax.ShapeDtypeStruct((t, d), F32), jax.ShapeDtypeStruct((t, LANES), F32)],
        compiler_params=_cparams("parallel"),
        name=name,
    )(*acts, *weights, h, g, b, rw, rb)


def _route_kernel(logit_ref, idx_ref, gate_ref, rank_ref, cnt_ref, *, tm):
    lane = lax.broadcasted_iota(I32, (tm, LANES), 1)
    lane_f = lane.astype(F32)
    x = jnp.where(lane < N_EXPERTS, logit_ref[...], -jnp.inf)
    vals, hots = [], []
    for k in range(TOP_K):
        m = jnp.max(x, axis=-1, keepdims=True)
        first = jnp.min(jnp.where(x == m, lane_f, float(LANES)), axis=-1, keepdims=True)
        hot = lane_f == first
        x = jnp.where(hot, -jnp.inf, x)
        vals.append(m)
        hots.append(hot)
        idx_ref[:, k:k + 1] = first.astype(I32)
    exps = [jnp.exp(v - vals[0]) for v in vals]
    denom = exps[0] + exps[1] + exps[2] + exps[3]
    for k in range(TOP_K):
        gate_ref[:, k:k + 1] = exps[k] / denom

    hot_sum = (hots[0] | hots[1] | hots[2] | hots[3]).astype(F32)
    r = lax.broadcasted_iota(I32, (tm, tm), 0)
    c = lax.broadcasted_iota(I32, (tm, tm), 1)
    earlier = (c < r).astype(BF16)
    before = _dot(earlier, hot_sum.astype(BF16))
    for k in range(TOP_K):
        rank_ref[:, k:k + 1] = jnp.sum(jnp.where(hots[k], before, 0.0), axis=-1, keepdims=True).astype(I32)
    cnt_ref[0] = jnp.sum(hot_sum, axis=0, keepdims=True)


def _route(logits):
    t = logits.shape[0]
    tm = min(TD_DISPATCH, t)
    narrow = pl.BlockSpec((tm, TOP_K), lambda i: (i, 0))
    return pl.pallas_call(
        functools.partial(_route_kernel, tm=tm),
        grid=(t // tm,),
        in_specs=[pl.BlockSpec((tm, LANES), lambda i: (i, 0))],
        out_specs=[narrow, narrow, narrow, pl.BlockSpec((1, 1, LANES), lambda i: (i, 0, 0))],
        out_shape=[jax.ShapeDtypeStruct((t, TOP_K), I32), jax.ShapeDtypeStruct((t, TOP_K), F32),
                   jax.ShapeDtypeStruct((t, TOP_K), I32), jax.ShapeDtypeStruct((t // tm, 1, LANES), F32)],
        compiler_params=_cparams("parallel"),
        name="route",
    )(logits)


def _row_copy(src_ref, src_row, dst_ref, dst_row, sem):
    return pltpu.make_async_copy(src_ref.at[pl.ds(src_row, 1), :], dst_ref.at[pl.ds(dst_row, 1), :], sem)


def _chunked_copy(src_ref, src_off, dst_ref, dst_off, count, largest, sem, start_not_wait):
    chunk = largest
    while chunk >= SUBLANES:
        present = (count & chunk) != 0
        copy = pltpu.make_async_copy(src_ref.at[pl.ds(pl.multiple_of(src_off, SUBLANES), chunk), :],
                                     dst_ref.at[pl.ds(pl.multiple_of(dst_off, SUBLANES), chunk), :], sem)

        @pl.when(jnp.logical_and(present, start_not_wait))
        def _():
            copy.start()

        @pl.when(jnp.logical_and(present, jnp.logical_not(start_not_wait)))
        def _():
            copy.wait()

        step = jnp.where(present, chunk, 0)
        src_off = src_off + step
        dst_off = dst_off + step
        chunk //= 2


def _dispatch_kernel(seg_rows_ref, seg_src_ref, seg_dst_ref, pad_start_ref, pad_count_ref, n_used_ref,
                     lp_ref, x_ref, xs_out, staged_ref, zero_ref, sem, pad_sem, *, td):
    i = pl.program_id(0)
    tile = zero_ref.shape[0]

    @pl.when(i == 0)
    def _():
        zero_ref[...] = jnp.zeros_like(zero_ref)

        def pads(e, start_not_wait):
            _chunked_copy(zero_ref, 0, xs_out, pad_start_ref[e], pad_count_ref[e], tile // 2, pad_sem,
                          start_not_wait)
            return start_not_wait

        def tail(j, start_not_wait):
            present = j >= n_used_ref[0]
            copy = pltpu.make_async_copy(zero_ref, xs_out.at[pl.ds(pl.multiple_of(j * tile, tile), tile), :], pad_sem)

            @pl.when(jnp.logical_and(present, start_not_wait))
            def _():
                copy.start()

            @pl.when(jnp.logical_and(present, jnp.logical_not(start_not_wait)))
            def _():
                copy.wait()
            return start_not_wait

        n_tiles = xs_out.shape[0] // tile
        lax.fori_loop(0, N_EXPERTS, pads, True)
        lax.fori_loop(0, n_tiles, tail, True)
        lax.fori_loop(0, N_EXPERTS, pads, False)
        lax.fori_loop(0, n_tiles, tail, False)

    rows = staged_ref.shape[0]
    r_iota = lax.broadcasted_iota(I32, (rows, td), 0)
    hit = r_iota == lp_ref[0, 0:1, :]
    for k in range(1, TOP_K):
        hit = hit | (r_iota == lp_ref[0, k:k + 1, :])
    staged_ref[...] = _dot(jnp.where(hit, 1.0, 0.0).astype(BF16), x_ref[...].astype(BF16))

    def runs(e, start_not_wait):
        s = i * N_EXPERTS + e
        _chunked_copy(staged_ref, seg_src_ref[s], xs_out, seg_dst_ref[s], seg_rows_ref[s], td, sem, start_not_wait)
        return start_not_wait

    lax.fori_loop(0, N_EXPERTS, runs, True)
    lax.fori_loop(0, N_EXPERTS, runs, False)


def _dispatch(seg_rows, seg_src, seg_dst, pad_start, pad_count, n_used, lp, x, n_rows):
    t, d = x.shape
    td = min(TD_DISPATCH, t)
    staged_rows = td * TOP_K + N_EXPERTS * SUBLANES
    grid_spec = pltpu.PrefetchScalarGridSpec(
        num_scalar_prefetch=6,
        grid=(t // td,),
        in_specs=[pl.BlockSpec((1, TOP_K, td), lambda i, *_: (i, 0, 0)),
                  pl.BlockSpec((td, d), lambda i, *_: (i, 0))],
        out_specs=pl.BlockSpec(memory_space=pl.ANY),
        scratch_shapes=[pltpu.VMEM((staged_rows, d), F32), pltpu.VMEM((TM_FFN, d), F32),
                        pltpu.SemaphoreType.DMA, pltpu.SemaphoreType.DMA],
    )
    return pl.pallas_call(
        functools.partial(_dispatch_kernel, td=td),
        grid_spec=grid_spec,
        out_shape=jax.ShapeDtypeStruct((n_rows, d), F32),
        compiler_params=_cparams("arbitrary"),
        name="moe_dispatch",
    )(seg_rows, seg_src, seg_dst, pad_start, pad_count, n_used, lp, x)


def _ffn_kernel(tile_expert_ref, n_used_ref, xs_ref, wgu_ref, bgu_ref, wd_ref, bd_ref, y_ref, wgu_bf, wd_bf):
    i = pl.program_id(0)
    d_ff = wd_ref.shape[2]
    new_expert = jnp.logical_or(i == 0, tile_expert_ref[i] != tile_expert_ref[jnp.maximum(i - 1, 0)])

    @pl.when(jnp.logical_and(new_expert, i < n_used_ref[0]))
    def _():
        wgu_bf[...] = wgu_ref[0, 0].astype(BF16)
        wd_bf[...] = wd_ref[0, 0].astype(BF16)

    @pl.when(i < n_used_ref[0])
    def _():
        x = xs_ref[...].astype(BF16)
        hgu = _dot(x, wgu_bf[...]) + bgu_ref[0, 0]
        gate = jnp.minimum(hgu[:, :d_ff], SWIGLU_LIMIT)
        up = jnp.clip(hgu[:, d_ff:], -SWIGLU_LIMIT, SWIGLU_LIMIT)
        act = gate * (1.0 / (1.0 + jnp.exp(-SWIGLU_ALPHA * gate))) * (up + 1.0)
        y_ref[...] = _dot(act.astype(BF16), wd_bf[...]) + bd_ref[0, 0]

    @pl.when(i >= n_used_ref[0])
    def _():
        y_ref[...] = jnp.zeros_like(y_ref)


def _expert_ffn(tile_expert, n_used, xs, w_gu, b_gu, w_down, b_down, layer):
    n_rows, d = xs.shape
    depth, e, _, f2 = w_gu.shape
    d_ff = w_down.shape[2]
    tm = TM_FFN
    grid_spec = pltpu.PrefetchScalarGridSpec(
        num_scalar_prefetch=2,
        grid=(n_rows // tm,),
        in_specs=[pl.BlockSpec((tm, d), lambda i, te, nu: (jnp.minimum(i, nu[0] - 1), 0)),
                  pl.BlockSpec((1, 1, d, f2), lambda i, te, nu: (layer, te[i], 0, 0)),
                  pl.BlockSpec((1, 1, 1, f2), lambda i, te, nu: (layer, te[i], 0, 0)),
                  pl.BlockSpec((1, 1, d_ff, d), lambda i, te, nu: (layer, te[i], 0, 0)),
                  pl.BlockSpec((1, 1, 1, d), lambda i, te, nu: (layer, te[i], 0, 0))],
        out_specs=pl.BlockSpec((tm, d), lambda i, te, nu: (i, 0)),
        scratch_shapes=[pltpu.VMEM((d, f2), BF16), pltpu.VMEM((d_ff, d), BF16)],
    )
    return pl.pallas_call(
        _ffn_kernel,
        grid_spec=grid_spec,
        out_shape=jax.ShapeDtypeStruct((n_rows, d), F32),
        compiler_params=_cparams("arbitrary"),
        name="expert_ffn",
    )(tile_expert, n_used, xs, w_gu, b_gu.reshape(depth, e, 1, f2), w_down, b_down.reshape(depth, e, 1, d))


def _combine_kernel(pos_ref, gate_ref, h_ref, g_ref, b_ref, y_hbm, out_ref, buf, sem, *, tc):
    def issue(i, _):
        for k in range(TOP_K):
            _row_copy(y_hbm, pos_ref[i * TOP_K + k], buf.at[k], i, sem).start()
        return 0

    lax.fori_loop(0, tc, issue, 0)

    def drain(i, _):
        for k in range(TOP_K):
            _row_copy(y_hbm, pos_ref[i * TOP_K + k], buf.at[k], i, sem).wait()
        return 0

    lax.fori_loop(0, tc, drain, 0)

    gates = gate_ref[...]
    ffn = gates[:, 0:1] * buf[0]
    for k in range(1, TOP_K):
        ffn = ffn + gates[:, k:k + 1] * buf[k]
    out_ref[...] = _layer_norm(DEEPNORM_ALPHA * h_ref[...] + ffn, g_ref[...], b_ref[...])


def _combine(pos_flat, gates, h, g, b, y):
    t, d = h.shape
    tc = min(TC_COMBINE, t)
    return pl.pallas_call(
        functools.partial(_combine_kernel, tc=tc),
        grid=(t // tc,),
        in_specs=[pl.BlockSpec((tc * TOP_K,), lambda i: (i,), memory_space=pltpu.SMEM),
                  pl.BlockSpec((tc, TOP_K), lambda i: (i, 0)),
                  pl.BlockSpec((tc, d), lambda i: (i, 0)),
                  pl.BlockSpec((1, d), lambda i: (0, 0)),
                  pl.BlockSpec((1, d), lambda i: (0, 0)),
                  pl.BlockSpec(memory_space=pl.ANY)],
        out_specs=pl.BlockSpec((tc, d), lambda i: (i, 0)),
        out_shape=jax.ShapeDtypeStruct((t, d), F32),
        scratch_shapes=[pltpu.VMEM((TOP_K, tc, d), F32), pltpu.SemaphoreType.DMA],
        compiler_params=_cparams("arbitrary"),
        name="moe_combine",
    )(pos_flat, gates, h, g, b, y)


def _moe(h, logits, w_gu, b_gu, w_down, b_down, g, b, layer):
    t = h.shape[0]
    td = min(TD_DISPATCH, t)
    n_tok_tiles = t // td
    idx, gates, rank, counts = _route(logits)
    counts = counts[:, 0, :N_EXPERTS].astype(I32)
    seg = (counts + SUBLANES - 1) // SUBLANES * SUBLANES
    rows = jnp.sum(seg, axis=0)
    tiles = (rows + TM_FFN - 1) // TM_FFN
    tile_end = jnp.cumsum(tiles)
    start = (tile_end - tiles) * TM_FFN
    seg_dst = start[None, :] + jnp.cumsum(seg, axis=0) - seg
    seg_src = jnp.cumsum(seg, axis=1) - seg
    n_tiles = -(-(t * TOP_K + n_tok_tiles * N_EXPERTS * (SUBLANES - 1)) // TM_FFN) + N_EXPERTS
    tile_ids = jnp.arange(n_tiles, dtype=I32)
    tile_expert = jnp.minimum(jnp.sum((tile_end[None, :] <= tile_ids[:, None]).astype(I32), axis=1), N_EXPERTS - 1)
    chosen = idx.reshape(n_tok_tiles, td, TOP_K, 1) == jnp.arange(N_EXPERTS, dtype=I32)
    pick = lambda table: jnp.sum(jnp.where(chosen, table[:, None, None, :], 0), axis=-1)
    rank = rank.reshape(n_tok_tiles, td, TOP_K)
    pos = (pick(seg_dst) + rank).reshape(-1).astype(I32)
    lp = jnp.swapaxes(pick(seg_src) + rank, 1, 2).astype(I32)
    n_used = tile_end[-1:].astype(I32)
    flat = lambda a: a.reshape(-1).astype(I32)
    xs = _dispatch(flat(seg), flat(seg_src), flat(seg_dst), start + rows, tiles * TM_FFN - rows, n_used, lp, h,
                   n_tiles * TM_FFN)
    y = _expert_ffn(tile_expert, n_used, xs, w_gu, b_gu, w_down, b_down, layer)
    return _combine(pos, gates, h, g, b, y)


def _router_params(router_w, router_b):
    rw = jnp.pad(router_w, ((0, 0), (0, LANES - N_EXPERTS))).astype(BF16)
    rb = jnp.pad(router_b, (0, LANES - N_EXPERTS)).reshape(1, LANES)
    return rw, rb


def _alibi_slopes(n):
    return 2.0 ** (-8.0 * jnp.arange(1, n + 1, dtype=F32) / n)


def _even_layer(h, bsz, seq, w_in, w_out, lq1, lk1, lq2, lk2, subw, layer, g, b, rw, rb):
    proj = _project(h, [w_in.astype(BF16)], [BF16])[0].reshape(bsz, seq, -1)
    o_a = _sb_attention(proj, bsz, seq)
    lam_init = 0.8 - 0.6 * math.exp(-0.3 * layer)
    row = lambda v: v.reshape(1, -1)
    o_b = _diff_attention(proj, _alibi_slopes(DIFF_HEADS), row(lq1), row(lk1), row(lq2), row(lk2), row(subw),
                          bsz, seq, lam_init)
    t = bsz * seq
    sbw = SB_HEADS * HEAD_DIM
    wo = w_out.astype(BF16)
    return _mix_call(_mix_even_kernel, "mix_even", [o_a.reshape(t, -1), o_b.reshape(t, -1)],
                     [wo[:sbw], wo[sbw:]], h, row(g), row(b), rw, rb)


def _odd_layer(h, bsz, seq, w_in, kvw, w_uv, w_out, g, b, rw, rb):
    qw = DSA_HEADS * DSA_LATENT
    c0, c1 = qw, qw + DSA_LATENT
    i0, i1 = c1, c1 + IDX_HEADS * IDX_DIM
    k1 = i1 + IDX_DIM
    w_main = jnp.concatenate([w_in[:, :qw], w_in[:, i0:i1], w_in[:, i1:k1], w_in[:, i1:k1]], axis=1).astype(BF16)
    w_small = jnp.pad(jnp.concatenate([w_in[:, c0:c1], w_in[:, k1:]], axis=1),
                      ((0, 0), (0, LANES - IDX_HEADS))).astype(BF16)
    proj, small = [p.reshape(bsz, seq, -1) for p in _project(h, [w_main, w_small], [BF16, F32])]
    o = _dsa_attention(proj, small, kvw.reshape(1, -1), w_uv, bsz, seq)
    row = lambda v: v.reshape(1, -1)
    return _mix_call(_mix_odd_kernel, "mix_odd", [o.reshape(bsz * seq, -1)], [w_out.astype(BF16)],
                     h, row(g), row(b), rw, rb)


def kernel(x, ev_w_in, ev_w_out, ev_lambda_q1, ev_lambda_k1, ev_lambda_q2, ev_lambda_k2, ev_subln_w, od_w_in, od_kv_norm_w, od_w_uv, od_w_out, ln_mix_g, ln_mix_b, router_w, router_b, exp_w_gu, exp_b_gu, exp_w_down, exp_b_down, ln_ffn_g, ln_ffn_b):
    bsz, seq, d = x.shape
    h = x.reshape(bsz * seq, d)
    for layer in range(ln_mix_g.shape[0]):
        j = layer // 2
        rw, rb = _router_params(router_w[layer], router_b[layer])
        if layer % 2 == 0:
            h, logits = _even_layer(h, bsz, seq, ev_w_in[j], ev_w_out[j], ev_lambda_q1[j], ev_lambda_k1[j],
                                    ev_lambda_q2[j], ev_lambda_k2[j], ev_subln_w[j], layer,
                                    ln_mix_g[layer], ln_mix_b[layer], rw, rb)
        else:
            h, logits = _odd_layer(h, bsz, seq, od_w_in[j], od_kv_norm_w[j], od_w_uv[j], od_w_out[j],
                                   ln_mix_g[layer], ln_mix_b[layer], rw, rb)
        h = _moe(h, logits, exp_w_gu, exp_b_gu, exp_w_down, exp_b_down,
                 ln_ffn_g[layer].reshape(1, -1), ln_ffn_b[layer].reshape(1, -1), layer)
    return h.reshape(bsz, seq, d)
```

```python
import functools
import math

import jax
import jax.numpy as jnp
from jax import lax
from jax.experimental import pallas as pl
from jax.experimental.pallas import tpu as pltpu

F32, BF16, I32 = jnp.float32, jnp.bfloat16, jnp.int32

HEAD_DIM = 64
SB_HEADS = 8
DIFF_HEADS = 4
DIFF_SUB = 64
DIFF_VDIM = 128
DSA_HEADS = 16
DSA_LATENT = 128
DSA_VDIM = 64
IDX_HEADS = 8
IDX_DIM = 64
TOPK_MAX = 256
N_EXPERTS = 32
TOP_K = 4
SWIGLU_LIMIT = 7.0
SWIGLU_ALPHA = 1.702
LN_EPS = 1e-5
RMS_EPS = 1e-5
DEPTH = 2
DEEPNORM_ALPHA = (2 * DEPTH) ** 0.25

LANES = 128
SUBLANES = 8
BF16_ROWS = 16
VMEM_LIMIT = 56 * 1024 * 1024
INT_MIN = -(2 ** 31)
MASKED = -1e30
DEAD_LOG2 = -152.0
LOG2_E = math.log2(math.e)

TM_PROJ = 512
TQ_ATT = 256
TQ_DSA = 128
TRI = 512
HG_DSA = 16
TM_FFN = 512
TD_DISPATCH = 512


def _cparams(*sem):
    return pltpu.CompilerParams(dimension_semantics=sem, vmem_limit_bytes=VMEM_LIMIT)


def _nt_dot(a, b):
    return lax.dot_general(a, b, (((1,), (1,)), ((), ())), preferred_element_type=F32)


def _dot(a, b):
    return jnp.dot(a, b, preferred_element_type=F32)


def _layer_norm(y, g, b):
    mu = jnp.mean(y, axis=-1, keepdims=True)
    d = y - mu
    var = jnp.mean(d * d, axis=-1, keepdims=True)
    return d * lax.rsqrt(var + LN_EPS) * g + b


def _col_chunk(n):
    for c in (512, 384, 256, 128):
        if n % c == 0:
            return c
    raise ValueError(n)


def _proj_kernel(x_ref, *refs):
    n_out = len(refs) // 2
    x = x_ref[...].astype(BF16)
    for w_ref, o_ref in zip(refs[:n_out], refs[n_out:]):
        n = o_ref.shape[-1]
        c = _col_chunk(n)
        for j in range(0, n, c):
            o_ref[:, j:j + c] = _dot(x, w_ref[:, j:j + c]).astype(o_ref.dtype)


def _project(x, weights, out_dtypes):
    t, k = x.shape
    tm = min(TM_PROJ, t)
    return pl.pallas_call(
        _proj_kernel,
        grid=(t // tm,),
        in_specs=[pl.BlockSpec((tm, k), lambda i: (i, 0))]
                 + [pl.BlockSpec(w.shape, lambda i: (0, 0)) for w in weights],
        out_specs=[pl.BlockSpec((tm, w.shape[1]), lambda i: (i, 0)) for w in weights],
        out_shape=[jax.ShapeDtypeStruct((t, w.shape[1]), dt) for w, dt in zip(weights, out_dtypes)],
        compiler_params=_cparams("parallel"),
        name="in_proj",
    )(x, *weights)


def _stack_halves(x, lane):
    zero = jnp.zeros_like(x)
    return jnp.concatenate([jnp.where(lane < HEAD_DIM, x, zero), jnp.where(lane >= HEAD_DIM, x, zero)], axis=0)


def _sb_kernel(q_ref, k_ref, v_ref, o_ref, *, tq):
    i = pl.program_id(2)
    scale = HEAD_DIM ** -0.5
    lane = lax.broadcasted_iota(I32, (1, LANES), 1)
    row = lax.broadcasted_iota(I32, (2 * tq, tq), 0)
    row = jnp.where(row >= tq, row - tq, row)
    col = lax.broadcasted_iota(I32, (2 * tq, tq), 1)
    strict = col < row
    r1 = lax.broadcasted_iota(I32, (tq, tq), 0)
    c1 = lax.broadcasted_iota(I32, (tq, tq), 1)
    after = (r1 > c1).astype(BF16)
    q2 = _stack_halves(q_ref[0], lane)

    def block(j, carry, masked):
        run, acc = carry
        start = pl.multiple_of(j * tq, tq)
        kb = k_ref[0, pl.ds(start, tq), :]
        vb = v_ref[0, pl.ds(start, tq), :]
        z2 = _nt_dot(q2, kb) * (scale * LOG2_E)
        neg = -z2
        keep2 = jnp.minimum(neg, 0.0) - jnp.log2(1.0 + jnp.exp2(jnp.minimum(z2, neg)))
        lk = jnp.where(strict, keep2, 0.0) if masked else keep2
        hi = lk.astype(BF16)
        lo = (lk - hi.astype(F32)).astype(BF16)
        suffix = _dot(hi, after) + _dot(lo, after)
        w = jnp.exp2(z2 + keep2 + suffix + run)
        if masked:
            w = jnp.where(strict, w, 0.0)
        wb = w.astype(BF16)
        zero = jnp.zeros_like(vb)
        acc = (acc + _dot(wb[:tq], jnp.where(lane < HEAD_DIM, vb, zero))
               + _dot(wb[tq:], jnp.where(lane >= HEAD_DIM, vb, zero)))
        run = run + suffix[:, 0:1] + lk[:, 0:1]
        return run, acc

    def alive(c):
        return jnp.logical_and(c[0] >= 0, c[1] > DEAD_LOG2)

    def step(c):
        j, _, run, acc = c
        run, acc = block(j, (run, acc), False)
        return j - 1, jnp.max(run), run, acc

    start = (jnp.zeros((2 * tq, 1), F32), jnp.zeros((tq, LANES), F32))

    @pl.when(i == 0)
    def _():
        o_ref[0] = block(0, start, True)[1].astype(o_ref.dtype)

    @pl.when(i > 0)
    def _():
        run, acc = block(i - 1, block(i, start, True), False)
        _, _, _, acc = lax.while_loop(alive, step, (i - 2, jnp.max(run), run, acc))
        o_ref[0] = acc.astype(o_ref.dtype)


def _sb_attention(proj, bsz, seq):
    tq = min(TQ_ATT, seq)
    pairs = SB_HEADS * HEAD_DIM // LANES
    return pl.pallas_call(
        functools.partial(_sb_kernel, tq=tq),
        grid=(bsz, pairs, seq // tq),
        in_specs=[pl.BlockSpec((1, tq, LANES), lambda b, p, i: (b, i, p)),
                  pl.BlockSpec((1, seq, LANES), lambda b, p, i: (b, 0, pairs + p)),
                  pl.BlockSpec((1, seq, LANES), lambda b, p, i: (b, 0, 2 * pairs + p))],
        out_specs=pl.BlockSpec((1, tq, LANES), lambda b, p, i: (b, i, p)),
        out_shape=jax.ShapeDtypeStruct((bsz, seq, SB_HEADS * HEAD_DIM), BF16),
        compiler_params=_cparams("parallel", "parallel", "parallel"),
        name="stickbreak_attn",
    )(proj, proj, proj)


def _diff_kernel(slope_ref, lq1_ref, lk1_ref, lq2_ref, lk2_ref, subw_ref, q_ref, k_ref, v_ref, o_ref,
                 *, tq, lam_init):
    h = pl.program_id(1)
    i = pl.program_id(2)
    scale = DIFF_SUB ** -0.5
    slope = slope_ref[h]
    lam = (jnp.exp(jnp.sum(lq1_ref[...] * lk1_ref[...], keepdims=True))
           - jnp.exp(jnp.sum(lq2_ref[...] * lk2_ref[...], keepdims=True)) + lam_init)
    lane = lax.broadcasted_iota(I32, (1, LANES), 1)
    row = lax.broadcasted_iota(I32, (2 * tq, tq), 0)
    row = jnp.where(row >= tq, row - tq, row)
    col = lax.broadcasted_iota(I32, (2 * tq, tq), 1)
    causal = col <= row
    key_off = lax.broadcasted_iota(I32, (1, tq), 1)
    q2 = _stack_halves(q_ref[0], lane)

    def block(j, carry, masked):
        m, l, acc = carry
        start = j * tq
        kb = k_ref[0, pl.ds(start, tq), :]
        vb = v_ref[0, pl.ds(start, tq), :]
        s = _nt_dot(q2, kb) * (scale * LOG2_E) + (slope * LOG2_E) * (start + key_off).astype(F32)
        if masked:
            s = jnp.where(causal, s, -jnp.inf)
        m_new = jnp.maximum(m, jnp.max(s, axis=-1, keepdims=True))
        a = jnp.exp2(m - m_new)
        p = jnp.exp2(s - m_new)
        l = a * l + jnp.sum(p, axis=-1, keepdims=True)
        acc = a * acc + _dot(p.astype(BF16), vb)
        return m_new, l, acc

    for n_left in range(k_ref.shape[1] // tq):
        @pl.when(i == n_left)
        def _(n_left=n_left):
            carry = (jnp.full((2 * tq, 1), -jnp.inf, F32), jnp.zeros((2 * tq, 1), F32),
                     jnp.zeros((2 * tq, LANES), F32))
            carry = block(n_left, carry, True)
            for j in range(n_left):
                carry = block(j, carry, False)
            out = carry[2] / carry[1]
            o = out[:tq] - lam * out[tq:]
            o = o * lax.rsqrt(jnp.mean(o * o, axis=-1, keepdims=True) + RMS_EPS) * subw_ref[...]
            o_ref[0] = (o * (1.0 - lam_init)).astype(o_ref.dtype)


def _diff_attention(proj, slopes, lq1, lk1, lq2, lk2, subw, bsz, seq, lam_init):
    tq = min(TQ_ATT, seq)
    q0 = 3 * SB_HEADS * HEAD_DIM // LANES
    k0 = q0 + DIFF_HEADS
    v0 = k0 + DIFF_HEADS
    vec = lambda n: pl.BlockSpec((1, n), lambda b, h, i: (0, 0))
    return pl.pallas_call(
        functools.partial(_diff_kernel, tq=tq, lam_init=lam_init),
        grid=(bsz, DIFF_HEADS, seq // tq),
        in_specs=[pl.BlockSpec(memory_space=pltpu.SMEM),
                  vec(DIFF_SUB), vec(DIFF_SUB), vec(DIFF_SUB), vec(DIFF_SUB), vec(DIFF_VDIM),
                  pl.BlockSpec((1, tq, LANES), lambda b, h, i: (b, i, q0 + h)),
                  pl.BlockSpec((1, seq, LANES), lambda b, h, i: (b, 0, k0 + h)),
                  pl.BlockSpec((1, seq, LANES), lambda b, h, i: (b, 0, v0 + h))],
        out_specs=pl.BlockSpec((1, tq, LANES), lambda b, h, i: (b, i, h)),
        out_shape=jax.ShapeDtypeStruct((bsz, seq, DIFF_HEADS * DIFF_VDIM), BF16),
        compiler_params=_cparams("parallel", "parallel", "parallel"),
        name="diff_attn",
    )(slopes, lq1, lk1, lq2, lk2, subw, proj, proj, proj)


def _tree_sum_rows(x):
    n = x.shape[0] // 8
    x = x.reshape(n, 8, x.shape[1])
    while n > 1:
        n //= 2
        x = x[:n] + x[n:]
    return x[0]


def _tree_max_rows(x):
    n = x.shape[0] // 8
    x = x.reshape(n, 8, x.shape[1])
    while n > 1:
        n //= 2
        x = jnp.maximum(x[:n], x[n:])
    return x[0]


def _dsa_kernel(q_ref, ckv_ref, kvw_ref, qi_ref, ki_ref, wi_ref, qx_ref, kx_ref, wup_ref, o_ref,
                cn_ref, cnt_ref, skey_ref, bias_ref, s_ref, *, tq, tk, k_sel, hg):
    i = pl.program_id(1)
    t0 = i * tq

    @pl.when(i == 0)
    def _():
        c = ckv_ref[0]
        cn = c * lax.rsqrt(jnp.mean(c * c, axis=-1, keepdims=True) + RMS_EPS) * kvw_ref[...]
        cn_ref[:, :DSA_LATENT] = cn.astype(BF16)
        cn_ref[:, DSA_LATENT:] = kx_ref[...]
        cnt_ref[:DSA_LATENT, :] = cn.T.astype(BF16)
        row = lax.broadcasted_iota(I32, (BF16_ROWS, cnt_ref.shape[1]), 0)
        cnt_ref[DSA_LATENT:, :] = jnp.where(row == 0, 1.0, 0.0).astype(BF16)

    for k in range(ckv_ref.shape[1] // tk):
        @pl.when((t0 + tq - 1) // tk == k)
        def _(k=k):
            _dsa_block(q_ref, qi_ref, ki_ref, wi_ref, qx_ref, wup_ref, o_ref, cn_ref, cnt_ref, skey_ref, bias_ref,
                       s_ref, t0=t0, n_tiles=k + 1, tq=tq, tk=tk, k_sel=k_sel, hg=hg)


def _static_loop(n, body, carry):
    for j in range(n):
        carry = body(j, carry)
    return carry


def _dsa_block(q_ref, qi_ref, ki_ref, wi_ref, qx_ref, wup_ref, o_ref, cn_ref, cnt_ref, skey_ref, bias_ref, s_ref,
               *, t0, n_tiles, tq, tk, k_sel, hg):
    def tile_slice(j):
        return pl.ds(j * tk, tk)

    lane = lax.broadcasted_iota(I32, (1, LANES), 1)
    q_pos = t0 + lax.broadcasted_iota(I32, (tk, tq), 1)
    key_off = lax.broadcasted_iota(I32, (tk, tq), 0)

    w_t = (wi_ref[0] * ((IDX_HEADS ** -0.5) * (IDX_DIM ** -0.5))).T
    stacked = []
    for h in range(IDX_HEADS):
        qp = qi_ref[0, :, (h // 2) * LANES:(h // 2 + 1) * LANES]
        stacked.append(jnp.where((lane // IDX_DIM) == (h % 2), qp, jnp.zeros_like(qp)))
    q8 = jnp.concatenate(stacked, axis=0)

    def score_tile(j, _):
        d = _nt_dot(ki_ref[0, tile_slice(j), :], q8)
        score = jnp.zeros((tk, tq), F32)
        for h in range(IDX_HEADS):
            score = score + jnp.maximum(d[:, h * tq:(h + 1) * tq], 0.0) * w_t[h:h + 1, :]
        score = jnp.where(score == 0.0, 0.0, score)
        bits = pltpu.bitcast(score, I32)
        skey = jnp.where(bits < 0, bits ^ 0x7FFFFFFF, bits)
        skey_ref[tile_slice(j), :] = jnp.where(j * tk + key_off <= q_pos, skey, INT_MIN)
        return 0

    _static_loop(n_tiles, score_tile, 0)

    def count(pred):
        def tile(j, cnt):
            return cnt + _tree_sum_rows(pred(skey_ref[tile_slice(j), :]).astype(F32))
        cnt = _static_loop(n_tiles, tile, jnp.zeros((8, tq), F32))
        return jnp.sum(cnt, axis=0, keepdims=True)

    kf = float(k_sel)
    thr = jnp.where(count(lambda sk: sk >= 0) >= kf, 0, INT_MIN).astype(I32)

    def bit_step(b, thr):
        cand = thr + jnp.left_shift(jnp.int32(1), 30 - b)
        return jnp.where(count(lambda sk: sk >= cand) >= kf, cand, thr)

    thr = lax.fori_loop(0, 31, bit_step, thr)
    need = kf - count(lambda sk: sk > thr)

    r2 = lax.broadcasted_iota(I32, (tk, tk), 0)
    c2 = lax.broadcasted_iota(I32, (tk, tk), 1)
    upto = (c2 <= r2).astype(BF16)

    def tie_tile(j, seen):
        sk = skey_ref[tile_slice(j), :]
        tied = sk == thr
        prefix = _dot(upto, tied.astype(F32).astype(BF16)) + seen
        selected = ((sk > thr) | (tied & (prefix <= need))) & (j * tk + key_off <= q_pos)
        bias_ref[tile_slice(j), :] = jnp.where(selected, 0.0, MASKED)
        return prefix[tk - 1:tk, :]

    _static_loop(n_tiles, tie_tile, jnp.zeros((1, tq), F32))

    log2_scale = (DSA_LATENT ** -0.5) * LOG2_E

    def head_group(g, _):
        heads = [g * hg + u for u in range(hg)]
        offs = [pl.multiple_of(h * DSA_LATENT, DSA_LATENT) for h in heads]
        qg = jnp.concatenate(
            [jnp.concatenate([q_ref[0, :, pl.ds(off, DSA_LATENT)],
                              jnp.broadcast_to(qx_ref[pl.ds(h, 1), :], (tq, LANES)).astype(BF16)], axis=1)
             for h, off in zip(heads, offs)], axis=0)

        def logits_tile(j, m):
            d = _nt_dot(cn_ref[tile_slice(j), :], qg)
            b = bias_ref[tile_slice(j), :]
            tops = []
            for u in range(hg):
                s = d[:, u * tq:(u + 1) * tq] * log2_scale + b
                s_ref[tile_slice(j), u * tq:(u + 1) * tq] = s
                tops.append(jnp.max(_tree_max_rows(s), axis=0, keepdims=True))
            return jnp.maximum(m, jnp.concatenate(tops, axis=1))

        m = _static_loop(n_tiles, logits_tile, jnp.full((1, hg * tq), MASKED, F32))

        def value_tile(j, acc):
            p = jnp.exp2((s_ref[tile_slice(j), :] - m).astype(BF16))
            return acc + _dot(cnt_ref[:, tile_slice(j)], p)

        acc = _static_loop(n_tiles, value_tile, jnp.zeros((DSA_LATENT + BF16_ROWS, hg * tq), F32))
        out = (acc[:DSA_LATENT] / acc[DSA_LATENT:DSA_LATENT + 1]).astype(BF16)
        for u in range(0, hg, 2):
            pair = jnp.concatenate([out[:, u * tq:(u + 1) * tq], out[:, (u + 1) * tq:(u + 2) * tq]], axis=0)
            up = _dot(wup_ref[(g * hg + u) // 2], pair)
            off = pl.multiple_of((g * hg + u) * DSA_VDIM, 2 * DSA_VDIM)
            o_ref[0, :, pl.ds(off, 2 * DSA_VDIM)] = up.T.astype(o_ref.dtype)
        return 0

    lax.fori_loop(0, DSA_HEADS // hg, head_group, 0)


def _bf16_parts(x, n):
    parts = []
    for _ in range(n):
        p = x.astype(BF16).astype(F32)
        parts.append(p)
        x = x - p
    return parts


def _dsa_attention(proj, small, kvw, w_uv, bsz, seq):
    tq = min(TQ_DSA, seq)
    tk = min(TRI, seq)
    k_sel = min(TOPK_MAX, seq // 4)
    qw = DSA_HEADS * DSA_LATENT
    qiw = IDX_HEADS * IDX_DIM
    slope = _alibi_slopes(DSA_HEADS) / (DSA_LATENT ** -0.5)
    s_parts = _bf16_parts(slope, 3)
    pos = jnp.arange(seq, dtype=I32)
    p_parts = [(pos // LANES * LANES).astype(F32), (pos % LANES).astype(F32)]
    qx = jnp.stack([sp for _ in p_parts for sp in s_parts], axis=1)
    kx = jnp.stack([pp for pp in p_parts for _ in s_parts], axis=1)
    qx = jnp.pad(qx, ((0, 0), (0, LANES - qx.shape[1])))
    kx = jnp.pad(kx, ((0, 0), (0, LANES - kx.shape[1]))).astype(BF16)
    wt = jnp.swapaxes(w_uv, 1, 2).reshape(DSA_HEADS // 2, 2, DSA_VDIM, DSA_LATENT)
    zero = jnp.zeros_like(wt[:, 0])
    wup = jnp.concatenate([jnp.concatenate([wt[:, 0], zero], axis=2),
                           jnp.concatenate([zero, wt[:, 1]], axis=2)], axis=1).astype(BF16)
    return pl.pallas_call(
        functools.partial(_dsa_kernel, tq=tq, tk=tk, k_sel=k_sel, hg=HG_DSA),
        grid=(bsz, seq // tq),
        in_specs=[pl.BlockSpec((1, tq, qw), lambda b, i: (b, i, 0)),
                  pl.BlockSpec((1, seq, DSA_LATENT), lambda b, i: (b, 0, 0)),
                  pl.BlockSpec((1, DSA_LATENT), lambda b, i: (0, 0)),
                  pl.BlockSpec((1, tq, qiw), lambda b, i: (b, i, qw // qiw)),
                  pl.BlockSpec((1, seq, LANES), lambda b, i: (b, 0, (qw + qiw) // LANES)),
                  pl.BlockSpec((1, tq, LANES), lambda b, i: (b, i, 1)),
                  pl.BlockSpec((DSA_HEADS, LANES), lambda b, i: (0, 0)),
                  pl.BlockSpec((seq, LANES), lambda b, i: (0, 0)),
                  pl.BlockSpec(wup.shape, lambda b, i: (0, 0, 0))],
        out_specs=pl.BlockSpec((1, tq, DSA_HEADS * DSA_VDIM), lambda b, i: (b, i, 0)),
        out_shape=jax.ShapeDtypeStruct((bsz, seq, DSA_HEADS * DSA_VDIM), BF16),
        scratch_shapes=[pltpu.VMEM((seq, DSA_LATENT + LANES), BF16),
                        pltpu.VMEM((DSA_LATENT + BF16_ROWS, seq), BF16),
                        pltpu.VMEM((seq, tq), I32), pltpu.VMEM((seq, tq), F32),
                        pltpu.VMEM((seq, HG_DSA * tq), F32)],
        compiler_params=_cparams("parallel", "arbitrary"),
        name="dsa_attn",
    )(proj, small, kvw, proj, proj, small, qx, kx, wup)


def _mix_tail(mix, h_ref, g_ref, b_ref, rw_ref, rb_ref, hout_ref, logit_ref):
    hn = _layer_norm(DEEPNORM_ALPHA * h_ref[...] + mix, g_ref[...], b_ref[...])
    hout_ref[...] = hn
    logit_ref[...] = _dot(hn.astype(BF16), rw_ref[...]) + rb_ref[...]


def _mix_even_kernel(oa_ref, ob_ref, wa_ref, wb_ref, h_ref, g_ref, b_ref, rw_ref, rb_ref, hout_ref, logit_ref):
    mix = _dot(oa_ref[...], wa_ref[...]) + _dot(ob_ref[...], wb_ref[...])
    _mix_tail(mix, h_ref, g_ref, b_ref, rw_ref, rb_ref, hout_ref, logit_ref)


def _mix_odd_kernel(o_ref, wo_ref, h_ref, g_ref, b_ref, rw_ref, rb_ref, hout_ref, logit_ref):
    _mix_tail(_dot(o_ref[...], wo_ref[...]), h_ref, g_ref, b_ref, rw_ref, rb_ref, hout_ref, logit_ref)


def _mix_call(kernel, name, acts, weights, h, g, b, rw, rb):
    t, d = h.shape
    tm = min(TM_PROJ, t)
    row = lambda n: pl.BlockSpec((tm, n), lambda i: (i, 0))
    full = lambda a: pl.BlockSpec(a.shape, lambda i: (0, 0))
    return pl.pallas_call(
        kernel,
        grid=(t // tm,),
        in_specs=[row(a.shape[1]) for a in acts] + [full(w) for w in weights]
                 + [row(d), full(g), full(b), full(rw), full(rb)],
        out_specs=[row(d), row(LANES)],
        out_shape=[jax.ShapeDtypeStruct((t, d), F32), jax.ShapeDtypeStruct((t, LANES), F32)],
        compiler_params=_cparams("parallel"),
        name=name,
    )(*acts, *weights, h, g, b, rw, rb)


def _route_kernel(logit_ref, idx_ref, gate_ref, rank_ref, cnt_ref, *, tm):
    lane = lax.broadcasted_iota(I32, (tm, LANES), 1)
    lane_f = lane.astype(F32)
    x = jnp.where(lane < N_EXPERTS, logit_ref[...], -jnp.inf)
    vals, hots = [], []
    for k in range(TOP_K):
        m = jnp.max(x, axis=-1, keepdims=True)
        first = jnp.min(jnp.where(x == m, lane_f, float(LANES)), axis=-1, keepdims=True)
        hot = lane_f == first
        x = jnp.where(hot, -jnp.inf, x)
        vals.append(m)
        hots.append(hot)
        idx_ref[:, k:k + 1] = first.astype(I32)
    exps = [jnp.exp(v - vals[0]) for v in vals]
    denom = exps[0] + exps[1] + exps[2] + exps[3]
    for k in range(TOP_K):
        gate_ref[:, k:k + 1] = exps[k] / denom

    hot_sum = (hots[0] | hots[1] | hots[2] | hots[3]).astype(F32)
    r = lax.broadcasted_iota(I32, (tm, tm), 0)
    c = lax.broadcasted_iota(I32, (tm, tm), 1)
    earlier = (c < r).astype(BF16)
    before = _dot(earlier, hot_sum.astype(BF16))
    for k in range(TOP_K):
        rank_ref[:, k:k + 1] = jnp.sum(jnp.where(hots[k], before, 0.0), axis=-1, keepdims=True).astype(I32)
    cnt_ref[0] = jnp.sum(hot_sum, axis=0, keepdims=True)


def _route(logits):
    t = logits.shape[0]
    tm = min(TD_DISPATCH, t)
    narrow = pl.BlockSpec((tm, TOP_K), lambda i: (i, 0))
    return pl.pallas_call(
        functools.partial(_route_kernel, tm=tm),
        grid=(t // tm,),
        in_specs=[pl.BlockSpec((tm, LANES), lambda i: (i, 0))],
        out_specs=[narrow, narrow, narrow, pl.BlockSpec((1, 1, LANES), lambda i: (i, 0, 0))],
        out_shape=[jax.ShapeDtypeStruct((t, TOP_K), I32), jax.ShapeDtypeStruct((t, TOP_K), F32),
                   jax.ShapeDtypeStruct((t, TOP_K), I32), jax.ShapeDtypeStruct((t // tm, 1, LANES), F32)],
        compiler_params=_cparams("parallel"),
        name="route",
    )(logits)


def _chunked_copy(src_ref, src_off, dst_ref, dst_off, count, largest, sem, start_not_wait):
    chunk = largest
    while chunk >= SUBLANES:
        present = (count & chunk) != 0
        copy = pltpu.make_async_copy(src_ref.at[pl.ds(pl.multiple_of(src_off, SUBLANES), chunk), :],
                                     dst_ref.at[pl.ds(pl.multiple_of(dst_off, SUBLANES), chunk), :], sem)

        @pl.when(jnp.logical_and(present, start_not_wait))
        def _():
            copy.start()

        @pl.when(jnp.logical_and(present, jnp.logical_not(start_not_wait)))
        def _():
            copy.wait()

        step = jnp.where(present, chunk, 0)
        src_off = src_off + step
        dst_off = dst_off + step
        chunk //= 2


def _dispatch_kernel(seg_rows_ref, seg_src_ref, seg_dst_ref, pad_start_ref, pad_count_ref, n_used_ref,
                     lp_ref, x_ref, xs_out, staged_ref, zero_ref, sem, pad_sem, *, td):
    i = pl.program_id(0)
    tile = zero_ref.shape[0]

    @pl.when(i == 0)
    def _():
        zero_ref[...] = jnp.zeros_like(zero_ref)

        def pads(e, start_not_wait):
            _chunked_copy(zero_ref, 0, xs_out, pad_start_ref[e], pad_count_ref[e], tile // 2, pad_sem,
                          start_not_wait)
            return start_not_wait

        def tail(j, start_not_wait):
            present = j >= n_used_ref[0]
            copy = pltpu.make_async_copy(zero_ref, xs_out.at[pl.ds(pl.multiple_of(j * tile, tile), tile), :], pad_sem)

            @pl.when(jnp.logical_and(present, start_not_wait))
            def _():
                copy.start()

            @pl.when(jnp.logical_and(present, jnp.logical_not(start_not_wait)))
            def _():
                copy.wait()
            return start_not_wait

        n_tiles = xs_out.shape[0] // tile
        lax.fori_loop(0, N_EXPERTS, pads, True)
        lax.fori_loop(0, n_tiles, tail, True)
        lax.fori_loop(0, N_EXPERTS, pads, False)
        lax.fori_loop(0, n_tiles, tail, False)

    rows = staged_ref.shape[0]
    r_iota = lax.broadcasted_iota(I32, (rows, td), 0)
    hit = r_iota == lp_ref[0, 0:1, :]
    for k in range(1, TOP_K):
        hit = hit | (r_iota == lp_ref[0, k:k + 1, :])
    staged_ref[...] = _dot(jnp.where(hit, 1.0, 0.0).astype(BF16), x_ref[...].astype(BF16))

    def runs(e, start_not_wait):
        s = i * N_EXPERTS + e
        _chunked_copy(staged_ref, seg_src_ref[s], xs_out, seg_dst_ref[s], seg_rows_ref[s], td, sem, start_not_wait)
        return start_not_wait

    lax.fori_loop(0, N_EXPERTS, runs, True)
    lax.fori_loop(0, N_EXPERTS, runs, False)


def _dispatch(seg_rows, seg_src, seg_dst, pad_start, pad_count, n_used, lp, x, n_rows):
    t, d = x.shape
    td = min(TD_DISPATCH, t)
    staged_rows = td * TOP_K + N_EXPERTS * SUBLANES
    grid_spec = pltpu.PrefetchScalarGridSpec(
        num_scalar_prefetch=6,
        grid=(t // td,),
        in_specs=[pl.BlockSpec((1, TOP_K, td), lambda i, *_: (i, 0, 0)),
                  pl.BlockSpec((td, d), lambda i, *_: (i, 0))],
        out_specs=pl.BlockSpec(memory_space=pl.ANY),
        scratch_shapes=[pltpu.VMEM((staged_rows, d), F32), pltpu.VMEM((TM_FFN, d), F32),
                        pltpu.SemaphoreType.DMA, pltpu.SemaphoreType.DMA],
    )
    return pl.pallas_call(
        functools.partial(_dispatch_kernel, td=td),
        grid_spec=grid_spec,
        out_shape=jax.ShapeDtypeStruct((n_rows, d), F32),
        compiler_params=_cparams("arbitrary"),
        name="moe_dispatch",
    )(seg_rows, seg_src, seg_dst, pad_start, pad_count, n_used, lp, x)


def _ffn_kernel(tile_expert_ref, n_used_ref, xs_ref, wgu_ref, bgu_ref, wd_ref, bd_ref, y_ref, wgu_bf, wd_bf):
    i = pl.program_id(0)
    d_ff = wd_ref.shape[2]
    new_expert = jnp.logical_or(i == 0, tile_expert_ref[i] != tile_expert_ref[jnp.maximum(i - 1, 0)])

    @pl.when(jnp.logical_and(new_expert, i < n_used_ref[0]))
    def _():
        wgu_bf[...] = wgu_ref[0, 0].astype(BF16)
        wd_bf[...] = wd_ref[0, 0].astype(BF16)

    @pl.when(i < n_used_ref[0])
    def _():
        x = xs_ref[...].astype(BF16)
        hgu = _dot(x, wgu_bf[...]) + bgu_ref[0, 0]
        gate = jnp.minimum(hgu[:, :d_ff], SWIGLU_LIMIT)
        up = jnp.clip(hgu[:, d_ff:], -SWIGLU_LIMIT, SWIGLU_LIMIT)
        act = gate * (1.0 / (1.0 + jnp.exp(-SWIGLU_ALPHA * gate))) * (up + 1.0)
        y_ref[...] = _dot(act.astype(BF16), wd_bf[...]) + bd_ref[0, 0]

    @pl.when(i >= n_used_ref[0])
    def _():
        y_ref[...] = jnp.zeros_like(y_ref)


def _expert_ffn(tile_expert, n_used, xs, w_gu, b_gu, w_down, b_down, layer):
    n_rows, d = xs.shape
    depth, e, _, f2 = w_gu.shape
    d_ff = w_down.shape[2]
    tm = TM_FFN
    grid_spec = pltpu.PrefetchScalarGridSpec(
        num_scalar_prefetch=2,
        grid=(n_rows // tm,),
        in_specs=[pl.BlockSpec((tm, d), lambda i, te, nu: (jnp.minimum(i, nu[0] - 1), 0)),
                  pl.BlockSpec((1, 1, d, f2), lambda i, te, nu: (layer, te[i], 0, 0)),
                  pl.BlockSpec((1, 1, 1, f2), lambda i, te, nu: (layer, te[i], 0, 0)),
                  pl.BlockSpec((1, 1, d_ff, d), lambda i, te, nu: (layer, te[i], 0, 0)),
                  pl.BlockSpec((1, 1, 1, d), lambda i, te, nu: (layer, te[i], 0, 0))],
        out_specs=pl.BlockSpec((tm, d), lambda i, te, nu: (i, 0)),
        scratch_shapes=[pltpu.VMEM((d, f2), BF16), pltpu.VMEM((d_ff, d), BF16)],
    )
    return pl.pallas_call(
        _ffn_kernel,
        grid_spec=grid_spec,
        out_shape=jax.ShapeDtypeStruct((n_rows, d), F32),
        compiler_params=_cparams("arbitrary"),
        name="expert_ffn",
    )(tile_expert, n_used, xs, w_gu, b_gu.reshape(depth, e, 1, f2), w_down, b_down.reshape(depth, e, 1, d))


def _combine_kernel(seg_rows_ref, seg_src_ref, seg_dst_ref, lp_ref, gate_ref, h_ref, g_ref, b_ref, y_hbm, out_ref,
                    staged_ref, ffn_ref, sem, *, tc):
    i = pl.program_id(0)

    def runs(e, start_not_wait):
        s = i * N_EXPERTS + e
        _chunked_copy(y_hbm, seg_dst_ref[s], staged_ref, seg_src_ref[s], seg_rows_ref[s], tc, sem, start_not_wait)
        return start_not_wait

    lax.fori_loop(0, N_EXPERTS, runs, True)
    lax.fori_loop(0, N_EXPERTS, runs, False)

    def token(t, _):
        acc = gate_ref[t * TOP_K] * staged_ref[pl.ds(lp_ref[t * TOP_K], 1), :]
        for k in range(1, TOP_K):
            acc = acc + gate_ref[t * TOP_K + k] * staged_ref[pl.ds(lp_ref[t * TOP_K + k], 1), :]
        ffn_ref[pl.ds(t, 1), :] = acc
        return 0

    lax.fori_loop(0, tc, token, 0)
    out_ref[...] = _layer_norm(DEEPNORM_ALPHA * h_ref[...] + ffn_ref[...], g_ref[...], b_ref[...])


def _combine(seg_rows, seg_src, seg_dst, lp_flat, gates_flat, h, g, b, y):
    t, d = h.shape
    tc = min(TD_DISPATCH, t)
    staged_rows = tc * TOP_K + N_EXPERTS * SUBLANES
    grid_spec = pltpu.PrefetchScalarGridSpec(
        num_scalar_prefetch=3,
        grid=(t // tc,),
        in_specs=[pl.BlockSpec((tc * TOP_K,), lambda i, *_: (i,), memory_space=pltpu.SMEM),
                  pl.BlockSpec((tc * TOP_K,), lambda i, *_: (i,), memory_space=pltpu.SMEM),
                  pl.BlockSpec((tc, d), lambda i, *_: (i, 0)),
                  pl.BlockSpec((1, d), lambda i, *_: (0, 0)),
                  pl.BlockSpec((1, d), lambda i, *_: (0, 0)),
                  pl.BlockSpec(memory_space=pl.ANY)],
        out_specs=pl.BlockSpec((tc, d), lambda i, *_: (i, 0)),
        scratch_shapes=[pltpu.VMEM((staged_rows, d), F32), pltpu.VMEM((tc, d), F32), pltpu.SemaphoreType.DMA],
    )
    return pl.pallas_call(
        functools.partial(_combine_kernel, tc=tc),
        grid_spec=grid_spec,
        out_shape=jax.ShapeDtypeStruct((t, d), F32),
        compiler_params=_cparams("arbitrary"),
        name="moe_combine",
    )(seg_rows, seg_src, seg_dst, lp_flat, gates_flat, h, g, b, y)


def _moe(h, logits, w_gu, b_gu, w_down, b_down, g, b, layer):
    t = h.shape[0]
    td = min(TD_DISPATCH, t)
    n_tok_tiles = t // td
    idx, gates, rank, counts = _route(logits)
    counts = counts[:, 0, :N_EXPERTS].astype(I32)
    seg = (counts + SUBLANES - 1) // SUBLANES * SUBLANES
    rows = jnp.sum(seg, axis=0)
    tiles = (rows + TM_FFN - 1) // TM_FFN
    tile_end = jnp.cumsum(tiles)
    start = (tile_end - tiles) * TM_FFN
    seg_dst = start[None, :] + jnp.cumsum(seg, axis=0) - seg
    seg_src = jnp.cumsum(seg, axis=1) - seg
    n_tiles = -(-(t * TOP_K + n_tok_tiles * N_EXPERTS * (SUBLANES - 1)) // TM_FFN) + N_EXPERTS
    tile_ids = jnp.arange(n_tiles, dtype=I32)
    tile_expert = jnp.minimum(jnp.sum((tile_end[None, :] <= tile_ids[:, None]).astype(I32), axis=1), N_EXPERTS - 1)
    chosen = idx.reshape(n_tok_tiles, td, TOP_K, 1) == jnp.arange(N_EXPERTS, dtype=I32)
    pick = lambda table: jnp.sum(jnp.where(chosen, table[:, None, None, :], 0), axis=-1)
    rank = rank.reshape(n_tok_tiles, td, TOP_K)
    lp = (pick(seg_src) + rank).astype(I32)
    n_used = tile_end[-1:].astype(I32)
    flat = lambda a: a.reshape(-1).astype(I32)
    xs = _dispatch(flat(seg), flat(seg_src), flat(seg_dst), start + rows, tiles * TM_FFN - rows, n_used,
                   jnp.swapaxes(lp, 1, 2), h, n_tiles * TM_FFN)
    y = _expert_ffn(tile_expert, n_used, xs, w_gu, b_gu, w_down, b_down, layer)
    return _combine(flat(seg), flat(seg_src), flat(seg_dst), flat(lp), gates.reshape(-1), h, g, b, y)


def _router_params(router_w, router_b):
    rw = jnp.pad(router_w, ((0, 0), (0, LANES - N_EXPERTS))).astype(BF16)
    rb = jnp.pad(router_b, (0, LANES - N_EXPERTS)).reshape(1, LANES)
    return rw, rb


def _alibi_slopes(n):
    return 2.0 ** (-8.0 * jnp.arange(1, n + 1, dtype=F32) / n)


def _even_layer(h, bsz, seq, w_in, w_out, lq1, lk1, lq2, lk2, subw, layer, g, b, rw, rb):
    proj = _project(h, [w_in.astype(BF16)], [BF16])[0].reshape(bsz, seq, -1)
    o_a = _sb_attention(proj, bsz, seq)
    lam_init = 0.8 - 0.6 * math.exp(-0.3 * layer)
    row = lambda v: v.reshape(1, -1)
    o_b = _diff_attention(proj, _alibi_slopes(DIFF_HEADS), row(lq1), row(lk1), row(lq2), row(lk2), row(subw),
                          bsz, seq, lam_init)
    t = bsz * seq
    sbw = SB_HEADS * HEAD_DIM
    wo = w_out.astype(BF16)
    return _mix_call(_mix_even_kernel, "mix_even", [o_a.reshape(t, -1), o_b.reshape(t, -1)],
                     [wo[:sbw], wo[sbw:]], h, row(g), row(b), rw, rb)


def _odd_layer(h, bsz, seq, w_in, kvw, w_uv, w_out, g, b, rw, rb):
    qw = DSA_HEADS * DSA_LATENT
    c0, c1 = qw, qw + DSA_LATENT
    i0, i1 = c1, c1 + IDX_HEADS * IDX_DIM
    k1 = i1 + IDX_DIM
    w_main = jnp.concatenate([w_in[:, :qw], w_in[:, i0:i1], w_in[:, i1:k1], w_in[:, i1:k1]], axis=1).astype(BF16)
    w_small = jnp.pad(jnp.concatenate([w_in[:, c0:c1], w_in[:, k1:]], axis=1),
                      ((0, 0), (0, LANES - IDX_HEADS))).astype(BF16)
    proj, small = [p.reshape(bsz, seq, -1) for p in _project(h, [w_main, w_small], [BF16, F32])]
    o = _dsa_attention(proj, small, kvw.reshape(1, -1), w_uv, bsz, seq)
    row = lambda v: v.reshape(1, -1)
    return _mix_call(_mix_odd_kernel, "mix_odd", [o.reshape(bsz * seq, -1)], [w_out.astype(BF16)],
                     h, row(g), row(b), rw, rb)


def kernel(x, ev_w_in, ev_w_out, ev_lambda_q1, ev_lambda_k1, ev_lambda_q2, ev_lambda_k2, ev_subln_w, od_w_in, od_kv_norm_w, od_w_uv, od_w_out, ln_mix_g, ln_mix_b, router_w, router_b, exp_w_gu, exp_b_gu, exp_w_down, exp_b_down, ln_ffn_g, ln_ffn_b):
    bsz, seq, d = x.shape
    h = x.reshape(bsz * seq, d)
    for layer in range(ln_mix_g.shape[0]):
        j = layer // 2
        rw, rb = _router_params(router_w[layer], router_b[layer])
        if layer % 2 == 0:
            h, logits = _even_layer(h, bsz, seq, ev_w_in[j], ev_w_out[j], ev_lambda_q1[j], ev_lambda_k1[j],
                                    ev_lambda_q2[j], ev_lambda_k2[j], ev_subln_w[j], layer,
                                    ln_mix_g[layer], ln_mix_b[layer], rw, rb)
        else:
            h, logits = _odd_layer(h, bsz, seq, od_w_in[j], od_kv_norm_w[j], od_w_uv[j], od_w_out[j],
                                   ln_mix_g[layer], ln_mix_b[layer], rw, rb)
        h = _moe(h, logits, exp_w_gu, exp_b_gu, exp_w_down, exp_b_down,
                 ln_ffn_g[layer].reshape(1, -1), ln_ffn_b[layer].reshape(1, -1), layer)
    return h.reshape(bsz, seq, d)
```

```python
import functools
import math

import jax
import jax.numpy as jnp
from jax import lax
from jax.experimental import pallas as pl
from jax.experimental.pallas import tpu as pltpu

F32, BF16, I32 = jnp.float32, jnp.bfloat16, jnp.int32

HEAD_DIM = 64
SB_HEADS = 8
DIFF_HEADS = 4
DIFF_SUB = 64
DIFF_VDIM = 128
DSA_HEADS = 16
DSA_LATENT = 128
DSA_VDIM = 64
IDX_HEADS = 8
IDX_DIM = 64
TOPK_MAX = 256
N_EXPERTS = 32
TOP_K = 4
SWIGLU_LIMIT = 7.0
SWIGLU_ALPHA = 1.702
LN_EPS = 1e-5
RMS_EPS = 1e-5
DEPTH = 2
DEEPNORM_ALPHA = (2 * DEPTH) ** 0.25

LANES = 128
SUBLANES = 8
BF16_ROWS = 16
VMEM_LIMIT = 56 * 1024 * 1024
INT_MIN = -(2 ** 31)
MASKED = -1e30
DEAD_LOG2 = -152.0
LOG2_E = math.log2(math.e)

TM_PROJ = 512
TQ_ATT = 256
TQ_DSA = 128
TRI = 512
HG_DSA = 16
TM_FFN = 512
TD_DISPATCH = 512


def _cparams(*sem):
    return pltpu.CompilerParams(dimension_semantics=sem, vmem_limit_bytes=VMEM_LIMIT)


def _nt_dot(a, b):
    return lax.dot_general(a, b, (((1,), (1,)), ((), ())), preferred_element_type=F32)


def _dot(a, b):
    return jnp.dot(a, b, preferred_element_type=F32)


def _layer_norm(y, g, b):
    mu = jnp.mean(y, axis=-1, keepdims=True)
    d = y - mu
    var = jnp.mean(d * d, axis=-1, keepdims=True)
    return d * lax.rsqrt(var + LN_EPS) * g + b


def _col_chunk(n):
    for c in (512, 384, 256, 128):
        if n % c == 0:
            return c
    raise ValueError(n)


def _proj_kernel(x_ref, *refs):
    n_out = len(refs) // 2
    x = x_ref[...].astype(BF16)
    for w_ref, o_ref in zip(refs[:n_out], refs[n_out:]):
        n = o_ref.shape[-1]
        c = _col_chunk(n)
        for j in range(0, n, c):
            o_ref[:, j:j + c] = _dot(x, w_ref[:, j:j + c]).astype(o_ref.dtype)


def _project(x, weights, out_dtypes):
    t, k = x.shape
    tm = min(TM_PROJ, t)
    return pl.pallas_call(
        _proj_kernel,
        grid=(t // tm,),
        in_specs=[pl.BlockSpec((tm, k), lambda i: (i, 0))]
                 + [pl.BlockSpec(w.shape, lambda i: (0, 0)) for w in weights],
        out_specs=[pl.BlockSpec((tm, w.shape[1]), lambda i: (i, 0)) for w in weights],
        out_shape=[jax.ShapeDtypeStruct((t, w.shape[1]), dt) for w, dt in zip(weights, out_dtypes)],
        compiler_params=_cparams("parallel"),
        name="in_proj",
    )(x, *weights)


def _stack_halves(x, lane):
    zero = jnp.zeros_like(x)
    return jnp.concatenate([jnp.where(lane < HEAD_DIM, x, zero), jnp.where(lane >= HEAD_DIM, x, zero)], axis=0)


def _sb_kernel(q_ref, k_ref, v_ref, o_ref, *, tq):
    i = pl.program_id(2)
    scale = HEAD_DIM ** -0.5
    lane = lax.broadcasted_iota(I32, (1, LANES), 1)
    row = lax.broadcasted_iota(I32, (2 * tq, tq), 0)
    row = jnp.where(row >= tq, row - tq, row)
    col = lax.broadcasted_iota(I32, (2 * tq, tq), 1)
    strict = col < row
    r1 = lax.broadcasted_iota(I32, (tq, tq), 0)
    c1 = lax.broadcasted_iota(I32, (tq, tq), 1)
    after = (r1 > c1).astype(BF16)
    q2 = _stack_halves(q_ref[0], lane)

    def block(j, carry, masked):
        run, acc = carry
        start = pl.multiple_of(j * tq, tq)
        kb = k_ref[0, pl.ds(start, tq), :]
        vb = v_ref[0, pl.ds(start, tq), :]
        z2 = _nt_dot(q2, kb) * (scale * LOG2_E)
        neg = -z2
        keep2 = jnp.minimum(neg, 0.0) - jnp.log2(1.0 + jnp.exp2(jnp.minimum(z2, neg)))
        lk = jnp.where(strict, keep2, 0.0) if masked else keep2
        hi = lk.astype(BF16)
        lo = (lk - hi.astype(F32)).astype(BF16)
        suffix = _dot(hi, after) + _dot(lo, after)
        w = jnp.exp2(z2 + keep2 + suffix + run)
        if masked:
            w = jnp.where(strict, w, 0.0)
        wb = w.astype(BF16)
        zero = jnp.zeros_like(vb)
        acc = (acc + _dot(wb[:tq], jnp.where(lane < HEAD_DIM, vb, zero))
               + _dot(wb[tq:], jnp.where(lane >= HEAD_DIM, vb, zero)))
        run = run + suffix[:, 0:1] + lk[:, 0:1]
        return run, acc

    def alive(c):
        return jnp.logical_and(c[0] >= 0, c[1] > DEAD_LOG2)

    def step(c):
        j, _, run, acc = c
        run, acc = block(j, (run, acc), False)
        return j - 1, jnp.max(run), run, acc

    start = (jnp.zeros((2 * tq, 1), F32), jnp.zeros((tq, LANES), F32))

    @pl.when(i == 0)
    def _():
        o_ref[0] = block(0, start, True)[1].astype(o_ref.dtype)

    @pl.when(i > 0)
    def _():
        run, acc = block(i - 1, block(i, start, True), False)
        _, _, _, acc = lax.while_loop(alive, step, (i - 2, jnp.max(run), run, acc))
        o_ref[0] = acc.astype(o_ref.dtype)


def _sb_attention(proj, bsz, seq):
    tq = min(TQ_ATT, seq)
    pairs = SB_HEADS * HEAD_DIM // LANES
    return pl.pallas_call(
        functools.partial(_sb_kernel, tq=tq),
        grid=(bsz, pairs, seq // tq),
        in_specs=[pl.BlockSpec((1, tq, LANES), lambda b, p, i: (b, i, p)),
                  pl.BlockSpec((1, seq, LANES), lambda b, p, i: (b, 0, pairs + p)),
                  pl.BlockSpec((1, seq, LANES), lambda b, p, i: (b, 0, 2 * pairs + p))],
        out_specs=pl.BlockSpec((1, tq, LANES), lambda b, p, i: (b, i, p)),
        out_shape=jax.ShapeDtypeStruct((bsz, seq, SB_HEADS * HEAD_DIM), BF16),
        compiler_params=_cparams("parallel", "parallel", "parallel"),
        name="stickbreak_attn",
    )(proj, proj, proj)


def _diff_kernel(slope_ref, lq1_ref, lk1_ref, lq2_ref, lk2_ref, subw_ref, q_ref, k_ref, v_ref, o_ref,
                 *, tq, lam_init):
    h = pl.program_id(1)
    i = pl.program_id(2)
    scale = DIFF_SUB ** -0.5
    slope = slope_ref[h]
    lam = (jnp.exp(jnp.sum(lq1_ref[...] * lk1_ref[...], keepdims=True))
           - jnp.exp(jnp.sum(lq2_ref[...] * lk2_ref[...], keepdims=True)) + lam_init)
    lane = lax.broadcasted_iota(I32, (1, LANES), 1)
    row = lax.broadcasted_iota(I32, (2 * tq, tq), 0)
    row = jnp.where(row >= tq, row - tq, row)
    col = lax.broadcasted_iota(I32, (2 * tq, tq), 1)
    causal = col <= row
    key_off = lax.broadcasted_iota(I32, (1, tq), 1)
    q2 = _stack_halves(q_ref[0], lane)

    def block(j, carry, masked):
        m, l, acc = carry
        start = j * tq
        kb = k_ref[0, pl.ds(start, tq), :]
        vb = v_ref[0, pl.ds(start, tq), :]
        s = _nt_dot(q2, kb) * (scale * LOG2_E) + (slope * LOG2_E) * (start + key_off).astype(F32)
        if masked:
            s = jnp.where(causal, s, -jnp.inf)
        m_new = jnp.maximum(m, jnp.max(s, axis=-1, keepdims=True))
        a = jnp.exp2(m - m_new)
        p = jnp.exp2(s - m_new)
        l = a * l + jnp.sum(p, axis=-1, keepdims=True)
        acc = a * acc + _dot(p.astype(BF16), vb)
        return m_new, l, acc

    for n_left in range(k_ref.shape[1] // tq):
        @pl.when(i == n_left)
        def _(n_left=n_left):
            carry = (jnp.full((2 * tq, 1), -jnp.inf, F32), jnp.zeros((2 * tq, 1), F32),
                     jnp.zeros((2 * tq, LANES), F32))
            carry = block(n_left, carry, True)
            for j in range(n_left):
                carry = block(j, carry, False)
            out = carry[2] / carry[1]
            o = out[:tq] - lam * out[tq:]
            o = o * lax.rsqrt(jnp.mean(o * o, axis=-1, keepdims=True) + RMS_EPS) * subw_ref[...]
            o_ref[0] = (o * (1.0 - lam_init)).astype(o_ref.dtype)


def _diff_attention(proj, slopes, lq1, lk1, lq2, lk2, subw, bsz, seq, lam_init):
    tq = min(TQ_ATT, seq)
    q0 = 3 * SB_HEADS * HEAD_DIM // LANES
    k0 = q0 + DIFF_HEADS
    v0 = k0 + DIFF_HEADS
    vec = lambda n: pl.BlockSpec((1, n), lambda b, h, i: (0, 0))
    return pl.pallas_call(
        functools.partial(_diff_kernel, tq=tq, lam_init=lam_init),
        grid=(bsz, DIFF_HEADS, seq // tq),
        in_specs=[pl.BlockSpec(memory_space=pltpu.SMEM),
                  vec(DIFF_SUB), vec(DIFF_SUB), vec(DIFF_SUB), vec(DIFF_SUB), vec(DIFF_VDIM),
                  pl.BlockSpec((1, tq, LANES), lambda b, h, i: (b, i, q0 + h)),
                  pl.BlockSpec((1, seq, LANES), lambda b, h, i: (b, 0, k0 + h)),
                  pl.BlockSpec((1, seq, LANES), lambda b, h, i: (b, 0, v0 + h))],
        out_specs=pl.BlockSpec((1, tq, LANES), lambda b, h, i: (b, i, h)),
        out_shape=jax.ShapeDtypeStruct((bsz, seq, DIFF_HEADS * DIFF_VDIM), BF16),
        compiler_params=_cparams("parallel", "parallel", "parallel"),
        name="diff_attn",
    )(slopes, lq1, lk1, lq2, lk2, subw, proj, proj, proj)


def _tree_sum_rows(x):
    n = x.shape[0] // 8
    x = x.reshape(n, 8, x.shape[1])
    while n > 1:
        n //= 2
        x = x[:n] + x[n:]
    return x[0]


def _tree_max_rows(x):
    n = x.shape[0] // 8
    x = x.reshape(n, 8, x.shape[1])
    while n > 1:
        n //= 2
        x = jnp.maximum(x[:n], x[n:])
    return x[0]


def _dsa_kernel(q_ref, ckv_ref, kvw_ref, qi_ref, ki_ref, wi_ref, qx_ref, kx_ref, wup_ref, o_ref,
                cn_ref, cnt_ref, skey_ref, bias_ref, s_ref, *, tq, tk, k_sel, hg):
    i = pl.program_id(1)
    t0 = i * tq

    @pl.when(i == 0)
    def _():
        c = ckv_ref[0]
        cn = c * lax.rsqrt(jnp.mean(c * c, axis=-1, keepdims=True) + RMS_EPS) * kvw_ref[...]
        cn_ref[:, :DSA_LATENT] = cn.astype(BF16)
        cn_ref[:, DSA_LATENT:] = kx_ref[...]
        cnt_ref[:DSA_LATENT, :] = cn.T.astype(BF16)
        row = lax.broadcasted_iota(I32, (BF16_ROWS, cnt_ref.shape[1]), 0)
        cnt_ref[DSA_LATENT:, :] = jnp.where(row == 0, 1.0, 0.0).astype(BF16)

    for k in range(ckv_ref.shape[1] // tk):
        @pl.when((t0 + tq - 1) // tk == k)
        def _(k=k):
            _dsa_block(q_ref, qi_ref, ki_ref, wi_ref, qx_ref, wup_ref, o_ref, cn_ref, cnt_ref, skey_ref, bias_ref,
                       s_ref, t0=t0, n_tiles=k + 1, tq=tq, tk=tk, k_sel=k_sel, hg=hg)


def _static_loop(n, body, carry):
    for j in range(n):
        carry = body(j, carry)
    return carry


def _dsa_block(q_ref, qi_ref, ki_ref, wi_ref, qx_ref, wup_ref, o_ref, cn_ref, cnt_ref, skey_ref, bias_ref, s_ref,
               *, t0, n_tiles, tq, tk, k_sel, hg):
    def tile_slice(j):
        return pl.ds(j * tk, tk)

    lane = lax.broadcasted_iota(I32, (1, LANES), 1)
    q_pos = t0 + lax.broadcasted_iota(I32, (tk, tq), 1)
    key_off = lax.broadcasted_iota(I32, (tk, tq), 0)

    w_t = (wi_ref[0] * ((IDX_HEADS ** -0.5) * (IDX_DIM ** -0.5))).T
    stacked = []
    for h in range(IDX_HEADS):
        qp = qi_ref[0, :, (h // 2) * LANES:(h // 2 + 1) * LANES]
        stacked.append(jnp.where((lane // IDX_DIM) == (h % 2), qp, jnp.zeros_like(qp)))
    q8 = jnp.concatenate(stacked, axis=0)

    def score_tile(j, _):
        d = _nt_dot(ki_ref[0, tile_slice(j), :], q8)
        score = jnp.zeros((tk, tq), F32)
        for h in range(IDX_HEADS):
            score = score + jnp.maximum(d[:, h * tq:(h + 1) * tq], 0.0) * w_t[h:h + 1, :]
        score = jnp.where(score == 0.0, 0.0, score)
        bits = pltpu.bitcast(score, I32)
        skey = jnp.where(bits < 0, bits ^ 0x7FFFFFFF, bits)
        skey_ref[tile_slice(j), :] = jnp.where(j * tk + key_off <= q_pos, skey, INT_MIN)
        return 0

    _static_loop(n_tiles, score_tile, 0)

    def count(pred):
        def tile(j, cnt):
            return cnt + _tree_sum_rows(pred(skey_ref[tile_slice(j), :]).astype(F32))
        cnt = _static_loop(n_tiles, tile, jnp.zeros((8, tq), F32))
        return jnp.sum(cnt, axis=0, keepdims=True)

    kf = float(k_sel)
    thr = jnp.where(count(lambda sk: sk >= 0) >= kf, 0, INT_MIN).astype(I32)

    def bit_step(b, thr):
        cand = thr + jnp.left_shift(jnp.int32(1), 30 - b)
        return jnp.where(count(lambda sk: sk >= cand) >= kf, cand, thr)

    thr = lax.fori_loop(0, 31, bit_step, thr)
    need = kf - count(lambda sk: sk > thr)

    r2 = lax.broadcasted_iota(I32, (tk, tk), 0)
    c2 = lax.broadcasted_iota(I32, (tk, tk), 1)
    upto = (c2 <= r2).astype(BF16)

    def tie_tile(j, seen):
        sk = skey_ref[tile_slice(j), :]
        tied = sk == thr
        prefix = _dot(upto, tied.astype(F32).astype(BF16)) + seen
        selected = ((sk > thr) | (tied & (prefix <= need))) & (j * tk + key_off <= q_pos)
        bias_ref[tile_slice(j), :] = jnp.where(selected, 0.0, MASKED)
        return prefix[tk - 1:tk, :]

    _static_loop(n_tiles, tie_tile, jnp.zeros((1, tq), F32))

    log2_scale = (DSA_LATENT ** -0.5) * LOG2_E

    def head_group(g, _):
        heads = [g * hg + u for u in range(hg)]
        offs = [pl.multiple_of(h * DSA_LATENT, DSA_LATENT) for h in heads]
        qg = jnp.concatenate(
            [jnp.concatenate([q_ref[0, :, pl.ds(off, DSA_LATENT)],
                              jnp.broadcast_to(qx_ref[pl.ds(h, 1), :], (tq, LANES)).astype(BF16)], axis=1)
             for h, off in zip(heads, offs)], axis=0)

        def logits_tile(j, m):
            d = _nt_dot(cn_ref[tile_slice(j), :], qg)
            b = bias_ref[tile_slice(j), :]
            tops = []
            for u in range(hg):
                s = d[:, u * tq:(u + 1) * tq] * log2_scale + b
                s_ref[tile_slice(j), u * tq:(u + 1) * tq] = s
                tops.append(jnp.max(_tree_max_rows(s), axis=0, keepdims=True))
            return jnp.maximum(m, jnp.concatenate(tops, axis=1))

        m = _static_loop(n_tiles, logits_tile, jnp.full((1, hg * tq), MASKED, F32))

        def value_tile(j, acc):
            p = jnp.exp2((s_ref[tile_slice(j), :] - m).astype(BF16))
            return acc + _dot(cnt_ref[:, tile_slice(j)], p)

        acc = _static_loop(n_tiles, value_tile, jnp.zeros((DSA_LATENT + BF16_ROWS, hg * tq), F32))
        out = (acc[:DSA_LATENT] / acc[DSA_LATENT:DSA_LATENT + 1]).astype(BF16)
        for u in range(0, hg, 2):
            pair = jnp.concatenate([out[:, u * tq:(u + 1) * tq], out[:, (u + 1) * tq:(u + 2) * tq]], axis=0)
            up = _dot(wup_ref[(g * hg + u) // 2], pair)
            off = pl.multiple_of((g * hg + u) * DSA_VDIM, 2 * DSA_VDIM)
            o_ref[0, :, pl.ds(off, 2 * DSA_VDIM)] = up.T.astype(o_ref.dtype)
        return 0

    lax.fori_loop(0, DSA_HEADS // hg, head_group, 0)


def _bf16_parts(x, n):
    parts = []
    for _ in range(n):
        p = x.astype(BF16).astype(F32)
        parts.append(p)
        x = x - p
    return parts


def _dsa_attention(proj, small, kvw, w_uv, bsz, seq):
    tq = min(TQ_DSA, seq)
    tk = min(TRI, seq)
    k_sel = min(TOPK_MAX, seq // 4)
    qw = DSA_HEADS * DSA_LATENT
    qiw = IDX_HEADS * IDX_DIM
    slope = _alibi_slopes(DSA_HEADS) / (DSA_LATENT ** -0.5)
    s_parts = _bf16_parts(slope, 3)
    pos = jnp.arange(seq, dtype=I32)
    p_parts = [(pos // LANES * LANES).astype(F32), (pos % LANES).astype(F32)]
    qx = jnp.stack([sp for _ in p_parts for sp in s_parts], axis=1)
    kx = jnp.stack([pp for pp in p_parts for _ in s_parts], axis=1)
    qx = jnp.pad(qx, ((0, 0), (0, LANES - qx.shape[1])))
    kx = jnp.pad(kx, ((0, 0), (0, LANES - kx.shape[1]))).astype(BF16)
    wt = jnp.swapaxes(w_uv, 1, 2).reshape(DSA_HEADS // 2, 2, DSA_VDIM, DSA_LATENT)
    zero = jnp.zeros_like(wt[:, 0])
    wup = jnp.concatenate([jnp.concatenate([wt[:, 0], zero], axis=2),
                           jnp.concatenate([zero, wt[:, 1]], axis=2)], axis=1).astype(BF16)
    return pl.pallas_call(
        functools.partial(_dsa_kernel, tq=tq, tk=tk, k_sel=k_sel, hg=HG_DSA),
        grid=(bsz, seq // tq),
        in_specs=[pl.BlockSpec((1, tq, qw), lambda b, i: (b, i, 0)),
                  pl.BlockSpec((1, seq, DSA_LATENT), lambda b, i: (b, 0, 0)),
                  pl.BlockSpec((1, DSA_LATENT), lambda b, i: (0, 0)),
                  pl.BlockSpec((1, tq, qiw), lambda b, i: (b, i, qw // qiw)),
                  pl.BlockSpec((1, seq, LANES), lambda b, i: (b, 0, (qw + qiw) // LANES)),
                  pl.BlockSpec((1, tq, LANES), lambda b, i: (b, i, 1)),
                  pl.BlockSpec((DSA_HEADS, LANES), lambda b, i: (0, 0)),
                  pl.BlockSpec((seq, LANES), lambda b, i: (0, 0)),
                  pl.BlockSpec(wup.shape, lambda b, i: (0, 0, 0))],
        out_specs=pl.BlockSpec((1, tq, DSA_HEADS * DSA_VDIM), lambda b, i: (b, i, 0)),
        out_shape=jax.ShapeDtypeStruct((bsz, seq, DSA_HEADS * DSA_VDIM), BF16),
        scratch_shapes=[pltpu.VMEM((seq, DSA_LATENT + LANES), BF16),
                        pltpu.VMEM((DSA_LATENT + BF16_ROWS, seq), BF16),
                        pltpu.VMEM((seq, tq), I32), pltpu.VMEM((seq, tq), F32),
                        pltpu.VMEM((seq, HG_DSA * tq), F32)],
        compiler_params=_cparams("parallel", "arbitrary"),
        name="dsa_attn",
    )(proj, small, kvw, proj, proj, small, qx, kx, wup)


def _mix_tail(rows, mix, h_ref, g_ref, b_ref, rw_ref, rb_ref, hout_ref, logit_ref):
    hn = _layer_norm(DEEPNORM_ALPHA * h_ref[rows, :] + mix, g_ref[...], b_ref[...])
    hout_ref[rows, :] = hn
    logit_ref[rows, :] = _dot(hn.astype(BF16), rw_ref[...]) + rb_ref[...]


def _row_halves(n):
    return [pl.ds(0, n // 2), pl.ds(n // 2, n // 2)]


def _mix_even_kernel(oa_ref, ob_ref, wa_ref, wb_ref, h_ref, g_ref, b_ref, rw_ref, rb_ref, hout_ref, logit_ref):
    for rows in _row_halves(h_ref.shape[0]):
        mix = _dot(oa_ref[rows, :], wa_ref[...]) + _dot(ob_ref[rows, :], wb_ref[...])
        _mix_tail(rows, mix, h_ref, g_ref, b_ref, rw_ref, rb_ref, hout_ref, logit_ref)


def _mix_odd_kernel(o_ref, wo_ref, h_ref, g_ref, b_ref, rw_ref, rb_ref, hout_ref, logit_ref):
    for rows in _row_halves(h_ref.shape[0]):
        _mix_tail(rows, _dot(o_ref[rows, :], wo_ref[...]), h_ref, g_ref, b_ref, rw_ref, rb_ref, hout_ref, logit_ref)


def _mix_call(kernel, name, acts, weights, h, g, b, rw, rb):
    t, d = h.shape
    tm = min(TM_PROJ, t)
    row = lambda n: pl.BlockSpec((tm, n), lambda i: (i, 0))
    full = lambda a: pl.BlockSpec(a.shape, lambda i: (0, 0))
    return pl.pallas_call(
        kernel,
        grid=(t // tm,),
        in_specs=[row(a.shape[1]) for a in acts] + [full(w) for w in weights]
                 + [row(d), full(g), full(b), full(rw), full(rb)],
        out_specs=[row(d), row(LANES)],
        out_shape=[jax.ShapeDtypeStruct((t, d), F32), jax.ShapeDtypeStruct((t, LANES), F32)],
        compiler_params=_cparams("parallel"),
        name=name,
    )(*acts, *weights, h, g, b, rw, rb)


def _route_kernel(logit_ref, idx_ref, gate_ref, rank_ref, cnt_ref, *, tm):
    lane = lax.broadcasted_iota(I32, (tm, LANES), 1)
    lane_f = lane.astype(F32)
    x = jnp.where(lane < N_EXPERTS, logit_ref[...], -jnp.inf)
    vals, hots = [], []
    for k in range(TOP_K):
        m = jnp.max(x, axis=-1, keepdims=True)
        first = jnp.min(jnp.where(x == m, lane_f, float(LANES)), axis=-1, keepdims=True)
        hot = lane_f == first
        x = jnp.where(hot, -jnp.inf, x)
        vals.append(m)
        hots.append(hot)
        idx_ref[:, k:k + 1] = first.astype(I32)
    exps = [jnp.exp(v - vals[0]) for v in vals]
    denom = exps[0] + exps[1] + exps[2] + exps[3]
    for k in range(TOP_K):
        gate_ref[:, k:k + 1] = exps[k] / denom

    hot_sum = (hots[0] | hots[1] | hots[2] | hots[3]).astype(F32)
    r = lax.broadcasted_iota(I32, (tm, tm), 0)
    c = lax.broadcasted_iota(I32, (tm, tm), 1)
    earlier = (c < r).astype(BF16)
    before = _dot(earlier, hot_sum.astype(BF16))
    for k in range(TOP_K):
        rank_ref[:, k:k + 1] = jnp.sum(jnp.where(hots[k], before, 0.0), axis=-1, keepdims=True).astype(I32)
    cnt_ref[0] = jnp.sum(hot_sum, axis=0, keepdims=True)


def _route(logits):
    t = logits.shape[0]
    tm = min(TD_DISPATCH, t)
    narrow = pl.BlockSpec((tm, TOP_K), lambda i: (i, 0))
    return pl.pallas_call(
        functools.partial(_route_kernel, tm=tm),
        grid=(t // tm,),
        in_specs=[pl.BlockSpec((tm, LANES), lambda i: (i, 0))],
        out_specs=[narrow, narrow, narrow, pl.BlockSpec((1, 1, LANES), lambda i: (i, 0, 0))],
        out_shape=[jax.ShapeDtypeStruct((t, TOP_K), I32), jax.ShapeDtypeStruct((t, TOP_K), F32),
                   jax.ShapeDtypeStruct((t, TOP_K), I32), jax.ShapeDtypeStruct((t // tm, 1, LANES), F32)],
        compiler_params=_cparams("parallel"),
        name="route",
    )(logits)


def _chunked_copy(src_ref, src_off, dst_ref, dst_off, count, largest, sem, start_not_wait):
    chunk = largest
    while chunk >= SUBLANES:
        present = (count & chunk) != 0
        copy = pltpu.make_async_copy(src_ref.at[pl.ds(pl.multiple_of(src_off, SUBLANES), chunk), :],
                                     dst_ref.at[pl.ds(pl.multiple_of(dst_off, SUBLANES), chunk), :], sem)

        @pl.when(jnp.logical_and(present, start_not_wait))
        def _():
            copy.start()

        @pl.when(jnp.logical_and(present, jnp.logical_not(start_not_wait)))
        def _():
            copy.wait()

        step = jnp.where(present, chunk, 0)
        src_off = src_off + step
        dst_off = dst_off + step
        chunk //= 2


def _dispatch_kernel(seg_rows_ref, seg_src_ref, seg_dst_ref, pad_start_ref, pad_count_ref, n_used_ref,
                     lp_ref, x_ref, xs_out, staged_ref, zero_ref, sem, pad_sem, *, td):
    i = pl.program_id(0)
    tile = zero_ref.shape[0]

    @pl.when(i == 0)
    def _():
        zero_ref[...] = jnp.zeros_like(zero_ref)

        def pads(e, start_not_wait):
            _chunked_copy(zero_ref, 0, xs_out, pad_start_ref[e], pad_count_ref[e], tile // 2, pad_sem,
                          start_not_wait)
            return start_not_wait

        def tail(j, start_not_wait):
            present = j >= n_used_ref[0]
            copy = pltpu.make_async_copy(zero_ref, xs_out.at[pl.ds(pl.multiple_of(j * tile, tile), tile), :], pad_sem)

            @pl.when(jnp.logical_and(present, start_not_wait))
            def _():
                copy.start()

            @pl.when(jnp.logical_and(present, jnp.logical_not(start_not_wait)))
            def _():
                copy.wait()
            return start_not_wait

        n_tiles = xs_out.shape[0] // tile
        lax.fori_loop(0, N_EXPERTS, pads, True)
        lax.fori_loop(0, n_tiles, tail, True)
        lax.fori_loop(0, N_EXPERTS, pads, False)
        lax.fori_loop(0, n_tiles, tail, False)

    rows = staged_ref.shape[0]
    r_iota = lax.broadcasted_iota(I32, (rows, td), 0)
    hit = r_iota == lp_ref[0, 0:1, :]
    for k in range(1, TOP_K):
        hit = hit | (r_iota == lp_ref[0, k:k + 1, :])
    staged_ref[...] = _dot(jnp.where(hit, 1.0, 0.0).astype(BF16), x_ref[...].astype(BF16))

    def runs(e, start_not_wait):
        s = i * N_EXPERTS + e
        _chunked_copy(staged_ref, seg_src_ref[s], xs_out, seg_dst_ref[s], seg_rows_ref[s], td, sem, start_not_wait)
        return start_not_wait

    lax.fori_loop(0, N_EXPERTS, runs, True)
    lax.fori_loop(0, N_EXPERTS, runs, False)


def _dispatch(seg_rows, seg_src, seg_dst, pad_start, pad_count, n_used, lp, x, n_rows):
    t, d = x.shape
    td = min(TD_DISPATCH, t)
    staged_rows = td * TOP_K + N_EXPERTS * SUBLANES
    grid_spec = pltpu.PrefetchScalarGridSpec(
        num_scalar_prefetch=6,
        grid=(t // td,),
        in_specs=[pl.BlockSpec((1, TOP_K, td), lambda i, *_: (i, 0, 0)),
                  pl.BlockSpec((td, d), lambda i, *_: (i, 0))],
        out_specs=pl.BlockSpec(memory_space=pl.ANY),
        scratch_shapes=[pltpu.VMEM((staged_rows, d), F32), pltpu.VMEM((TM_FFN, d), F32),
                        pltpu.SemaphoreType.DMA, pltpu.SemaphoreType.DMA],
    )
    return pl.pallas_call(
        functools.partial(_dispatch_kernel, td=td),
        grid_spec=grid_spec,
        out_shape=jax.ShapeDtypeStruct((n_rows, d), F32),
        compiler_params=_cparams("arbitrary"),
        name="moe_dispatch",
    )(seg_rows, seg_src, seg_dst, pad_start, pad_count, n_used, lp, x)


def _ffn_kernel(tile_expert_ref, n_used_ref, xs_ref, wgu_ref, bgu_ref, wd_ref, bd_ref, y_ref, wgu_bf, wd_bf):
    i = pl.program_id(0)
    d_ff = wd_ref.shape[2]
    new_expert = jnp.logical_or(i == 0, tile_expert_ref[i] != tile_expert_ref[jnp.maximum(i - 1, 0)])

    @pl.when(jnp.logical_and(new_expert, i < n_used_ref[0]))
    def _():
        wgu_bf[...] = wgu_ref[0, 0].astype(BF16)
        wd_bf[...] = wd_ref[0, 0].astype(BF16)

    @pl.when(i < n_used_ref[0])
    def _():
        x = xs_ref[...].astype(BF16)
        hgu = _dot(x, wgu_bf[...]) + bgu_ref[0, 0]
        gate = jnp.minimum(hgu[:, :d_ff], SWIGLU_LIMIT)
        up = jnp.clip(hgu[:, d_ff:], -SWIGLU_LIMIT, SWIGLU_LIMIT)
        act = gate * (1.0 / (1.0 + jnp.exp(-SWIGLU_ALPHA * gate))) * (up + 1.0)
        y_ref[...] = _dot(act.astype(BF16), wd_bf[...]) + bd_ref[0, 0]

    @pl.when(i >= n_used_ref[0])
    def _():
        y_ref[...] = jnp.zeros_like(y_ref)


def _expert_ffn(tile_expert, n_used, xs, w_gu, b_gu, w_down, b_down, layer):
    n_rows, d = xs.shape
    depth, e, _, f2 = w_gu.shape
    d_ff = w_down.shape[2]
    tm = TM_FFN
    grid_spec = pltpu.PrefetchScalarGridSpec(
        num_scalar_prefetch=2,
        grid=(n_rows // tm,),
        in_specs=[pl.BlockSpec((tm, d), lambda i, te, nu: (jnp.minimum(i, nu[0] - 1), 0)),
                  pl.BlockSpec((1, 1, d, f2), lambda i, te, nu: (layer, te[i], 0, 0)),
                  pl.BlockSpec((1, 1, 1, f2), lambda i, te, nu: (layer, te[i], 0, 0)),
                  pl.BlockSpec((1, 1, d_ff, d), lambda i, te, nu: (layer, te[i], 0, 0)),
                  pl.BlockSpec((1, 1, 1, d), lambda i, te, nu: (layer, te[i], 0, 0))],
        out_specs=pl.BlockSpec((tm, d), lambda i, te, nu: (i, 0)),
        scratch_shapes=[pltpu.VMEM((d, f2), BF16), pltpu.VMEM((d_ff, d), BF16)],
    )
    return pl.pallas_call(
        _ffn_kernel,
        grid_spec=grid_spec,
        out_shape=jax.ShapeDtypeStruct((n_rows, d), F32),
        compiler_params=_cparams("arbitrary"),
        name="expert_ffn",
    )(tile_expert, n_used, xs, w_gu, b_gu.reshape(depth, e, 1, f2), w_down, b_down.reshape(depth, e, 1, d))


def _combine_kernel(seg_rows_ref, seg_src_ref, seg_dst_ref, lp_ref, gate_ref, h_ref, g_ref, b_ref, y_hbm, out_ref,
                    staged_ref, ffn_ref, sem, *, tc):
    i = pl.program_id(0)

    def runs(e, start_not_wait):
        s = i * N_EXPERTS + e
        _chunked_copy(y_hbm, seg_dst_ref[s], staged_ref, seg_src_ref[s], seg_rows_ref[s], tc, sem, start_not_wait)
        return start_not_wait

    lax.fori_loop(0, N_EXPERTS, runs, True)
    lax.fori_loop(0, N_EXPERTS, runs, False)

    def token(t, _):
        acc = gate_ref[t * TOP_K] * staged_ref[pl.ds(lp_ref[t * TOP_K], 1), :]
        for k in range(1, TOP_K):
            acc = acc + gate_ref[t * TOP_K + k] * staged_ref[pl.ds(lp_ref[t * TOP_K + k], 1), :]
        ffn_ref[pl.ds(t, 1), :] = acc
        return 0

    lax.fori_loop(0, tc, token, 0, unroll=4)
    out_ref[...] = _layer_norm(DEEPNORM_ALPHA * h_ref[...] + ffn_ref[...], g_ref[...], b_ref[...])


def _combine(seg_rows, seg_src, seg_dst, lp_flat, gates_flat, h, g, b, y):
    t, d = h.shape
    tc = min(TD_DISPATCH, t)
    staged_rows = tc * TOP_K + N_EXPERTS * SUBLANES
    grid_spec = pltpu.PrefetchScalarGridSpec(
        num_scalar_prefetch=3,
        grid=(t // tc,),
        in_specs=[pl.BlockSpec((tc * TOP_K,), lambda i, *_: (i,), memory_space=pltpu.SMEM),
                  pl.BlockSpec((tc * TOP_K,), lambda i, *_: (i,), memory_space=pltpu.SMEM),
                  pl.BlockSpec((tc, d), lambda i, *_: (i, 0)),
                  pl.BlockSpec((1, d), lambda i, *_: (0, 0)),
                  pl.BlockSpec((1, d), lambda i, *_: (0, 0)),
                  pl.BlockSpec(memory_space=pl.ANY)],
        out_specs=pl.BlockSpec((tc, d), lambda i, *_: (i, 0)),
        scratch_shapes=[pltpu.VMEM((staged_rows, d), F32), pltpu.VMEM((tc, d), F32), pltpu.SemaphoreType.DMA],
    )
    return pl.pallas_call(
        functools.partial(_combine_kernel, tc=tc),
        grid_spec=grid_spec,
        out_shape=jax.ShapeDtypeStruct((t, d), F32),
        compiler_params=_cparams("arbitrary"),
        name="moe_combine",
    )(seg_rows, seg_src, seg_dst, lp_flat, gates_flat, h, g, b, y)


def _moe(h, logits, w_gu, b_gu, w_down, b_down, g, b, layer):
    t = h.shape[0]
    td = min(TD_DISPATCH, t)
    n_tok_tiles = t // td
    idx, gates, rank, counts = _route(logits)
    counts = counts[:, 0, :N_EXPERTS].astype(I32)
    seg = (counts + SUBLANES - 1) // SUBLANES * SUBLANES
    rows = jnp.sum(seg, axis=0)
    tiles = (rows + TM_FFN - 1) // TM_FFN
    tile_end = jnp.cumsum(tiles)
    start = (tile_end - tiles) * TM_FFN
    seg_dst = start[None, :] + jnp.cumsum(seg, axis=0) - seg
    seg_src = jnp.cumsum(seg, axis=1) - seg
    n_tiles = -(-(t * TOP_K + n_tok_tiles * N_EXPERTS * (SUBLANES - 1)) // TM_FFN) + N_EXPERTS
    tile_ids = jnp.arange(n_tiles, dtype=I32)
    tile_expert = jnp.minimum(jnp.sum((tile_end[None, :] <= tile_ids[:, None]).astype(I32), axis=1), N_EXPERTS - 1)
    chosen = idx.reshape(n_tok_tiles, td, TOP_K, 1) == jnp.arange(N_EXPERTS, dtype=I32)
    pick = lambda table: jnp.sum(jnp.where(chosen, table[:, None, None, :], 0), axis=-1)
    rank = rank.reshape(n_tok_tiles, td, TOP_K)
    lp = (pick(seg_src) + rank).astype(I32)
    n_used = tile_end[-1:].astype(I32)
    flat = lambda a: a.reshape(-1).astype(I32)
    xs = _dispatch(flat(seg), flat(seg_src), flat(seg_dst), start + rows, tiles * TM_FFN - rows, n_used,
                   jnp.swapaxes(lp, 1, 2), h, n_tiles * TM_FFN)
    y = _expert_ffn(tile_expert, n_used, xs, w_gu, b_gu, w_down, b_down, layer)
    return _combine(flat(seg), flat(seg_src), flat(seg_dst), flat(lp), gates.reshape(-1), h, g, b, y)


def _router_params(router_w, router_b):
    rw = jnp.pad(router_w, ((0, 0), (0, LANES - N_EXPERTS))).astype(BF16)
    rb = jnp.pad(router_b, (0, LANES - N_EXPERTS)).reshape(1, LANES)
    return rw, rb


def _alibi_slopes(n):
    return 2.0 ** (-8.0 * jnp.arange(1, n + 1, dtype=F32) / n)


def _even_layer(h, bsz, seq, w_in, w_out, lq1, lk1, lq2, lk2, subw, layer, g, b, rw, rb):
    proj = _project(h, [w_in.astype(BF16)], [BF16])[0].reshape(bsz, seq, -1)
    o_a = _sb_attention(proj, bsz, seq)
    lam_init = 0.8 - 0.6 * math.exp(-0.3 * layer)
    row = lambda v: v.reshape(1, -1)
    o_b = _diff_attention(proj, _alibi_slopes(DIFF_HEADS), row(lq1), row(lk1), row(lq2), row(lk2), row(subw),
                          bsz, seq, lam_init)
    t = bsz * seq
    sbw = SB_HEADS * HEAD_DIM
    wo = w_out.astype(BF16)
    return _mix_call(_mix_even_kernel, "mix_even", [o_a.reshape(t, -1), o_b.reshape(t, -1)],
                     [wo[:sbw], wo[sbw:]], h, row(g), row(b), rw, rb)


def _odd_layer(h, bsz, seq, w_in, kvw, w_uv, w_out, g, b, rw, rb):
    qw = DSA_HEADS * DSA_LATENT
    c0, c1 = qw, qw + DSA_LATENT
    i0, i1 = c1, c1 + IDX_HEADS * IDX_DIM
    k1 = i1 + IDX_DIM
    w_main = jnp.concatenate([w_in[:, :qw], w_in[:, i0:i1], w_in[:, i1:k1], w_in[:, i1:k1]], axis=1).astype(BF16)
    w_small = jnp.pad(jnp.concatenate([w_in[:, c0:c1], w_in[:, k1:]], axis=1),
                      ((0, 0), (0, LANES - IDX_HEADS))).astype(BF16)
    proj, small = [p.reshape(bsz, seq, -1) for p in _project(h, [w_main, w_small], [BF16, F32])]
    o = _dsa_attention(proj, small, kvw.reshape(1, -1), w_uv, bsz, seq)
    row = lambda v: v.reshape(1, -1)
    return _mix_call(_mix_odd_kernel, "mix_odd", [o.reshape(bsz * seq, -1)], [w_out.astype(BF16)],
                     h, row(g), row(b), rw, rb)


def kernel(x, ev_w_in, ev_w_out, ev_lambda_q1, ev_lambda_k1, ev_lambda_q2, ev_lambda_k2, ev_subln_w, od_w_in, od_kv_norm_w, od_w_uv, od_w_out, ln_mix_g, ln_mix_b, router_w, router_b, exp_w_gu, exp_b_gu, exp_w_down, exp_b_down, ln_ffn_g, ln_ffn_b):
    bsz, seq, d = x.shape
    h = x.reshape(bsz * seq, d)
    for layer in range(ln_mix_g.shape[0]):
        j = layer // 2
        rw, rb = _router_params(router_w[layer], router_b[layer])
        if layer % 2 == 0:
            h, logits = _even_layer(h, bsz, seq, ev_w_in[j], ev_w_out[j], ev_lambda_q1[j], ev_lambda_k1[j],
                                    ev_lambda_q2[j], ev_lambda_k2[j], ev_subln_w[j], layer,
                                    ln_mix_g[layer], ln_mix_b[layer], rw, rb)
        else:
            h, logits = _odd_layer(h, bsz, seq, od_w_in[j], od_kv_norm_w[j], od_w_uv[j], od_w_out[j],
                                   ln_mix_g[layer], ln_mix_b[layer], rw, rb)
        h = _moe(h, logits, exp_w_gu, exp_b_gu, exp_w_down, exp_b_down,
                 ln_ffn_g[layer].reshape(1, -1), ln_ffn_b[layer].reshape(1, -1), layer)
    return h.reshape(bsz, seq, d)
```

```python
import functools
import math

import jax
import jax.numpy as jnp
from jax import lax
from jax.experimental import pallas as pl
from jax.experimental.pallas import tpu as pltpu

F32, BF16, I32 = jnp.float32, jnp.bfloat16, jnp.int32

HEAD_DIM = 64
SB_HEADS = 8
DIFF_HEADS = 4
DIFF_SUB = 64
DIFF_VDIM = 128
DSA_HEADS = 16
DSA_LATENT = 128
DSA_VDIM = 64
IDX_HEADS = 8
IDX_DIM = 64
TOPK_MAX = 256
N_EXPERTS = 32
TOP_K = 4
SWIGLU_LIMIT = 7.0
SWIGLU_ALPHA = 1.702
LN_EPS = 1e-5
RMS_EPS = 1e-5
DEPTH = 2
DEEPNORM_ALPHA = (2 * DEPTH) ** 0.25

LANES = 128
SUBLANES = 8
BF16_ROWS = 16
VMEM_LIMIT = 56 * 1024 * 1024
INT_MIN = -(2 ** 31)
MASKED = -1e30
DEAD_LOG2 = -152.0
LOG2_E = math.log2(math.e)

TM_PROJ = 512
TQ_ATT = 256
TQ_DSA = 128
TRI = 512
HG_DSA = 16
TM_FFN = 512
TD_DISPATCH = 512


def _cparams(*sem):
    return pltpu.CompilerParams(dimension_semantics=sem, vmem_limit_bytes=VMEM_LIMIT)


def _nt_dot(a, b):
    return lax.dot_general(a, b, (((1,), (1,)), ((), ())), preferred_element_type=F32)


def _dot(a, b):
    return jnp.dot(a, b, preferred_element_type=F32)


def _layer_norm(y, g, b):
    mu = jnp.mean(y, axis=-1, keepdims=True)
    d = y - mu
    var = jnp.mean(d * d, axis=-1, keepdims=True)
    return d * lax.rsqrt(var + LN_EPS) * g + b


def _col_chunk(n):
    for c in (512, 384, 256, 128):
        if n % c == 0:
            return c
    raise ValueError(n)


def _proj_kernel(x_ref, *refs):
    n_out = len(refs) // 2
    x = x_ref[...].astype(BF16)
    for w_ref, o_ref in zip(refs[:n_out], refs[n_out:]):
        n = o_ref.shape[-1]
        c = _col_chunk(n)
        for j in range(0, n, c):
            o_ref[:, j:j + c] = _dot(x, w_ref[:, j:j + c]).astype(o_ref.dtype)


def _project(x, weights, out_dtypes):
    t, k = x.shape
    tm = min(TM_PROJ, t)
    return pl.pallas_call(
        _proj_kernel,
        grid=(t // tm,),
        in_specs=[pl.BlockSpec((tm, k), lambda i: (i, 0))]
                 + [pl.BlockSpec(w.shape, lambda i: (0, 0)) for w in weights],
        out_specs=[pl.BlockSpec((tm, w.shape[1]), lambda i: (i, 0)) for w in weights],
        out_shape=[jax.ShapeDtypeStruct((t, w.shape[1]), dt) for w, dt in zip(weights, out_dtypes)],
        compiler_params=_cparams("parallel"),
        name="in_proj",
    )(x, *weights)


def _stack_halves(x, lane):
    zero = jnp.zeros_like(x)
    return jnp.concatenate([jnp.where(lane < HEAD_DIM, x, zero), jnp.where(lane >= HEAD_DIM, x, zero)], axis=0)


def _sb_kernel(q_ref, k_ref, v_ref, o_ref, *, tq):
    i = pl.program_id(2)
    scale = HEAD_DIM ** -0.5
    lane = lax.broadcasted_iota(I32, (1, LANES), 1)
    row = lax.broadcasted_iota(I32, (2 * tq, tq), 0)
    row = jnp.where(row >= tq, row - tq, row)
    col = lax.broadcasted_iota(I32, (2 * tq, tq), 1)
    strict = col < row
    r1 = lax.broadcasted_iota(I32, (tq, tq), 0)
    c1 = lax.broadcasted_iota(I32, (tq, tq), 1)
    after = (r1 > c1).astype(BF16)
    q2 = _stack_halves(q_ref[0], lane)

    def block(j, carry, masked):
        run, acc = carry
        start = pl.multiple_of(j * tq, tq)
        kb = k_ref[0, pl.ds(start, tq), :]
        vb = v_ref[0, pl.ds(start, tq), :]
        z2 = _nt_dot(q2, kb) * (scale * LOG2_E)
        neg = -z2
        keep2 = jnp.minimum(neg, 0.0) - jnp.log2(1.0 + jnp.exp2(jnp.minimum(z2, neg)))
        lk = jnp.where(strict, keep2, 0.0) if masked else keep2
        hi = lk.astype(BF16)
        lo = (lk - hi.astype(F32)).astype(BF16)
        suffix = _dot(hi, after) + _dot(lo, after)
        w = jnp.exp2(z2 + keep2 + suffix + run)
        if masked:
            w = jnp.where(strict, w, 0.0)
        wb = w.astype(BF16)
        zero = jnp.zeros_like(vb)
        acc = (acc + _dot(wb[:tq], jnp.where(lane < HEAD_DIM, vb, zero))
               + _dot(wb[tq:], jnp.where(lane >= HEAD_DIM, vb, zero)))
        run = run + suffix[:, 0:1] + lk[:, 0:1]
        return run, acc

    def alive(c):
        return jnp.logical_and(c[0] >= 0, c[1] > DEAD_LOG2)

    def step(c):
        j, _, run, acc = c
        run, acc = block(j, (run, acc), False)
        return j - 1, jnp.max(run), run, acc

    start = (jnp.zeros((2 * tq, 1), F32), jnp.zeros((tq, LANES), F32))

    @pl.when(i == 0)
    def _():
        o_ref[0] = block(0, start, True)[1].astype(o_ref.dtype)

    @pl.when(i > 0)
    def _():
        run, acc = block(i - 1, block(i, start, True), False)
        _, _, _, acc = lax.while_loop(alive, step, (i - 2, jnp.max(run), run, acc))
        o_ref[0] = acc.astype(o_ref.dtype)


def _sb_attention(proj, bsz, seq):
    tq = min(TQ_ATT, seq)
    pairs = SB_HEADS * HEAD_DIM // LANES
    return pl.pallas_call(
        functools.partial(_sb_kernel, tq=tq),
        grid=(bsz, pairs, seq // tq),
        in_specs=[pl.BlockSpec((1, tq, LANES), lambda b, p, i: (b, i, p)),
                  pl.BlockSpec((1, seq, LANES), lambda b, p, i: (b, 0, pairs + p)),
                  pl.BlockSpec((1, seq, LANES), lambda b, p, i: (b, 0, 2 * pairs + p))],
        out_specs=pl.BlockSpec((1, tq, LANES), lambda b, p, i: (b, i, p)),
        out_shape=jax.ShapeDtypeStruct((bsz, seq, SB_HEADS * HEAD_DIM), BF16),
        compiler_params=_cparams("parallel", "parallel", "parallel"),
        name="stickbreak_attn",
    )(proj, proj, proj)


def _diff_kernel(slope_ref, lq1_ref, lk1_ref, lq2_ref, lk2_ref, subw_ref, q_ref, k_ref, v_ref, o_ref,
                 *, tq, lam_init):
    h = pl.program_id(1)
    i = pl.program_id(2)
    scale = DIFF_SUB ** -0.5
    slope = slope_ref[h]
    lam = (jnp.exp(jnp.sum(lq1_ref[...] * lk1_ref[...], keepdims=True))
           - jnp.exp(jnp.sum(lq2_ref[...] * lk2_ref[...], keepdims=True)) + lam_init)
    lane = lax.broadcasted_iota(I32, (1, LANES), 1)
    row = lax.broadcasted_iota(I32, (2 * tq, tq), 0)
    row = jnp.where(row >= tq, row - tq, row)
    col = lax.broadcasted_iota(I32, (2 * tq, tq), 1)
    causal = col <= row
    key_off = lax.broadcasted_iota(I32, (1, tq), 1)
    q2 = _stack_halves(q_ref[0], lane)

    def block(j, carry, masked):
        m, l, acc = carry
        start = j * tq
        kb = k_ref[0, pl.ds(start, tq), :]
        vb = v_ref[0, pl.ds(start, tq), :]
        s = _nt_dot(q2, kb) * (scale * LOG2_E) + (slope * LOG2_E) * (start + key_off).astype(F32)
        if masked:
            s = jnp.where(causal, s, -jnp.inf)
        m_new = jnp.maximum(m, jnp.max(s, axis=-1, keepdims=True))
        a = jnp.exp2(m - m_new)
        p = jnp.exp2(s - m_new)
        l = a * l + jnp.sum(p, axis=-1, keepdims=True)
        acc = a * acc + _dot(p.astype(BF16), vb)
        return m_new, l, acc

    for n_left in range(k_ref.shape[1] // tq):
        @pl.when(i == n_left)
        def _(n_left=n_left):
            carry = (jnp.full((2 * tq, 1), -jnp.inf, F32), jnp.zeros((2 * tq, 1), F32),
                     jnp.zeros((2 * tq, LANES), F32))
            carry = block(n_left, carry, True)
            for j in range(n_left):
                carry = block(j, carry, False)
            out = carry[2] / carry[1]
            o = out[:tq] - lam * out[tq:]
            o = o * lax.rsqrt(jnp.mean(o * o, axis=-1, keepdims=True) + RMS_EPS) * subw_ref[...]
            o_ref[0] = (o * (1.0 - lam_init)).astype(o_ref.dtype)


def _diff_attention(proj, slopes, lq1, lk1, lq2, lk2, subw, bsz, seq, lam_init):
    tq = min(TQ_ATT, seq)
    q0 = 3 * SB_HEADS * HEAD_DIM // LANES
    k0 = q0 + DIFF_HEADS
    v0 = k0 + DIFF_HEADS
    vec = lambda n: pl.BlockSpec((1, n), lambda b, h, i: (0, 0))
    return pl.pallas_call(
        functools.partial(_diff_kernel, tq=tq, lam_init=lam_init),
        grid=(bsz, DIFF_HEADS, seq // tq),
        in_specs=[pl.BlockSpec(memory_space=pltpu.SMEM),
                  vec(DIFF_SUB), vec(DIFF_SUB), vec(DIFF_SUB), vec(DIFF_SUB), vec(DIFF_VDIM),
                  pl.BlockSpec((1, tq, LANES), lambda b, h, i: (b, i, q0 + h)),
                  pl.BlockSpec((1, seq, LANES), lambda b, h, i: (b, 0, k0 + h)),
                  pl.BlockSpec((1, seq, LANES), lambda b, h, i: (b, 0, v0 + h))],
        out_specs=pl.BlockSpec((1, tq, LANES), lambda b, h, i: (b, i, h)),
        out_shape=jax.ShapeDtypeStruct((bsz, seq, DIFF_HEADS * DIFF_VDIM), BF16),
        compiler_params=_cparams("parallel", "parallel", "parallel"),
        name="diff_attn",
    )(slopes, lq1, lk1, lq2, lk2, subw, proj, proj, proj)


def _tree_sum_rows(x):
    n = x.shape[0] // 8
    x = x.reshape(n, 8, x.shape[1])
    while n > 1:
        n //= 2
        x = x[:n] + x[n:]
    return x[0]


def _tree_max_rows(x):
    n = x.shape[0] // 8
    x = x.reshape(n, 8, x.shape[1])
    while n > 1:
        n //= 2
        x = jnp.maximum(x[:n], x[n:])
    return x[0]


def _dsa_kernel(q_ref, ckv_ref, kvw_ref, qi_ref, ki_ref, wi_ref, qx_ref, kx_ref, wup_ref, o_ref,
                cn_ref, cnt_ref, skey_ref, bias_ref, s_ref, *, tq, tk, k_sel, hg):
    i = pl.program_id(1)
    t0 = i * tq

    @pl.when(i == 0)
    def _():
        c = ckv_ref[0]
        cn = c * lax.rsqrt(jnp.mean(c * c, axis=-1, keepdims=True) + RMS_EPS) * kvw_ref[...]
        cn_ref[:, :DSA_LATENT] = cn.astype(BF16)
        cn_ref[:, DSA_LATENT:] = kx_ref[...]
        cnt_ref[:DSA_LATENT, :] = cn.T.astype(BF16)
        row = lax.broadcasted_iota(I32, (BF16_ROWS, cnt_ref.shape[1]), 0)
        cnt_ref[DSA_LATENT:, :] = jnp.where(row == 0, 1.0, 0.0).astype(BF16)

    for k in range(ckv_ref.shape[1] // tk):
        @pl.when((t0 + tq - 1) // tk == k)
        def _(k=k):
            _dsa_block(q_ref, qi_ref, ki_ref, wi_ref, qx_ref, wup_ref, o_ref, cn_ref, cnt_ref, skey_ref, bias_ref,
                       s_ref, t0=t0, n_tiles=k + 1, tq=tq, tk=tk, k_sel=k_sel, hg=hg)


def _static_loop(n, body, carry):
    for j in range(n):
        carry = body(j, carry)
    return carry


def _dsa_block(q_ref, qi_ref, ki_ref, wi_ref, qx_ref, wup_ref, o_ref, cn_ref, cnt_ref, skey_ref, bias_ref, s_ref,
               *, t0, n_tiles, tq, tk, k_sel, hg):
    def tile_slice(j):
        return pl.ds(j * tk, tk)

    lane = lax.broadcasted_iota(I32, (1, LANES), 1)
    q_pos = t0 + lax.broadcasted_iota(I32, (tk, tq), 1)
    key_off = lax.broadcasted_iota(I32, (tk, tq), 0)

    w_t = (wi_ref[0] * ((IDX_HEADS ** -0.5) * (IDX_DIM ** -0.5))).T
    stacked = []
    for h in range(IDX_HEADS):
        qp = qi_ref[0, :, (h // 2) * LANES:(h // 2 + 1) * LANES]
        stacked.append(jnp.where((lane // IDX_DIM) == (h % 2), qp, jnp.zeros_like(qp)))
    q8 = jnp.concatenate(stacked, axis=0)

    def score_tile(j, _):
        d = _nt_dot(ki_ref[0, tile_slice(j), :], q8)
        score = jnp.zeros((tk, tq), F32)
        for h in range(IDX_HEADS):
            score = score + jnp.maximum(d[:, h * tq:(h + 1) * tq], 0.0) * w_t[h:h + 1, :]
        score = jnp.where(score == 0.0, 0.0, score)
        bits = pltpu.bitcast(score, I32)
        skey = jnp.where(bits < 0, bits ^ 0x7FFFFFFF, bits)
        skey_ref[tile_slice(j), :] = jnp.where(j * tk + key_off <= q_pos, skey, INT_MIN)
        return 0

    _static_loop(n_tiles, score_tile, 0)

    def count(pred):
        def tile(j, cnt):
            return cnt + _tree_sum_rows(pred(skey_ref[tile_slice(j), :]).astype(F32))
        cnt = _static_loop(n_tiles, tile, jnp.zeros((8, tq), F32))
        return jnp.sum(cnt, axis=0, keepdims=True)

    kf = float(k_sel)
    thr = jnp.where(count(lambda sk: sk >= 0) >= kf, 0, INT_MIN).astype(I32)

    def bit_step(b, thr):
        cand = thr + jnp.left_shift(jnp.int32(1), 30 - b)
        return jnp.where(count(lambda sk: sk >= cand) >= kf, cand, thr)

    thr = lax.fori_loop(0, 31, bit_step, thr)
    need = kf - count(lambda sk: sk > thr)

    r2 = lax.broadcasted_iota(I32, (tk, tk), 0)
    c2 = lax.broadcasted_iota(I32, (tk, tk), 1)
    upto = (c2 <= r2).astype(BF16)

    def tie_tile(j, seen):
        sk = skey_ref[tile_slice(j), :]
        tied = sk == thr
        prefix = _dot(upto, tied.astype(F32).astype(BF16)) + seen
        selected = ((sk > thr) | (tied & (prefix <= need))) & (j * tk + key_off <= q_pos)
        bias_ref[tile_slice(j), :] = jnp.where(selected, 0.0, MASKED)
        return prefix[tk - 1:tk, :]

    _static_loop(n_tiles, tie_tile, jnp.zeros((1, tq), F32))

    log2_scale = (DSA_LATENT ** -0.5) * LOG2_E

    def head_group(g, _):
        heads = [g * hg + u for u in range(hg)]
        offs = [pl.multiple_of(h * DSA_LATENT, DSA_LATENT) for h in heads]
        qg = jnp.concatenate(
            [jnp.concatenate([q_ref[0, :, pl.ds(off, DSA_LATENT)],
                              jnp.broadcast_to(qx_ref[pl.ds(h, 1), :], (tq, LANES)).astype(BF16)], axis=1)
             for h, off in zip(heads, offs)], axis=0)

        def logits_tile(j, m):
            d = _nt_dot(cn_ref[tile_slice(j), :], qg)
            b = bias_ref[tile_slice(j), :]
            tops = []
            for u in range(hg):
                s = d[:, u * tq:(u + 1) * tq] * log2_scale + b
                s_ref[tile_slice(j), u * tq:(u + 1) * tq] = s
                tops.append(jnp.max(_tree_max_rows(s), axis=0, keepdims=True))
            return jnp.maximum(m, jnp.concatenate(tops, axis=1))

        m = _static_loop(n_tiles, logits_tile, jnp.full((1, hg * tq), MASKED, F32))

        def value_tile(j, acc):
            p = jnp.exp2((s_ref[tile_slice(j), :] - m).astype(BF16))
            return acc + _dot(cnt_ref[:, tile_slice(j)], p)

        acc = _static_loop(n_tiles, value_tile, jnp.zeros((DSA_LATENT + BF16_ROWS, hg * tq), F32))
        out = (acc[:DSA_LATENT] / acc[DSA_LATENT:DSA_LATENT + 1]).astype(BF16)
        for u in range(0, hg, 2):
            pair = jnp.concatenate([out[:, u * tq:(u + 1) * tq], out[:, (u + 1) * tq:(u + 2) * tq]], axis=0)
            up = _dot(wup_ref[(g * hg + u) // 2], pair)
            off = pl.multiple_of((g * hg + u) * DSA_VDIM, 2 * DSA_VDIM)
            o_ref[0, :, pl.ds(off, 2 * DSA_VDIM)] = up.T.astype(o_ref.dtype)
        return 0

    lax.fori_loop(0, DSA_HEADS // hg, head_group, 0)


def _bf16_parts(x, n):
    parts = []
    for _ in range(n):
        p = x.astype(BF16).astype(F32)
        parts.append(p)
        x = x - p
    return parts


def _dsa_attention(proj, small, kvw, w_uv, bsz, seq):
    tq = min(TQ_DSA, seq)
    tk = min(TRI, seq)
    k_sel = min(TOPK_MAX, seq // 4)
    qw = DSA_HEADS * DSA_LATENT
    qiw = IDX_HEADS * IDX_DIM
    slope = _alibi_slopes(DSA_HEADS) / (DSA_LATENT ** -0.5)
    s_parts = _bf16_parts(slope, 3)
    pos = jnp.arange(seq, dtype=I32)
    p_parts = [(pos // LANES * LANES).astype(F32), (pos % LANES).astype(F32)]
    qx = jnp.stack([sp for _ in p_parts for sp in s_parts], axis=1)
    kx = jnp.stack([pp for pp in p_parts for _ in s_parts], axis=1)
    qx = jnp.pad(qx, ((0, 0), (0, LANES - qx.shape[1])))
    kx = jnp.pad(kx, ((0, 0), (0, LANES - kx.shape[1]))).astype(BF16)
    wt = jnp.swapaxes(w_uv, 1, 2).reshape(DSA_HEADS // 2, 2, DSA_VDIM, DSA_LATENT)
    zero = jnp.zeros_like(wt[:, 0])
    wup = jnp.concatenate([jnp.concatenate([wt[:, 0], zero], axis=2),
                           jnp.concatenate([zero, wt[:, 1]], axis=2)], axis=1).astype(BF16)
    return pl.pallas_call(
        functools.partial(_dsa_kernel, tq=tq, tk=tk, k_sel=k_sel, hg=HG_DSA),
        grid=(bsz, seq // tq),
        in_specs=[pl.BlockSpec((1, tq, qw), lambda b, i: (b, i, 0)),
                  pl.BlockSpec((1, seq, DSA_LATENT), lambda b, i: (b, 0, 0)),
                  pl.BlockSpec((1, DSA_LATENT), lambda b, i: (0, 0)),
                  pl.BlockSpec((1, tq, qiw), lambda b, i: (b, i, qw // qiw)),
                  pl.BlockSpec((1, seq, LANES), lambda b, i: (b, 0, (qw + qiw) // LANES)),
                  pl.BlockSpec((1, tq, LANES), lambda b, i: (b, i, 1)),
                  pl.BlockSpec((DSA_HEADS, LANES), lambda b, i: (0, 0)),
                  pl.BlockSpec((seq, LANES), lambda b, i: (0, 0)),
                  pl.BlockSpec(wup.shape, lambda b, i: (0, 0, 0))],
        out_specs=pl.BlockSpec((1, tq, DSA_HEADS * DSA_VDIM), lambda b, i: (b, i, 0)),
        out_shape=jax.ShapeDtypeStruct((bsz, seq, DSA_HEADS * DSA_VDIM), BF16),
        scratch_shapes=[pltpu.VMEM((seq, DSA_LATENT + LANES), BF16),
                        pltpu.VMEM((DSA_LATENT + BF16_ROWS, seq), BF16),
                        pltpu.VMEM((seq, tq), I32), pltpu.VMEM((seq, tq), F32),
                        pltpu.VMEM((seq, HG_DSA * tq), F32)],
        compiler_params=_cparams("parallel", "arbitrary"),
        name="dsa_attn",
    )(proj, small, kvw, proj, proj, small, qx, kx, wup)


def _mix_tail(rows, mix, h_ref, g_ref, b_ref, rw_ref, rb_ref, hout_ref, logit_ref):
    hn = _layer_norm(DEEPNORM_ALPHA * h_ref[rows, :] + mix, g_ref[...], b_ref[...])
    hout_ref[rows, :] = hn
    logit_ref[rows, :] = _dot(hn.astype(BF16), rw_ref[...]) + rb_ref[...]


def _row_halves(n):
    return [pl.ds(0, n // 2), pl.ds(n // 2, n // 2)]


def _mix_even_kernel(oa_ref, ob_ref, wa_ref, wb_ref, h_ref, g_ref, b_ref, rw_ref, rb_ref,
                     hout_ref, idx_ref, gate_ref, rank_ref, cnt_ref, logit_ref):
    for rows in _row_halves(h_ref.shape[0]):
        mix = _dot(oa_ref[rows, :], wa_ref[...]) + _dot(ob_ref[rows, :], wb_ref[...])
        _mix_tail(rows, mix, h_ref, g_ref, b_ref, rw_ref, rb_ref, hout_ref, logit_ref)
    _route_tile(logit_ref, idx_ref, gate_ref, rank_ref, cnt_ref, tm=h_ref.shape[0])


def _mix_odd_kernel(o_ref, wo_ref, h_ref, g_ref, b_ref, rw_ref, rb_ref,
                    hout_ref, idx_ref, gate_ref, rank_ref, cnt_ref, logit_ref):
    for rows in _row_halves(h_ref.shape[0]):
        _mix_tail(rows, _dot(o_ref[rows, :], wo_ref[...]), h_ref, g_ref, b_ref, rw_ref, rb_ref, hout_ref, logit_ref)
    _route_tile(logit_ref, idx_ref, gate_ref, rank_ref, cnt_ref, tm=h_ref.shape[0])


def _mix_call(kernel, name, acts, weights, h, g, b, rw, rb):
    t, d = h.shape
    tm = min(TD_DISPATCH, t)
    row = lambda n: pl.BlockSpec((tm, n), lambda i: (i, 0))
    full = lambda a: pl.BlockSpec(a.shape, lambda i: (0, 0))
    narrow = jax.ShapeDtypeStruct((t, TOP_K), I32)
    return pl.pallas_call(
        kernel,
        grid=(t // tm,),
        in_specs=[row(a.shape[1]) for a in acts] + [full(w) for w in weights]
                 + [row(d), full(g), full(b), full(rw), full(rb)],
        out_specs=[row(d), row(TOP_K), row(TOP_K), row(TOP_K), pl.BlockSpec((1, 1, LANES), lambda i: (i, 0, 0))],
        out_shape=[jax.ShapeDtypeStruct((t, d), F32), narrow, jax.ShapeDtypeStruct((t, TOP_K), F32), narrow,
                   jax.ShapeDtypeStruct((t // tm, 1, LANES), F32)],
        scratch_shapes=[pltpu.VMEM((tm, LANES), F32)],
        compiler_params=_cparams("parallel"),
        name=name,
    )(*acts, *weights, h, g, b, rw, rb)


def _route_tile(logit_ref, idx_ref, gate_ref, rank_ref, cnt_ref, *, tm):
    lane = lax.broadcasted_iota(I32, (tm, LANES), 1)
    lane_f = lane.astype(F32)
    x = jnp.where(lane < N_EXPERTS, logit_ref[...], -jnp.inf)
    vals, hots = [], []
    for k in range(TOP_K):
        m = jnp.max(x, axis=-1, keepdims=True)
        first = jnp.min(jnp.where(x == m, lane_f, float(LANES)), axis=-1, keepdims=True)
        hot = lane_f == first
        x = jnp.where(hot, -jnp.inf, x)
        vals.append(m)
        hots.append(hot)
        idx_ref[:, k:k + 1] = first.astype(I32)
    exps = [jnp.exp(v - vals[0]) for v in vals]
    denom = exps[0] + exps[1] + exps[2] + exps[3]
    for k in range(TOP_K):
        gate_ref[:, k:k + 1] = exps[k] / denom

    hot_sum = (hots[0] | hots[1] | hots[2] | hots[3]).astype(F32)
    r = lax.broadcasted_iota(I32, (tm, tm), 0)
    c = lax.broadcasted_iota(I32, (tm, tm), 1)
    earlier = (c < r).astype(BF16)
    before = _dot(earlier, hot_sum.astype(BF16))
    for k in range(TOP_K):
        rank_ref[:, k:k + 1] = jnp.sum(jnp.where(hots[k], before, 0.0), axis=-1, keepdims=True).astype(I32)
    cnt_ref[0] = jnp.sum(hot_sum, axis=0, keepdims=True)


def _chunked_copy(src_ref, src_off, dst_ref, dst_off, count, largest, sem, start_not_wait):
    chunk = largest
    while chunk >= SUBLANES:
        present = (count & chunk) != 0
        copy = pltpu.make_async_copy(src_ref.at[pl.ds(pl.multiple_of(src_off, SUBLANES), chunk), :],
                                     dst_ref.at[pl.ds(pl.multiple_of(dst_off, SUBLANES), chunk), :], sem)

        @pl.when(jnp.logical_and(present, start_not_wait))
        def _():
            copy.start()

        @pl.when(jnp.logical_and(present, jnp.logical_not(start_not_wait)))
        def _():
            copy.wait()

        step = jnp.where(present, chunk, 0)
        src_off = src_off + step
        dst_off = dst_off + step
        chunk //= 2


def _dispatch_kernel(seg_rows_ref, seg_src_ref, seg_dst_ref, pad_start_ref, pad_count_ref, n_used_ref,
                     lp_ref, x_ref, xs_out, staged_ref, zero_ref, sem, pad_sem, *, td):
    i = pl.program_id(0)
    tile = zero_ref.shape[0]

    @pl.when(i == 0)
    def _():
        zero_ref[...] = jnp.zeros_like(zero_ref)

        def pads(e, start_not_wait):
            _chunked_copy(zero_ref, 0, xs_out, pad_start_ref[e], pad_count_ref[e], tile // 2, pad_sem,
                          start_not_wait)
            return start_not_wait

        def tail(j, start_not_wait):
            present = j >= n_used_ref[0]
            copy = pltpu.make_async_copy(zero_ref, xs_out.at[pl.ds(pl.multiple_of(j * tile, tile), tile), :], pad_sem)

            @pl.when(jnp.logical_and(present, start_not_wait))
            def _():
                copy.start()

            @pl.when(jnp.logical_and(present, jnp.logical_not(start_not_wait)))
            def _():
                copy.wait()
            return start_not_wait

        n_tiles = xs_out.shape[0] // tile
        lax.fori_loop(0, N_EXPERTS, pads, True)
        lax.fori_loop(0, n_tiles, tail, True)
        lax.fori_loop(0, N_EXPERTS, pads, False)
        lax.fori_loop(0, n_tiles, tail, False)

    rows = staged_ref.shape[0]
    r_iota = lax.broadcasted_iota(I32, (rows, td), 0)
    hit = r_iota == lp_ref[0, 0:1, :]
    for k in range(1, TOP_K):
        hit = hit | (r_iota == lp_ref[0, k:k + 1, :])
    staged_ref[...] = _dot(jnp.where(hit, 1.0, 0.0).astype(BF16), x_ref[...].astype(BF16))

    def runs(e, start_not_wait):
        s = i * N_EXPERTS + e
        _chunked_copy(staged_ref, seg_src_ref[s], xs_out, seg_dst_ref[s], seg_rows_ref[s], td, sem, start_not_wait)
        return start_not_wait

    lax.fori_loop(0, N_EXPERTS, runs, True)
    lax.fori_loop(0, N_EXPERTS, runs, False)


def _dispatch(seg_rows, seg_src, seg_dst, pad_start, pad_count, n_used, lp, x, n_rows):
    t, d = x.shape
    td = min(TD_DISPATCH, t)
    staged_rows = td * TOP_K + N_EXPERTS * SUBLANES
    grid_spec = pltpu.PrefetchScalarGridSpec(
        num_scalar_prefetch=6,
        grid=(t // td,),
        in_specs=[pl.BlockSpec((1, TOP_K, td), lambda i, *_: (i, 0, 0)),
                  pl.BlockSpec((td, d), lambda i, *_: (i, 0))],
        out_specs=pl.BlockSpec(memory_space=pl.ANY),
        scratch_shapes=[pltpu.VMEM((staged_rows, d), F32), pltpu.VMEM((TM_FFN, d), F32),
                        pltpu.SemaphoreType.DMA, pltpu.SemaphoreType.DMA],
    )
    return pl.pallas_call(
        functools.partial(_dispatch_kernel, td=td),
        grid_spec=grid_spec,
        out_shape=jax.ShapeDtypeStruct((n_rows, d), F32),
        compiler_params=_cparams("arbitrary"),
        name="moe_dispatch",
    )(seg_rows, seg_src, seg_dst, pad_start, pad_count, n_used, lp, x)


def _ffn_kernel(tile_expert_ref, n_used_ref, xs_ref, wgu_ref, bgu_ref, wd_ref, bd_ref, y_ref, wgu_bf, wd_bf):
    i = pl.program_id(0)
    d_ff = wd_ref.shape[2]
    new_expert = jnp.logical_or(i == 0, tile_expert_ref[i] != tile_expert_ref[jnp.maximum(i - 1, 0)])

    @pl.when(jnp.logical_and(new_expert, i < n_used_ref[0]))
    def _():
        wgu_bf[...] = wgu_ref[0, 0].astype(BF16)
        wd_bf[...] = wd_ref[0, 0].astype(BF16)

    @pl.when(i < n_used_ref[0])
    def _():
        x = xs_ref[...].astype(BF16)
        hgu = _dot(x, wgu_bf[...]) + bgu_ref[0, 0]
        gate = jnp.minimum(hgu[:, :d_ff], SWIGLU_LIMIT)
        up = jnp.clip(hgu[:, d_ff:], -SWIGLU_LIMIT, SWIGLU_LIMIT)
        act = gate * (1.0 / (1.0 + jnp.exp(-SWIGLU_ALPHA * gate))) * (up + 1.0)
        y_ref[...] = _dot(act.astype(BF16), wd_bf[...]) + bd_ref[0, 0]

    @pl.when(i >= n_used_ref[0])
    def _():
        y_ref[...] = jnp.zeros_like(y_ref)


def _expert_ffn(tile_expert, n_used, xs, w_gu, b_gu, w_down, b_down, layer):
    n_rows, d = xs.shape
    depth, e, _, f2 = w_gu.shape
    d_ff = w_down.shape[2]
    tm = TM_FFN
    grid_spec = pltpu.PrefetchScalarGridSpec(
        num_scalar_prefetch=2,
        grid=(n_rows // tm,),
        in_specs=[pl.BlockSpec((tm, d), lambda i, te, nu: (jnp.minimum(i, nu[0] - 1), 0)),
                  pl.BlockSpec((1, 1, d, f2), lambda i, te, nu: (layer, te[i], 0, 0)),
                  pl.BlockSpec((1, 1, 1, f2), lambda i, te, nu: (layer, te[i], 0, 0)),
                  pl.BlockSpec((1, 1, d_ff, d), lambda i, te, nu: (layer, te[i], 0, 0)),
                  pl.BlockSpec((1, 1, 1, d), lambda i, te, nu: (layer, te[i], 0, 0))],
        out_specs=pl.BlockSpec((tm, d), lambda i, te, nu: (i, 0)),
        scratch_shapes=[pltpu.VMEM((d, f2), BF16), pltpu.VMEM((d_ff, d), BF16)],
    )
    return pl.pallas_call(
        _ffn_kernel,
        grid_spec=grid_spec,
        out_shape=jax.ShapeDtypeStruct((n_rows, d), F32),
        compiler_params=_cparams("arbitrary"),
        name="expert_ffn",
    )(tile_expert, n_used, xs, w_gu, b_gu.reshape(depth, e, 1, f2), w_down, b_down.reshape(depth, e, 1, d))


def _combine_kernel(seg_rows_ref, seg_src_ref, seg_dst_ref, lp_ref, gate_ref, h_ref, g_ref, b_ref, y_hbm, out_ref,
                    staged_ref, ffn_ref, sem, *, tc):
    i = pl.program_id(0)

    def runs(e, start_not_wait):
        s = i * N_EXPERTS + e
        _chunked_copy(y_hbm, seg_dst_ref[s], staged_ref, seg_src_ref[s], seg_rows_ref[s], tc, sem, start_not_wait)
        return start_not_wait

    lax.fori_loop(0, N_EXPERTS, runs, True)
    lax.fori_loop(0, N_EXPERTS, runs, False)

    def token(t, _):
        acc = gate_ref[t * TOP_K] * staged_ref[pl.ds(lp_ref[t * TOP_K], 1), :]
        for k in range(1, TOP_K):
            acc = acc + gate_ref[t * TOP_K + k] * staged_ref[pl.ds(lp_ref[t * TOP_K + k], 1), :]
        ffn_ref[pl.ds(t, 1), :] = acc
        return 0

    lax.fori_loop(0, tc, token, 0, unroll=4)
    out_ref[...] = _layer_norm(DEEPNORM_ALPHA * h_ref[...] + ffn_ref[...], g_ref[...], b_ref[...])


def _combine(seg_rows, seg_src, seg_dst, lp_flat, gates_flat, h, g, b, y):
    t, d = h.shape
    tc = min(TD_DISPATCH, t)
    staged_rows = tc * TOP_K + N_EXPERTS * SUBLANES
    grid_spec = pltpu.PrefetchScalarGridSpec(
        num_scalar_prefetch=3,
        grid=(t // tc,),
        in_specs=[pl.BlockSpec((tc * TOP_K,), lambda i, *_: (i,), memory_space=pltpu.SMEM),
                  pl.BlockSpec((tc * TOP_K,), lambda i, *_: (i,), memory_space=pltpu.SMEM),
                  pl.BlockSpec((tc, d), lambda i, *_: (i, 0)),
                  pl.BlockSpec((1, d), lambda i, *_: (0, 0)),
                  pl.BlockSpec((1, d), lambda i, *_: (0, 0)),
                  pl.BlockSpec(memory_space=pl.ANY)],
        out_specs=pl.BlockSpec((tc, d), lambda i, *_: (i, 0)),
        scratch_shapes=[pltpu.VMEM((staged_rows, d), F32), pltpu.VMEM((tc, d), F32), pltpu.SemaphoreType.DMA],
    )
    return pl.pallas_call(
        functools.partial(_combine_kernel, tc=tc),
        grid_spec=grid_spec,
        out_shape=jax.ShapeDtypeStruct((t, d), F32),
        compiler_params=_cparams("arbitrary"),
        name="moe_combine",
    )(seg_rows, seg_src, seg_dst, lp_flat, gates_flat, h, g, b, y)


def _moe(h, routing, w_gu, b_gu, w_down, b_down, g, b, layer):
    t = h.shape[0]
    td = min(TD_DISPATCH, t)
    n_tok_tiles = t // td
    idx, gates, rank, counts = routing
    counts = counts[:, 0, :N_EXPERTS].astype(I32)
    seg = (counts + SUBLANES - 1) // SUBLANES * SUBLANES
    rows = jnp.sum(seg, axis=0)
    tiles = (rows + TM_FFN - 1) // TM_FFN
    tile_end = jnp.cumsum(tiles)
    start = (tile_end - tiles) * TM_FFN
    seg_dst = start[None, :] + jnp.cumsum(seg, axis=0) - seg
    seg_src = jnp.cumsum(seg, axis=1) - seg
    n_tiles = -(-(t * TOP_K + n_tok_tiles * N_EXPERTS * (SUBLANES - 1)) // TM_FFN) + N_EXPERTS
    tile_ids = jnp.arange(n_tiles, dtype=I32)
    tile_expert = jnp.minimum(jnp.sum((tile_end[None, :] <= tile_ids[:, None]).astype(I32), axis=1), N_EXPERTS - 1)
    chosen = idx.reshape(n_tok_tiles, td, TOP_K, 1) == jnp.arange(N_EXPERTS, dtype=I32)
    pick = lambda table: jnp.sum(jnp.where(chosen, table[:, None, None, :], 0), axis=-1)
    rank = rank.reshape(n_tok_tiles, td, TOP_K)
    lp = (pick(seg_src) + rank).astype(I32)
    n_used = tile_end[-1:].astype(I32)
    flat = lambda a: a.reshape(-1).astype(I32)
    xs = _dispatch(flat(seg), flat(seg_src), flat(seg_dst), start + rows, tiles * TM_FFN - rows, n_used,
                   jnp.swapaxes(lp, 1, 2), h, n_tiles * TM_FFN)
    y = _expert_ffn(tile_expert, n_used, xs, w_gu, b_gu, w_down, b_down, layer)
    return _combine(flat(seg), flat(seg_src), flat(seg_dst), flat(lp), gates.reshape(-1), h, g, b, y)


def _router_params(router_w, router_b):
    rw = jnp.pad(router_w, ((0, 0), (0, LANES - N_EXPERTS))).astype(BF16)
    rb = jnp.pad(router_b, (0, LANES - N_EXPERTS)).reshape(1, LANES)
    return rw, rb


def _alibi_slopes(n):
    return 2.0 ** (-8.0 * jnp.arange(1, n + 1, dtype=F32) / n)


def _even_layer(h, bsz, seq, w_in, w_out, lq1, lk1, lq2, lk2, subw, layer, g, b, rw, rb):
    proj = _project(h, [w_in.astype(BF16)], [BF16])[0].reshape(bsz, seq, -1)
    o_a = _sb_attention(proj, bsz, seq)
    lam_init = 0.8 - 0.6 * math.exp(-0.3 * layer)
    row = lambda v: v.reshape(1, -1)
    o_b = _diff_attention(proj, _alibi_slopes(DIFF_HEADS), row(lq1), row(lk1), row(lq2), row(lk2), row(subw),
                          bsz, seq, lam_init)
    t = bsz * seq
    sbw = SB_HEADS * HEAD_DIM
    wo = w_out.astype(BF16)
    return _mix_call(_mix_even_kernel, "mix_even", [o_a.reshape(t, -1), o_b.reshape(t, -1)],
                     [wo[:sbw], wo[sbw:]], h, row(g), row(b), rw, rb)


def _odd_layer(h, bsz, seq, w_in, kvw, w_uv, w_out, g, b, rw, rb):
    qw = DSA_HEADS * DSA_LATENT
    c0, c1 = qw, qw + DSA_LATENT
    i0, i1 = c1, c1 + IDX_HEADS * IDX_DIM
    k1 = i1 + IDX_DIM
    w_main = jnp.concatenate([w_in[:, :qw], w_in[:, i0:i1], w_in[:, i1:k1], w_in[:, i1:k1]], axis=1).astype(BF16)
    w_small = jnp.pad(jnp.concatenate([w_in[:, c0:c1], w_in[:, k1:]], axis=1),
                      ((0, 0), (0, LANES - IDX_HEADS))).astype(BF16)
    proj, small = [p.reshape(bsz, seq, -1) for p in _project(h, [w_main, w_small], [BF16, F32])]
    o = _dsa_attention(proj, small, kvw.reshape(1, -1), w_uv, bsz, seq)
    row = lambda v: v.reshape(1, -1)
    return _mix_call(_mix_odd_kernel, "mix_odd", [o.reshape(bsz * seq, -1)], [w_out.astype(BF16)],
                     h, row(g), row(b), rw, rb)


def kernel(x, ev_w_in, ev_w_out, ev_lambda_q1, ev_lambda_k1, ev_lambda_q2, ev_lambda_k2, ev_subln_w, od_w_in, od_kv_norm_w, od_w_uv, od_w_out, ln_mix_g, ln_mix_b, router_w, router_b, exp_w_gu, exp_b_gu, exp_w_down, exp_b_down, ln_ffn_g, ln_ffn_b):
    bsz, seq, d = x.shape
    h = x.reshape(bsz * seq, d)
    for layer in range(ln_mix_g.shape[0]):
        j = layer // 2
        rw, rb = _router_params(router_w[layer], router_b[layer])
        if layer % 2 == 0:
            h, *routing = _even_layer(h, bsz, seq, ev_w_in[j], ev_w_out[j], ev_lambda_q1[j], ev_lambda_k1[j],
                                    ev_lambda_q2[j], ev_lambda_k2[j], ev_subln_w[j], layer,
                                    ln_mix_g[layer], ln_mix_b[layer], rw, rb)
        else:
            h, *routing = _odd_layer(h, bsz, seq, od_w_in[j], od_kv_norm_w[j], od_w_uv[j], od_w_out[j],
                                   ln_mix_g[layer], ln_mix_b[layer], rw, rb)
        h = _moe(h, routing, exp_w_gu, exp_b_gu, exp_w_down, exp_b_down,
                 ln_ffn_g[layer].reshape(1, -1), ln_ffn_b[layer].reshape(1, -1), layer)
    return h.reshape(bsz, seq, d)
```
